```python
import math
import jax, jax.numpy as jnp
from jax import lax
import numpy as np

D_MODEL = 1024
BATCH = 2
SEQ = 8192
DEPTH = 2

N_MIXERS = 2

DN_QK_HEADS = 8
DN_V_HEADS = 16
DN_HEAD_DIM = 128
DN_CONV = 4
DN_CHUNK = 64
DN_KEY_DIM = DN_QK_HEADS * DN_HEAD_DIM
DN_VAL_DIM = DN_V_HEADS * DN_HEAD_DIM
DN_CONV_DIM = 2 * DN_KEY_DIM + DN_VAL_DIM
DN_PROJ = DN_CONV_DIM + DN_VAL_DIM + 2 * DN_V_HEADS

CF_KERNEL = 31
CF_INNER = D_MODEL

N_EXPERTS = 32
TOP_K = 4
D_FF = D_MODEL
SWIGLU_LIMIT = 7.0
SWIGLU_ALPHA = 1.702
EXPERT_BLOCK = 128

LN_EPS = 1e-5
RMS_EPS = 1e-6
L2_EPS = 1e-6
DEEPNORM_ALPHA = (2 * DEPTH) ** 0.25
DEEPNORM_BETA = (8 * DEPTH) ** -0.25

kernel_name = 'hybrid_deltanet_conformer_moe_deepnorm_adaln'


def layer_norm(x, g, b):
    xf = x.astype(jnp.float32)
    mu = jnp.mean(xf, -1, keepdims=True)
    var = jnp.mean(jnp.square(xf - mu), -1, keepdims=True)
    return ((xf - mu) * lax.rsqrt(var + LN_EPS) * g + b).astype(x.dtype)


def modulate(x, shift, scale):
    return x * (1.0 + scale[:, None, :]) + shift[:, None, :]


def causal_depthwise_conv(x, w):
    K, C = w.shape
    return lax.conv_general_dilated(
        x, w[:, None, :].astype(x.dtype), window_strides=(1,), padding=[(K - 1, 0)],
        dimension_numbers=('NWC', 'WIO', 'NWC'), feature_group_count=C)


def gated_delta_rule(q, k, v, g, beta):
    f32 = jnp.float32
    B, S, H, DK = q.shape
    DV = v.shape[-1]
    C = DN_CHUNK
    N = S // C
    q, k, v, g, beta = (t.astype(f32) for t in (q, k, v, g, beta))
    q = q * lax.rsqrt(jnp.sum(q * q, -1, keepdims=True) + L2_EPS) * (DK ** -0.5)
    k = k * lax.rsqrt(jnp.sum(k * k, -1, keepdims=True) + L2_EPS)

    def chunks(t):
        t = t.reshape((B, N, C, H) + t.shape[3:])
        return jnp.moveaxis(t, 3, 1)

    q, k, v, beta = chunks(q), chunks(k), chunks(v), chunks(beta)
    g = jnp.cumsum(chunks(g), axis=-1)
    causal = jnp.tril(jnp.ones((C, C), dtype=bool))
    strict = jnp.tril(jnp.ones((C, C), dtype=bool), -1)
    decay = jnp.exp(jnp.where(causal, g[..., :, None] - g[..., None, :], -jnp.inf))
    kb = k * beta[..., None]
    a = jnp.where(strict, jnp.einsum('bhnid,bhnjd->bhnij', kb, k) * decay, 0.0)
    lower = a + jnp.eye(C, dtype=f32)
    rhs = jnp.concatenate([v * beta[..., None], kb * jnp.exp(g)[..., None]], axis=-1)
    sol = lax.linalg.triangular_solve(lower, rhs, left_side=True, lower=True, unit_diagonal=True)
    u, w = sol[..., :DV], sol[..., DV:]
    qk = jnp.where(causal, jnp.einsum('bhnid,bhnjd->bhnij', q, k) * decay, 0.0)
    qg = q * jnp.exp(g)[..., None]
    kd = k * jnp.exp(g[..., -1:] - g)[..., None]
    g_last = jnp.exp(g[..., -1])

    def step(state, xs):
        qg_i, qk_i, u_i, w_i, kd_i, gl_i = xs
        v_new = u_i - jnp.einsum('bhck,bhkv->bhcv', w_i, state)
        o = jnp.einsum('bhck,bhkv->bhcv', qg_i, state) + jnp.einsum('bhij,bhjv->bhiv', qk_i, v_new)
        state = state * gl_i[..., None, None] + jnp.einsum('bhck,bhcv->bhkv', kd_i, v_new)
        return state, o

    xs = tuple(jnp.moveaxis(t, 2, 0) for t in (qg, qk, u, w, kd, g_last))
    _, o = lax.scan(step, jnp.zeros((B, H, DK, DV), f32), xs)
    return jnp.transpose(o, (1, 0, 3, 2, 4)).reshape(B, S, H, DV)


def gated_deltanet(h, in_w, conv_w, A_log, dt_bias, onorm_w, out_w):
    B, S, _ = h.shape
    proj = h @ in_w
    qkv, z, b, a = jnp.split(proj, [DN_CONV_DIM, DN_CONV_DIM + DN_VAL_DIM,
                                    DN_CONV_DIM + DN_VAL_DIM + DN_V_HEADS], axis=-1)
    qkv = jax.nn.silu(causal_depthwise_conv(qkv, conv_w))
    q, k, v = jnp.split(qkv, [DN_KEY_DIM, 2 * DN_KEY_DIM], axis=-1)
    rep = DN_V_HEADS // DN_QK_HEADS
    q = jnp.repeat(q.reshape(B, S, DN_QK_HEADS, DN_HEAD_DIM), rep, axis=2)
    k = jnp.repeat(k.reshape(B, S, DN_QK_HEADS, DN_HEAD_DIM), rep, axis=2)
    v = v.reshape(B, S, DN_V_HEADS, DN_HEAD_DIM)
    beta = jax.nn.sigmoid(b.astype(jnp.float32))
    g = -jnp.exp(A_log.astype(jnp.float32)) * jax.nn.softplus(a.astype(jnp.float32) + dt_bias)
    o = gated_delta_rule(q, k, v, g, beta)
    o = o * lax.rsqrt(jnp.mean(o * o, -1, keepdims=True) + RMS_EPS) * onorm_w
    o = o * jax.nn.silu(z.astype(jnp.float32).reshape(B, S, DN_V_HEADS, DN_HEAD_DIM))
    return o.reshape(B, S, DN_VAL_DIM).astype(h.dtype) @ out_w


def conformer_conv(h, pw1_w, pw1_b, dw_w, dw_b, ln_g, ln_b, pw2_w, pw2_b):
    p = h @ pw1_w + pw1_b
    u = p[..., :CF_INNER] * jax.nn.sigmoid(p[..., CF_INNER:])
    u = causal_depthwise_conv(u, dw_w) + dw_b
    u = jax.nn.silu(layer_norm(u, ln_g, ln_b))
    return u @ pw2_w + pw2_b


def _padded_rows(n_assign):
    bound = n_assign + N_EXPERTS * (EXPERT_BLOCK - 1)
    return -(-bound // EXPERT_BLOCK) * EXPERT_BLOCK


def moe(h, router_w, router_b, w1, b1, w2, b2):
    B, S, D = h.shape
    T = B * S
    ht = h.reshape(T, D)
    logits = (ht @ router_w + router_b).astype(jnp.float32)
    top_val, top_idx = lax.top_k(logits, TOP_K)
    gates = jax.nn.softmax(top_val, axis=-1)
    A = T * TOP_K
    e_flat = top_idx.reshape(A)
    tok_flat = jnp.arange(A, dtype=jnp.int32) // TOP_K
    order = jnp.argsort(e_flat)
    e_sorted = e_flat[order]
    counts = jnp.zeros((N_EXPERTS,), jnp.int32).at[e_flat].add(1)
    padded = (counts + EXPERT_BLOCK - 1) // EXPERT_BLOCK * EXPERT_BLOCK
    starts = jnp.cumsum(counts) - counts
    pends = jnp.cumsum(padded)
    pstarts = pends - padded
    dest = pstarts[e_sorted] + jnp.arange(A, dtype=jnp.int32) - starts[e_sorted]
    P = _padded_rows(A)
    nb = P // EXPERT_BLOCK
    row_tok = jnp.zeros((P,), jnp.int32).at[dest].set(tok_flat[order])
    row_w = jnp.zeros((P,), jnp.float32).at[dest].set(gates.reshape(A)[order])
    block_e = jnp.minimum(
        jnp.searchsorted(pends, jnp.arange(nb, dtype=jnp.int32) * EXPERT_BLOCK, side='right'),
        N_EXPERTS - 1)
    xb = ht[row_tok].reshape(nb, EXPERT_BLOCK, D)

    def expert_block(args):
        xg, e = args
        hh = xg @ w1[e] + b1[e]
        x_glu, x_lin = hh[..., :D_FF], hh[..., D_FF:]
        x_glu = jnp.minimum(x_glu, SWIGLU_LIMIT)
        x_lin = jnp.clip(x_lin, -SWIGLU_LIMIT, SWIGLU_LIMIT)
        act = x_glu * jax.nn.sigmoid(SWIGLU_ALPHA * x_glu) * (x_lin + 1.0)
        return act @ w2[e] + b2[e]

    yb = lax.map(expert_block, (xb, block_e)).reshape(P, D)
    y = jax.ops.segment_sum(yb * row_w[:, None], row_tok, num_segments=T)
    return y.reshape(B, S, D).astype(h.dtype)


def setup_inputs(seed: int = 0) -> dict:
    key = jax.random.key(seed)
    ks = iter(jax.random.split(key, 32))
    n_dn = (DEPTH + N_MIXERS - 1) // N_MIXERS
    n_cf = DEPTH // N_MIXERS

    def nrm(shape, scale):
        return jax.random.normal(next(ks), shape, jnp.float32) * scale

    def gain(shape):
        return 1.0 + nrm(shape, 0.02)

    d = D_MODEL
    inputs = {}
    inputs['x'] = nrm((BATCH, SEQ, d), 1.0)
    inputs['c'] = nrm((BATCH, d), 1.0)
    inputs['ada_w'] = nrm((DEPTH, d, 6 * d), 0.5 * d ** -0.5)
    inputs['ada_b'] = nrm((DEPTH, 6 * d), 0.02)
    inputs['dn_in_w'] = nrm((n_dn, d, DN_PROJ), d ** -0.5)
    inputs['dn_conv_w'] = nrm((n_dn, DN_CONV, DN_CONV_DIM), DN_CONV ** -0.5)
    inputs['dn_A_log'] = jnp.log(jax.random.uniform(next(ks), (n_dn, DN_V_HEADS), jnp.float32, 1.0, 16.0))
    dt = jnp.exp(jax.random.uniform(next(ks), (n_dn, DN_V_HEADS), jnp.float32,
                                    math.log(1e-3), math.log(1e-1)))
    inputs['dn_dt_bias'] = dt + jnp.log(-jnp.expm1(-dt))
    inputs['dn_onorm_w'] = gain((n_dn, DN_HEAD_DIM))
    inputs['dn_out_w'] = nrm((n_dn, DN_VAL_DIM, d), DN_VAL_DIM ** -0.5 * DEEPNORM_BETA)
    inputs['cf_pw1_w'] = nrm((n_cf, d, 2 * CF_INNER), d ** -0.5)
    inputs['cf_pw1_b'] = nrm((n_cf, 2 * CF_INNER), 0.02)
    inputs['cf_dw_w'] = nrm((n_cf, CF_KERNEL, CF_INNER), CF_KERNEL ** -0.5)
    inputs['cf_dw_b'] = nrm((n_cf, CF_INNER), 0.02)
    inputs['cf_ln_g'] = gain((n_cf, CF_INNER))
    inputs['cf_ln_b'] = nrm((n_cf, CF_INNER), 0.02)
    inputs['cf_pw2_w'] = nrm((n_cf, CF_INNER, d), CF_INNER ** -0.5 * DEEPNORM_BETA)
    inputs['cf_pw2_b'] = nrm((n_cf, d), 0.02)
    inputs['ln1_g'] = gain((DEPTH, d))
    inputs['ln1_b'] = nrm((DEPTH, d), 0.02)
    inputs['router_w'] = nrm((DEPTH, d, N_EXPERTS), d ** -0.5)
    inputs['router_b'] = nrm((DEPTH, N_EXPERTS), 0.01)
    inputs['e_w1'] = nrm((DEPTH, N_EXPERTS, d, 2 * D_FF), d ** -0.5)
    inputs['e_b1'] = nrm((DEPTH, N_EXPERTS, 2 * D_FF), 0.02)
    inputs['e_w2'] = nrm((DEPTH, N_EXPERTS, D_FF, d), D_FF ** -0.5 * DEEPNORM_BETA)
    inputs['e_b2'] = nrm((DEPTH, N_EXPERTS, d), 0.02)
    inputs['ln2_g'] = gain((DEPTH, d))
    inputs['ln2_b'] = nrm((DEPTH, d), 0.02)
    return inputs


def reference(x, c, ada_w, ada_b, dn_in_w, dn_conv_w, dn_A_log, dn_dt_bias, dn_onorm_w, dn_out_w,
              cf_pw1_w, cf_pw1_b, cf_dw_w, cf_dw_b, cf_ln_g, cf_ln_b, cf_pw2_w, cf_pw2_b,
              ln1_g, ln1_b, router_w, router_b, e_w1, e_b1, e_w2, e_b2, ln2_g, ln2_b):
    cond = jax.nn.silu(c)
    for i in range(DEPTH):
        mod = cond @ ada_w[i] + ada_b[i]
        sh1, sc1, gt1, sh2, sc2, gt2 = jnp.split(mod, 6, axis=-1)
        h = modulate(x, sh1, sc1)
        j = i // N_MIXERS
        if i % N_MIXERS == 0:
            y = gated_deltanet(h, dn_in_w[j], dn_conv_w[j], dn_A_log[j], dn_dt_bias[j],
                               dn_onorm_w[j], dn_out_w[j])
        else:
            y = conformer_conv(h, cf_pw1_w[j], cf_pw1_b[j], cf_dw_w[j], cf_dw_b[j],
                               cf_ln_g[j], cf_ln_b[j], cf_pw2_w[j], cf_pw2_b[j])
        x = layer_norm(DEEPNORM_ALPHA * x + (1.0 + gt1)[:, None, :] * y, ln1_g[i], ln1_b[i])
        h = modulate(x, sh2, sc2)
        y = moe(h, router_w[i], router_b[i], e_w1[i], e_b1[i], e_w2[i], e_b2[i])
        x = layer_norm(DEEPNORM_ALPHA * x + (1.0 + gt2)[:, None, :] * y, ln2_g[i], ln2_b[i])
    return x
```

```python
import functools

import jax
import jax.numpy as jnp
from jax import lax
from jax.experimental import pallas as pl
from jax.experimental.pallas import tpu as pltpu

F32 = jnp.float32
BF16 = jnp.bfloat16
HIGHEST = lax.Precision.HIGHEST

DEPTH = 2
DN_QK_HEADS = 8
DN_V_HEADS = 16
DN_HEAD_DIM = 128
DN_CONV = 4
DN_CHUNK = 64
CF_KERNEL = 31
N_EXPERTS = 32
TOP_K = 4
SWIGLU_LIMIT = 7.0
SWIGLU_ALPHA = 1.702
LN_EPS = 1e-5
RMS_EPS = 1e-6
L2_EPS = 1e-6
DEEPNORM_ALPHA = (2 * DEPTH) ** 0.25

LANES = 128
SUBLANES = 8
BF16_SUBLANES = 16
VMEM_LIMIT = 56 * 1024 * 1024

ADA_TN = 1536
PROJ_TM = 1024
PROJ_TN = 1024
PREP_TS = 512
CHUNKS_PER_STEP = 4
POST_TM = 512
RANK_TT = 512
DISPATCH_TT = 1024
EXPERT_TM = 256
COMBINE_TT = 256
CF_TM = 512
CF_HALO = 32

NEG_BIG = -1e30


def _cparams(sem):
    return pltpu.CompilerParams(dimension_semantics=sem, vmem_limit_bytes=VMEM_LIMIT)


def _sigmoid(x):
    return jax.nn.sigmoid(x)


def _layer_norm(v, g, b):
    mu = jnp.mean(v, -1, keepdims=True)
    d = v - mu
    var = jnp.mean(d * d, -1, keepdims=True)
    return d * lax.rsqrt(var + LN_EPS) * g + b


def _rows_to_tiles(o3_ref, val):
    for s in range(val.shape[1] // LANES):
        o3_ref[:, s, :] = val[:, s * LANES:(s + 1) * LANES]


def _tiles_to_rows(x3_ref):
    return jnp.concatenate([x3_ref[:, s, :] for s in range(x3_ref.shape[1])], axis=1)


def _ada_kernel(c_ref, w_ref, b_ref, o_ref):
    c = c_ref[...]
    cond = c * _sigmoid(c)
    o_ref[0] = jnp.dot(cond, w_ref[0], precision=HIGHEST, preferred_element_type=F32) + b_ref[0]


def _ada_ln(c, ada_w, ada_b):
    depth, d, n = ada_w.shape
    bsz = c.shape[0]
    c_pad = jnp.zeros((SUBLANES, d), F32).at[:bsz].set(c)
    out = pl.pallas_call(
        _ada_kernel,
        grid=(depth, n // ADA_TN),
        in_specs=[
            pl.BlockSpec((SUBLANES, d), lambda i, j: (0, 0)),
            pl.BlockSpec((1, d, ADA_TN), lambda i, j: (i, 0, j)),
            pl.BlockSpec((1, 1, ADA_TN), lambda i, j: (i, 0, j)),
        ],
        out_specs=pl.BlockSpec((1, SUBLANES, ADA_TN), lambda i, j: (i, 0, j)),
        out_shape=jax.ShapeDtypeStruct((depth, SUBLANES, n), F32),
        compiler_params=_cparams(("parallel", "parallel")),
        name="ada_ln",
    )(c_pad, ada_w, ada_b.reshape(depth, 1, n))
    return out[:, :bsz].reshape(depth, bsz, 6, d)


def _inproj_kernel(x_ref, mod_ref, w_ref, wba_ref, proj_ref, ba_ref, h_scr):
    @pl.when(pl.program_id(2) == 0)
    def _():
        h = x_ref[0] * (1.0 + mod_ref[0, 1:2, :]) + mod_ref[0, 0:1, :]
        h_scr[...] = h.astype(BF16)
        ba_ref[0] = jnp.dot(h, wba_ref[...], precision=HIGHEST, preferred_element_type=F32)

    proj_ref[0] = jnp.dot(h_scr[...], w_ref[...], preferred_element_type=F32).astype(BF16)


def _dn_in_proj(x, mod, in_w):
    bsz, seq, d = x.shape
    n_main = in_w.shape[1] - 2 * DN_V_HEADS
    w_main = in_w[:, :n_main].astype(BF16)
    w_ba = jnp.zeros((d, LANES), F32).at[:, :2 * DN_V_HEADS].set(in_w[:, n_main:])
    tm = min(PROJ_TM, seq)
    return pl.pallas_call(
        _inproj_kernel,
        grid=(bsz, seq // tm, n_main // PROJ_TN),
        in_specs=[
            pl.BlockSpec((1, tm, d), lambda b, i, j: (b, i, 0)),
            pl.BlockSpec((1, 6, d), lambda b, i, j: (b, 0, 0)),
            pl.BlockSpec((d, PROJ_TN), lambda b, i, j: (0, j)),
            pl.BlockSpec((d, LANES), lambda b, i, j: (0, 0)),
        ],
        out_specs=[
            pl.BlockSpec((1, tm, PROJ_TN), lambda b, i, j: (b, i, j)),
            pl.BlockSpec((1, tm, LANES), lambda b, i, j: (b, i, 0)),
        ],
        out_shape=[
            jax.ShapeDtypeStruct((bsz, seq, n_main), BF16),
            jax.ShapeDtypeStruct((bsz, seq, LANES), F32),
        ],
        scratch_shapes=[pltpu.VMEM((tm, d), BF16)],
        compiler_params=_cparams(("parallel", "parallel", "arbitrary")),
        name="dn_in_proj",
    )(x, mod, w_main, w_ba)


def _dn_prep_kernel(q_ref, k_ref, v_ref, qh_ref, kh_ref, vh_ref, cw_ref, ba_ref, alog_ref, dt_ref,
                    qo_ref, ko_ref, vo_ref, g_ref, scr):
    ts = q_ref.shape[1]
    halo = qh_ref.shape[1]
    keep = (pl.program_id(1) > 0).astype(F32)

    def conv_silu(x_ref, h_ref, c0):
        width = x_ref.shape[2]
        scr[0:halo, 0:width] = h_ref[0].astype(F32) * keep
        scr[halo:halo + ts, 0:width] = x_ref[0].astype(F32)
        acc = None
        for j in range(DN_CONV):
            off = halo - (DN_CONV - 1) + j
            term = scr[off:off + ts, 0:width] * cw_ref[j:j + 1, c0:c0 + width]
            acc = term if acc is None else acc + term
        return acc * _sigmoid(acc)

    def l2norm_store(o_ref, x, scale):
        for h in range(x.shape[1] // DN_HEAD_DIM):
            xh = x[:, h * DN_HEAD_DIM:(h + 1) * DN_HEAD_DIM]
            ss = jnp.sum(xh * xh, -1, keepdims=True)
            o_ref[0, :, h * DN_HEAD_DIM:(h + 1) * DN_HEAD_DIM] = (
                xh * lax.rsqrt(ss + L2_EPS) * scale).astype(o_ref.dtype)

    kd = q_ref.shape[2]
    l2norm_store(qo_ref, conv_silu(q_ref, qh_ref, 0), DN_HEAD_DIM ** -0.5)
    l2norm_store(ko_ref, conv_silu(k_ref, kh_ref, kd), 1.0)
    vo_ref[0] = conv_silu(v_ref, vh_ref, 2 * kd).astype(vo_ref.dtype)

    ba = ba_ref[0]
    beta = _sigmoid(ba)
    zz = ba + dt_ref[...]
    softplus = jnp.maximum(zz, 0.0) + jnp.log1p(jnp.exp(-jnp.abs(zz)))
    g = -jnp.exp(alog_ref[...]) * softplus
    r = lax.broadcasted_iota(jnp.int32, (ts, ts), 0)
    c = lax.broadcasted_iota(jnp.int32, (ts, ts), 1)
    in_chunk_tril = ((r // DN_CHUNK == c // DN_CHUNK) & (c <= r)).astype(F32)
    gcum = jnp.dot(in_chunk_tril, g, precision=HIGHEST, preferred_element_type=F32)
    lane = lax.broadcasted_iota(jnp.int32, ba.shape, 1)
    g_ref[0] = jnp.where(lane < DN_V_HEADS, beta, gcum)


def _dn_prep(proj, ba, conv_w, a_log, dt_bias):
    bsz, seq, _ = proj.shape
    kd = DN_QK_HEADS * DN_HEAD_DIM
    vd = DN_V_HEADS * DN_HEAD_DIM
    ts = min(PREP_TS, seq)
    halo = BF16_SUBLANES
    hb = ts // halo
    alog_row = jnp.zeros((1, LANES), F32).at[0, DN_V_HEADS:2 * DN_V_HEADS].set(a_log)
    dt_row = jnp.zeros((1, LANES), F32).at[0, DN_V_HEADS:2 * DN_V_HEADS].set(dt_bias)

    def halo_map(col):
        return lambda b, i: (b, jnp.maximum(i * hb - 1, 0), col)

    return pl.pallas_call(
        _dn_prep_kernel,
        grid=(bsz, seq // ts),
        in_specs=[
            pl.BlockSpec((1, ts, kd), lambda b, i: (b, i, 0)),
            pl.BlockSpec((1, ts, kd), lambda b, i: (b, i, 1)),
            pl.BlockSpec((1, ts, vd), lambda b, i: (b, i, 1)),
            pl.BlockSpec((1, halo, kd), halo_map(0)),
            pl.BlockSpec((1, halo, kd), halo_map(1)),
            pl.BlockSpec((1, halo, vd), halo_map(1)),
            pl.BlockSpec((DN_CONV, 2 * kd + vd), lambda b, i: (0, 0)),
            pl.BlockSpec((1, ts, LANES), lambda b, i: (b, i, 0)),
            pl.BlockSpec((1, LANES), lambda b, i: (0, 0)),
            pl.BlockSpec((1, LANES), lambda b, i: (0, 0)),
        ],
        out_specs=[
            pl.BlockSpec((1, ts, kd), lambda b, i: (b, i, 0)),
            pl.BlockSpec((1, ts, kd), lambda b, i: (b, i, 0)),
            pl.BlockSpec((1, ts, vd), lambda b, i: (b, i, 0)),
            pl.BlockSpec((1, ts, LANES), lambda b, i: (b, i, 0)),
        ],
        out_shape=[
            jax.ShapeDtypeStruct((bsz, seq, kd), BF16),
            jax.ShapeDtypeStruct((bsz, seq, kd), BF16),
            jax.ShapeDtypeStruct((bsz, seq, vd), BF16),
            jax.ShapeDtypeStruct((bsz, seq, LANES), F32),
        ],
        scratch_shapes=[pltpu.VMEM((ts + halo, vd), F32)],
        compiler_params=_cparams(("parallel", "arbitrary")),
        name="dn_prep",
    )(proj, proj, proj, proj, proj, proj, conv_w, ba, alog_row, dt_row)


def _bdot(a, b):
    return jnp.dot(a.astype(BF16), b.astype(BF16), preferred_element_type=F32)


def _unit_lower_inverse(a_strict, eye):
    c = a_strict.shape[0]
    b_pow = -a_strict
    p = eye + b_pow
    b_pow = _bdot(b_pow, b_pow)
    span = 2
    while 2 * span < c:
        both = _bdot(b_pow, jnp.concatenate([b_pow, p], axis=1))
        p = p + both[:, c:]
        b_pow = both[:, :c]
        span *= 2
    return p + _bdot(b_pow, p)


def _dn_chunk_kernel(q_ref, k_ref, v_ref, z_ref, g_ref, gt_ref, ow_ref, o_ref, s_ref):
    c_len = DN_CHUNK
    dh = DN_HEAD_DIM
    rep = DN_V_HEADS // DN_QK_HEADS

    @pl.when(pl.program_id(1) == 0)
    def _():
        s_ref[...] = jnp.zeros(s_ref.shape, F32)

    ri = lax.broadcasted_iota(jnp.int32, (c_len, c_len), 0)
    ci = lax.broadcasted_iota(jnp.int32, (c_len, c_len), 1)
    causal = ci <= ri
    strict = ci < ri
    eye = (ci == ri).astype(F32)
    onorm = ow_ref[...]

    def chunk(c, carry):
        r0 = pl.multiple_of(c * c_len, c_len)
        rows = pl.ds(r0, c_len)
        gcols = g_ref[0, rows, :]
        grows = gt_ref[0, c]
        for hq in range(DN_QK_HEADS):
            cols_q = slice(hq * dh, (hq + 1) * dh)
            qn = q_ref[0, rows, cols_q]
            kn = k_ref[0, rows, cols_q]
            qk_kk = lax.dot_general(jnp.concatenate([qn, kn], axis=0), kn,
                                    (((1,), (1,)), ((), ())), preferred_element_type=F32)
            qk = qk_kk[:c_len]
            kk = qk_kk[c_len:]
            qf = qn.astype(F32)
            kf = kn.astype(F32)
            for r in range(rep):
                h = hq * rep + r
                cols_v = slice(h * dh, (h + 1) * dh)
                beta_c = gcols[:, h:h + 1]
                g_c = gcols[:, DN_V_HEADS + h:DN_V_HEADS + h + 1]
                g_r = grows[DN_V_HEADS + h:DN_V_HEADS + h + 1, :]
                g_last = g_c[c_len - 1:c_len, :]
                decay = jnp.where(causal, jnp.exp(g_c - g_r), 0.0)
                a = jnp.where(strict, beta_c * kk * decay, 0.0)
                t_inv = _unit_lower_inverse(a, eye)
                eg = jnp.exp(g_c)
                vf = v_ref[0, rows, cols_v].astype(F32)
                rhs = jnp.concatenate([vf * beta_c, kf * (beta_c * eg)], axis=1)
                sol = _bdot(t_inv, rhs)
                u = sol[:, :dh]
                w = sol[:, dh:]
                qkm = jnp.where(causal, qk * decay, 0.0)
                qg = qf * eg
                kdec = kf * jnp.exp(g_last - g_c)
                state = s_ref[h]
                ws = _bdot(jnp.concatenate([w, qg], axis=0), state)
                v_new = u - ws[:c_len]
                o = ws[c_len:] + _bdot(qkm, v_new)
                s_ref[h] = state * jnp.exp(g_last) + lax.dot_general(
                    kdec.astype(BF16), v_new.astype(BF16), (((0,), (0,)), ((), ())),
                    preferred_element_type=F32)
                o = o * lax.rsqrt(jnp.mean(o * o, -1, keepdims=True) + RMS_EPS) * onorm
                zf = z_ref[0, rows, cols_v].astype(F32)
                o_ref[0, rows, cols_v] = (o * (zf * _sigmoid(zf))).astype(o_ref.dtype)
        return carry

    lax.fori_loop(0, q_ref.shape[1] // c_len, chunk, 0)


def _dn_chunk(qn, kn, vv, proj, gates, onorm_w):
    bsz, seq, kd = qn.shape
    vd = vv.shape[2]
    n_chunks = seq // DN_CHUNK
    cb = min(CHUNKS_PER_STEP, n_chunks)
    rows = cb * DN_CHUNK
    gates_t = jnp.swapaxes(gates[:, :, :2 * DN_V_HEADS].reshape(bsz, n_chunks, DN_CHUNK, 2 * DN_V_HEADS), 2, 3)
    z_col = (2 * kd + vd) // vd
    return pl.pallas_call(
        _dn_chunk_kernel,
        grid=(bsz, n_chunks // cb),
        in_specs=[
            pl.BlockSpec((1, rows, kd), lambda b, n: (b, n, 0)),
            pl.BlockSpec((1, rows, kd), lambda b, n: (b, n, 0)),
            pl.BlockSpec((1, rows, vd), lambda b, n: (b, n, 0)),
            pl.BlockSpec((1, rows, vd), lambda b, n: (b, n, z_col)),
            pl.BlockSpec((1, rows, LANES), lambda b, n: (b, n, 0)),
            pl.BlockSpec((1, cb, 2 * DN_V_HEADS, DN_CHUNK), lambda b, n: (b, n, 0, 0)),
            pl.BlockSpec((1, DN_HEAD_DIM), lambda b, n: (0, 0)),
        ],
        out_specs=pl.BlockSpec((1, rows, vd), lambda b, n: (b, n, 0)),
        out_shape=jax.ShapeDtypeStruct((bsz, seq, vd), BF16),
        scratch_shapes=[pltpu.VMEM((DN_V_HEADS, DN_HEAD_DIM, DN_HEAD_DIM), F32)],
        compiler_params=_cparams(("parallel", "arbitrary")),
        name="dn_chunk",
    )(qn, kn, vv, proj, gates, gates_t, onorm_w.reshape(1, DN_HEAD_DIM))


def _residual_ln(x, y, gate_row, ln_g, ln_b):
    return _layer_norm(DEEPNORM_ALPHA * x + (1.0 + gate_row) * y, ln_g, ln_b)


def _route_store(h, rw_ref, rb_ref, idx_ref, gate_ref):
    logits = jnp.dot(h, rw_ref[...], precision=HIGHEST, preferred_element_type=F32) + rb_ref[...]
    lane = lax.broadcasted_iota(jnp.int32, logits.shape, 1).astype(F32)
    work = logits
    idx_out = jnp.zeros(logits.shape, F32)
    val_out = jnp.full(logits.shape, NEG_BIG, F32)
    for k in range(TOP_K):
        m = jnp.max(work, -1, keepdims=True)
        am = jnp.min(jnp.where(work == m, lane, float(LANES)), -1, keepdims=True)
        idx_out = jnp.where(lane == k, am, idx_out)
        val_out = jnp.where(lane == k, m, val_out)
        work = jnp.where(lane == am, NEG_BIG * 2.0, work)
    top = jnp.max(val_out, -1, keepdims=True)
    e = jnp.where(lane < TOP_K, jnp.exp(val_out - top), 0.0)
    idx_ref[...] = idx_out.astype(jnp.int32)
    gate_ref[...] = e / jnp.sum(e, -1, keepdims=True)


def _post_mixer_tail(x, y, mod_ref, lng_ref, lnb_ref, rw_ref, rb_ref, x1_ref, h3_ref, idx_ref, gate_ref):
    x1 = _residual_ln(x, y, mod_ref[0, 2:3, :], lng_ref[...], lnb_ref[...])
    x1_ref[...] = x1
    h2 = x1 * (1.0 + mod_ref[0, 4:5, :]) + mod_ref[0, 3:4, :]
    _rows_to_tiles(h3_ref, h2)
    _route_store(h2, rw_ref, rb_ref, idx_ref, gate_ref)


def _router_operands(router_w, router_b):
    d = router_w.shape[0]
    rw = jnp.zeros((d, LANES), F32).at[:, :N_EXPERTS].set(router_w)
    rb = jnp.full((1, LANES), NEG_BIG, F32).at[0, :N_EXPERTS].set(router_b)
    return rw, rb


def _post_out_specs(tm, d):
    return [
        pl.BlockSpec((tm, d), lambda i: (i, 0)),
        pl.BlockSpec((tm, d // LANES, LANES), lambda i: (i, 0, 0)),
        pl.BlockSpec((tm, LANES), lambda i: (i, 0)),
        pl.BlockSpec((tm, LANES), lambda i: (i, 0)),
    ]


def _post_out_shapes(t, d):
    return [
        jax.ShapeDtypeStruct((t, d), F32),
        jax.ShapeDtypeStruct((t, d // LANES, LANES), F32),
        jax.ShapeDtypeStruct((t, LANES), jnp.int32),
        jax.ShapeDtypeStruct((t, LANES), F32),
    ]


def _dn_out_kernel(o_ref, w_ref, x_ref, mod_ref, lng_ref, lnb_ref, rw_ref, rb_ref,
                   x1_ref, h3_ref, idx_ref, gate_ref):
    y = jnp.dot(o_ref[...], w_ref[...], preferred_element_type=F32)
    _post_mixer_tail(x_ref[...], y, mod_ref, lng_ref, lnb_ref, rw_ref, rb_ref,
                     x1_ref, h3_ref, idx_ref, gate_ref)


def _dn_out(o, out_w, x, mod, ln_g, ln_b, router_w, router_b):
    bsz, seq, d = x.shape
    t = bsz * seq
    vd = o.shape[2]
    tm = min(POST_TM, seq)
    rw, rb = _router_operands(router_w, router_b)
    per_b = seq // tm
    return pl.pallas_call(
        _dn_out_kernel,
        grid=(t // tm,),
        in_specs=[
            pl.BlockSpec((tm, vd), lambda i: (i, 0)),
            pl.BlockSpec((vd, d), lambda i: (0, 0)),
            pl.BlockSpec((tm, d), lambda i: (i, 0)),
            pl.BlockSpec((1, 6, d), lambda i: (i // per_b, 0, 0)),
            pl.BlockSpec((1, d), lambda i: (0, 0)),
            pl.BlockSpec((1, d), lambda i: (0, 0)),
            pl.BlockSpec((d, LANES), lambda i: (0, 0)),
            pl.BlockSpec((1, LANES), lambda i: (0, 0)),
        ],
        out_specs=_post_out_specs(tm, d),
        out_shape=_post_out_shapes(t, d),
        compiler_params=_cparams(("parallel",)),
        name="dn_out_ln_route",
    )(o.reshape(t, vd), out_w.astype(BF16), x.reshape(t, d), mod, ln_g.reshape(1, d), ln_b.reshape(1, d), rw, rb)


def _cf_pw1_kernel(x_ref, mod_ref, wa_ref, wb_ref, ba_ref, bb_ref, u_ref):
    h = (x_ref[0] * (1.0 + mod_ref[0, 1:2, :]) + mod_ref[0, 0:1, :]).astype(BF16)
    pa = jnp.dot(h, wa_ref[...], preferred_element_type=F32) + ba_ref[...]
    pb = jnp.dot(h, wb_ref[...], preferred_element_type=F32) + bb_ref[...]
    u_ref[0] = (pa * _sigmoid(pb)).astype(u_ref.dtype)


def _cf_pw1(x, mod, pw1_w, pw1_b):
    bsz, seq, d = x.shape
    inner = pw1_w.shape[1] // 2
    tm = min(PROJ_TM, seq)
    w = pw1_w.astype(BF16)
    b = pw1_b.reshape(1, 2 * inner)
    return pl.pallas_call(
        _cf_pw1_kernel,
        grid=(bsz, seq // tm),
        in_specs=[
            pl.BlockSpec((1, tm, d), lambda b_, i: (b_, i, 0)),
            pl.BlockSpec((1, 6, d), lambda b_, i: (b_, 0, 0)),
            pl.BlockSpec((d, inner), lambda b_, i: (0, 0)),
            pl.BlockSpec((d, inner), lambda b_, i: (0, 1)),
            pl.BlockSpec((1, inner), lambda b_, i: (0, 0)),
            pl.BlockSpec((1, inner), lambda b_, i: (0, 1)),
        ],
        out_specs=pl.BlockSpec((1, tm, inner), lambda b_, i: (b_, i, 0)),
        out_shape=jax.ShapeDtypeStruct((bsz, seq, inner), BF16),
        compiler_params=_cparams(("parallel", "parallel")),
        name="cf_pw1_glu",
    )(x, mod, w, w, b, b)


def _cf_tail_kernel(u_ref, uh_ref, dw_ref, dwb_ref, cg_ref, cb_ref, w2_ref, b2_ref,
                    x_ref, mod_ref, lng_ref, lnb_ref, rw_ref, rb_ref,
                    x1_ref, h3_ref, idx_ref, gate_ref, scr):
    tm = u_ref.shape[1]
    halo = uh_ref.shape[1]
    keep = (pl.program_id(1) > 0).astype(F32)
    scr[0:halo, :] = uh_ref[0].astype(F32) * keep
    scr[halo:halo + tm, :] = u_ref[0].astype(F32)
    acc = None
    for j in range(CF_KERNEL):
        off = halo - (CF_KERNEL - 1) + j
        term = scr[off:off + tm, :] * dw_ref[j:j + 1, :]
        acc = term if acc is None else acc + term
    conv = acc + dwb_ref[...]
    normed = _layer_norm(conv, cg_ref[...], cb_ref[...])
    act = normed * _sigmoid(normed)
    y = jnp.dot(act.astype(BF16), w2_ref[...], preferred_element_type=F32) + b2_ref[...]
    _post_mixer_tail(x_ref[0], y, mod_ref, lng_ref, lnb_ref, rw_ref, rb_ref,
                     x1_ref, h3_ref, idx_ref, gate_ref)


def _cf_tail(u, dw_w, dw_b, cf_ln_g, cf_ln_b, pw2_w, pw2_b, x, mod, ln_g, ln_b, router_w, router_b):
    bsz, seq, d = x.shape
    inner = u.shape[2]
    t = bsz * seq
    tm = min(CF_TM, seq)
    per_b = seq // tm
    hb = tm // CF_HALO
    rw, rb = _router_operands(router_w, router_b)
    row = lambda v: v.reshape(1, -1)
    const = lambda b, i: (0, 0)
    flat = lambda b, i: (b * per_b + i, 0)
    return pl.pallas_call(
        _cf_tail_kernel,
        grid=(bsz, per_b),
        in_specs=[
            pl.BlockSpec((1, tm, inner), lambda b, i: (b, i, 0)),
            pl.BlockSpec((1, CF_HALO, inner), lambda b, i: (b, jnp.maximum(i * hb - 1, 0), 0)),
            pl.BlockSpec((CF_KERNEL, inner), const),
            pl.BlockSpec((1, inner), const),
            pl.BlockSpec((1, inner), const),
            pl.BlockSpec((1, inner), const),
            pl.BlockSpec((inner, d), const),
            pl.BlockSpec((1, d), const),
            pl.BlockSpec((1, tm, d), lambda b, i: (b, i, 0)),
            pl.BlockSpec((1, 6, d), lambda b, i: (b, 0, 0)),
            pl.BlockSpec((1, d), const),
            pl.BlockSpec((1, d), const),
            pl.BlockSpec((d, LANES), const),
            pl.BlockSpec((1, LANES), const),
        ],
        out_specs=[
            pl.BlockSpec((tm, d), flat),
            pl.BlockSpec((tm, d // LANES, LANES), lambda b, i: (b * per_b + i, 0, 0)),
            pl.BlockSpec((tm, LANES), flat),
            pl.BlockSpec((tm, LANES), flat),
        ],
        out_shape=_post_out_shapes(t, d),
        scratch_shapes=[pltpu.VMEM((tm + CF_HALO, inner), F32)],
        compiler_params=_cparams(("parallel", "arbitrary")),
        name="cf_conv_ln_pw2_route",
    )(u, u, dw_w, row(dw_b), row(cf_ln_g), row(cf_ln_b), pw2_w.astype(BF16), row(pw2_b),
      x, mod, row(ln_g), row(ln_b), rw, rb)


def _rank_kernel(idx_ref, rank_ref, cnt_ref, carry):
    tt = idx_ref.shape[0]

    @pl.when(pl.program_id(0) == 0)
    def _():
        carry[...] = jnp.zeros(carry.shape, F32)

    idx = idx_ref[...]
    lane = lax.broadcasted_iota(jnp.int32, idx.shape, 1)
    sel = [lane == idx[:, k:k + 1] for k in range(TOP_K)]
    multi_hot = sel[0]
    for k in range(1, TOP_K):
        multi_hot = multi_hot | sel[k]
    mh = multi_hot.astype(BF16)
    r = lax.broadcasted_iota(jnp.int32, (tt, tt), 0)
    c = lax.broadcasted_iota(jnp.int32, (tt, tt), 1)
    before = jnp.dot((c < r).astype(BF16), mh, preferred_element_type=F32) + carry[...]
    rank = jnp.zeros(idx.shape, F32)
    for k in range(TOP_K):
        rk = jnp.sum(jnp.where(sel[k], before, 0.0), -1, keepdims=True)
        rank = jnp.where(lane == k, rk, rank)
    rank_ref[...] = rank.astype(jnp.int32)
    total = carry[...] + jnp.sum(mh.astype(F32), 0, keepdims=True)
    carry[...] = total
    cnt_ref[...] = total.astype(jnp.int32)


def _rank_and_count(idx):
    t = idx.shape[0]
    tt = min(RANK_TT, t)
    return pl.pallas_call(
        _rank_kernel,
        grid=(t // tt,),
        in_specs=[pl.BlockSpec((tt, LANES), lambda i: (i, 0))],
        out_specs=[
            pl.BlockSpec((tt, LANES), lambda i: (i, 0)),
            pl.BlockSpec((1, LANES), lambda i: (0, 0)),
        ],
        out_shape=[
            jax.ShapeDtypeStruct((t, LANES), jnp.int32),
            jax.ShapeDtypeStruct((1, LANES), jnp.int32),
        ],
        scratch_shapes=[pltpu.VMEM((1, LANES), F32)],
        compiler_params=_cparams(("arbitrary",)),
        name="moe_rank_count",
    )(idx)


def _num_expert_blocks(n_assign):
    bound = n_assign + N_EXPERTS * (EXPERT_TM - 1)
    return -(-bound // EXPERT_TM)


def _routing_tables(idx, rank, counts):
    t = idx.shape[0]
    nb = _num_expert_blocks(t * TOP_K)
    counts = counts[0, :N_EXPERTS]
    padded = (counts + EXPERT_TM - 1) // EXPERT_TM * EXPERT_TM
    pends = jnp.cumsum(padded)
    pstarts = pends - padded
    e = idx[:, :TOP_K]
    dest = (pstarts[e] + rank[:, :TOP_K]).astype(jnp.int32).reshape(t * TOP_K)
    block_row0 = jnp.arange(nb, dtype=jnp.int32) * EXPERT_TM
    block_e = jnp.minimum(jnp.searchsorted(pends, block_row0, side='right'), N_EXPERTS - 1).astype(jnp.int32)
    live = jnp.clip(counts[block_e] - (block_row0 - pstarts[block_e]), 0, EXPERT_TM).astype(jnp.int32)
    return dest, block_e, live


def _pad_fill_copies(live_ref, i, zeros, xb_hbm, sem):
    lv = live_ref[i]
    pad = EXPERT_TM - lv
    row0 = i * EXPERT_TM + lv
    copies = []
    piece = EXPERT_TM
    while piece >= 1:
        offset = pad & ~(2 * piece - 1)
        copy = pltpu.make_async_copy(zeros.at[pl.ds(0, piece)], xb_hbm.at[pl.ds(row0 + offset, piece)], sem)
        copies.append(((pad & piece) != 0, copy))
        piece //= 2
    return copies


def _dispatch_kernel(dest_ref, live_ref, h_hbm, xb_hbm, zeros, sem, fill_sem):
    base = pl.program_id(0) * DISPATCH_TT
    n_blocks = xb_hbm.shape[0] // EXPERT_TM

    @pl.when(pl.program_id(0) == 0)
    def _():
        zeros[...] = jnp.zeros(zeros.shape, F32)

        def start_fill(i, carry):
            for live_bit, copy in _pad_fill_copies(live_ref, i, zeros, xb_hbm, fill_sem):
                pl.when(live_bit)(copy.start)
            return carry

        lax.fori_loop(0, n_blocks, start_fill, 0)

    def issue(t, carry):
        tok = base + t
        for k in range(TOP_K):
            pltpu.make_async_copy(h_hbm.at[tok], xb_hbm.at[dest_ref[tok * TOP_K + k]], sem).start()
        return carry

    lax.fori_loop(0, DISPATCH_TT, issue, 0)
    n = DISPATCH_TT
    for _ in range(TOP_K):
        pltpu.make_async_copy(h_hbm.at[pl.ds(0, n)], xb_hbm.at[pl.ds(0, n)], sem).wait()

    @pl.when(pl.program_id(0) == 0)
    def _():
        def wait_fill(i, carry):
            for live_bit, copy in _pad_fill_copies(live_ref, i, zeros, xb_hbm, fill_sem):
                pl.when(live_bit)(copy.wait)
            return carry

        lax.fori_loop(0, n_blocks, wait_fill, 0)


def _dispatch(dest, live, h3, n_rows):
    t, s, l = h3.shape
    return pl.pallas_call(
        _dispatch_kernel,
        grid_spec=pltpu.PrefetchScalarGridSpec(
            num_scalar_prefetch=2,
            grid=(t // DISPATCH_TT,),
            in_specs=[pl.BlockSpec(memory_space=pl.ANY)],
            out_specs=pl.BlockSpec(memory_space=pl.ANY),
            scratch_shapes=[pltpu.VMEM((EXPERT_TM, s, l), F32), pltpu.SemaphoreType.DMA,
                            pltpu.SemaphoreType.DMA],
        ),
        out_shape=jax.ShapeDtypeStruct((n_rows, s, l), F32),
        compiler_params=_cparams(("arbitrary",)),
        name="moe_dispatch",
    )(dest, live, h3)


def _expert_kernel(be_ref, live_ref, x_ref, w1_ref, b1_ref, w2_ref, b2_ref, y_ref, w1s, w2s):
    i = pl.program_id(0)
    live = live_ref[i]
    e = be_ref[i]
    e_prev = be_ref[jnp.maximum(i - 1, 0)]

    @pl.when((i == 0) | (e != e_prev))
    def _():
        w1s[...] = w1_ref[0].astype(BF16)
        w2s[...] = w2_ref[0].astype(BF16)

    @pl.when(live > 0)
    def _():
        dff = w2s.shape[0]
        x = _tiles_to_rows(x_ref)
        row = lax.broadcasted_iota(jnp.int32, (x.shape[0], 1), 0)
        x = jnp.where(row < live, x, 0.0).astype(BF16)
        hh = jnp.dot(x, w1s[...], preferred_element_type=F32) + b1_ref[0]
        x_glu = jnp.minimum(hh[:, :dff], SWIGLU_LIMIT)
        x_lin = jnp.clip(hh[:, dff:], -SWIGLU_LIMIT, SWIGLU_LIMIT)
        act = x_glu * _sigmoid(SWIGLU_ALPHA * x_glu) * (x_lin + 1.0)
        y = jnp.dot(act.astype(BF16), w2s[...], preferred_element_type=F32) + b2_ref[0]
        _rows_to_tiles(y_ref, y)

    @pl.when(live == 0)
    def _():
        y_ref[...] = jnp.zeros(y_ref.shape, F32)


def _experts(block_e, live, xb, w1, b1, w2, b2):
    n_rows, s, l = xb.shape
    n_e, d, two_f = w1.shape
    dff = two_f // 2
    nb = n_rows // EXPERT_TM
    return pl.pallas_call(
        _expert_kernel,
        grid_spec=pltpu.PrefetchScalarGridSpec(
            num_scalar_prefetch=2,
            grid=(nb,),
            in_specs=[
                pl.BlockSpec((EXPERT_TM, s, l), lambda i, be, lv: (i, 0, 0)),
                pl.BlockSpec((1, d, two_f), lambda i, be, lv: (be[i], 0, 0)),
                pl.BlockSpec((1, 1, two_f), lambda i, be, lv: (be[i], 0, 0)),
                pl.BlockSpec((1, dff, d), lambda i, be, lv: (be[i], 0, 0)),
                pl.BlockSpec((1, 1, d), lambda i, be, lv: (be[i], 0, 0)),
            ],
            out_specs=pl.BlockSpec((EXPERT_TM, s, l), lambda i, be, lv: (i, 0, 0)),
            scratch_shapes=[pltpu.VMEM((d, two_f), BF16), pltpu.VMEM((dff, d), BF16)],
        ),
        out_shape=jax.ShapeDtypeStruct((n_rows, s, l), F32),
        compiler_params=_cparams(("arbitrary",)),
        name="moe_experts",
    )(block_e, live, xb, w1, b1.reshape(n_e, 1, two_f), w2, b2.reshape(n_e, 1, d))


def _combine_kernel(dest_ref, yb_hbm, gate_ref, x_ref, mod_ref, lng_ref, lnb_ref, o_ref, buf, sem):
    tt = x_ref.shape[0]
    base = pl.program_id(0) * tt

    def issue(t, carry):
        for k in range(TOP_K):
            pltpu.make_async_copy(yb_hbm.at[dest_ref[(base + t) * TOP_K + k]], buf.at[k, t], sem).start()
        return carry

    lax.fori_loop(0, tt, issue, 0)
    for k in range(TOP_K):
        pltpu.make_async_copy(yb_hbm.at[pl.ds(0, tt)], buf.at[k], sem).wait()

    gates = gate_ref[...]
    y = None
    for k in range(TOP_K):
        term = gates[:, k:k + 1] * _tiles_to_rows(buf.at[k])
        y = term if y is None else y + term
    o_ref[...] = _residual_ln(x_ref[...], y, mod_ref[0, 5:6, :], lng_ref[...], lnb_ref[...])


def _combine(dest, yb, gates, x1, mod, ln_g, ln_b, seq):
    t, d = x1.shape
    _, s, l = yb.shape
    tt = min(COMBINE_TT, seq)
    per_b = seq // tt
    return pl.pallas_call(
        _combine_kernel,
        grid_spec=pltpu.PrefetchScalarGridSpec(
            num_scalar_prefetch=1,
            grid=(t // tt,),
            in_specs=[
                pl.BlockSpec(memory_space=pl.ANY),
                pl.BlockSpec((tt, LANES), lambda i, dst: (i, 0)),
                pl.BlockSpec((tt, d), lambda i, dst: (i, 0)),
                pl.BlockSpec((1, 6, d), lambda i, dst: (i // per_b, 0, 0)),
                pl.BlockSpec((1, d), lambda i, dst: (0, 0)),
                pl.BlockSpec((1, d), lambda i, dst: (0, 0)),
            ],
            out_specs=pl.BlockSpec((tt, d), lambda i, dst: (i, 0)),
            scratch_shapes=[pltpu.VMEM((TOP_K, tt, s, l), F32), pltpu.SemaphoreType.DMA],
        ),
        out_shape=jax.ShapeDtypeStruct((t, d), F32),
        compiler_params=_cparams(("arbitrary",)),
        name="moe_combine_ln",
    )(dest, yb, gates, x1, mod, ln_g.reshape(1, d), ln_b.reshape(1, d))


def _moe_block(x1, h3, idx, gates, mod, w1, b1, w2, b2, ln_g, ln_b, seq):
    t = x1.shape[0]
    rank, counts = _rank_and_count(idx)
    dest, block_e, live = _routing_tables(idx, rank, counts)
    n_rows = _num_expert_blocks(t * TOP_K) * EXPERT_TM
    xb = _dispatch(dest, live, h3, n_rows)
    yb = _experts(block_e, live, xb, w1, b1, w2, b2)
    return _combine(dest, yb, gates, x1, mod, ln_g, ln_b, seq)


def kernel(x, c, ada_w, ada_b, dn_in_w, dn_conv_w, dn_A_log, dn_dt_bias, dn_onorm_w, dn_out_w,
           cf_pw1_w, cf_pw1_b, cf_dw_w, cf_dw_b, cf_ln_g, cf_ln_b, cf_pw2_w, cf_pw2_b,
           ln1_g, ln1_b, router_w, router_b, e_w1, e_b1, e_w2, e_b2, ln2_g, ln2_b):
    bsz, seq, d = x.shape
    mods = _ada_ln(c, ada_w, ada_b)

    proj, ba = _dn_in_proj(x, mods[0], dn_in_w[0])
    qn, kn, vv, gates_dn = _dn_prep(proj, ba, dn_conv_w[0], dn_A_log[0], dn_dt_bias[0])
    o = _dn_chunk(qn, kn, vv, proj, gates_dn, dn_onorm_w[0])
    x1, h3, idx, gates = _dn_out(o, dn_out_w[0], x, mods[0], ln1_g[0], ln1_b[0], router_w[0], router_b[0])
    x2 = _moe_block(x1, h3, idx, gates, mods[0], e_w1[0], e_b1[0], e_w2[0], e_b2[0], ln2_g[0], ln2_b[0], seq)

    x2 = x2.reshape(bsz, seq, d)
    u = _cf_pw1(x2, mods[1], cf_pw1_w[0], cf_pw1_b[0])
    x3, h3, idx, gates = _cf_tail(u, cf_dw_w[0], cf_dw_b[0], cf_ln_g[0], cf_ln_b[0], cf_pw2_w[0], cf_pw2_b[0],
                                  x2, mods[1], ln1_g[1], ln1_b[1], router_w[1], router_b[1])
    x4 = _moe_block(x3, h3, idx, gates, mods[1], e_w1[1], e_b1[1], e_w2[1], e_b2[1], ln2_g[1], ln2_b[1], seq)
    return x4.reshape(bsz, seq, d)
```

```python
import functools

import jax
import jax.numpy as jnp
from jax import lax
from jax.experimental import pallas as pl
from jax.experimental.pallas import tpu as pltpu

F32 = jnp.float32
BF16 = jnp.bfloat16
HIGHEST = lax.Precision.HIGHEST

DEPTH = 2
DN_QK_HEADS = 8
DN_V_HEADS = 16
DN_HEAD_DIM = 128
DN_CONV = 4
DN_CHUNK = 64
CF_KERNEL = 31
N_EXPERTS = 32
TOP_K = 4
SWIGLU_LIMIT = 7.0
SWIGLU_ALPHA = 1.702
LN_EPS = 1e-5
RMS_EPS = 1e-6
L2_EPS = 1e-6
DEEPNORM_ALPHA = (2 * DEPTH) ** 0.25

LANES = 128
SUBLANES = 8
BF16_SUBLANES = 16
VMEM_LIMIT = 56 * 1024 * 1024

ADA_TN = 1536
PROJ_TM = 1024
PROJ_TN = 1024
PREP_TS = 512
CHUNKS_PER_STEP = 4
POST_TM = 512
RANK_TT = 512
DISPATCH_TT = 512
EXPERT_TM = 256
COMBINE_TT = 256
CF_TM = 512
CF_HALO = 32

NEG_BIG = -1e30


def _cparams(sem):
    return pltpu.CompilerParams(dimension_semantics=sem, vmem_limit_bytes=VMEM_LIMIT)


def _sigmoid(x):
    return jax.nn.sigmoid(x)


def _layer_norm(v, g, b):
    mu = jnp.mean(v, -1, keepdims=True)
    d = v - mu
    var = jnp.mean(d * d, -1, keepdims=True)
    return d * lax.rsqrt(var + LN_EPS) * g + b


def _rows_to_tiles(o3_ref, val):
    for s in range(val.shape[1] // LANES):
        o3_ref[:, s, :] = val[:, s * LANES:(s + 1) * LANES]


def _tiles_to_rows(x3_ref):
    return jnp.concatenate([x3_ref[:, s, :] for s in range(x3_ref.shape[1])], axis=1)


def _ada_kernel(c_ref, w_ref, b_ref, o_ref):
    c = c_ref[...]
    cond = c * _sigmoid(c)
    o_ref[0] = jnp.dot(cond, w_ref[0], precision=HIGHEST, preferred_element_type=F32) + b_ref[0]


def _ada_ln(c, ada_w, ada_b):
    depth, d, n = ada_w.shape
    bsz = c.shape[0]
    c_pad = jnp.zeros((SUBLANES, d), F32).at[:bsz].set(c)
    out = pl.pallas_call(
        _ada_kernel,
        grid=(depth, n // ADA_TN),
        in_specs=[
            pl.BlockSpec((SUBLANES, d), lambda i, j: (0, 0)),
            pl.BlockSpec((1, d, ADA_TN), lambda i, j: (i, 0, j)),
            pl.BlockSpec((1, 1, ADA_TN), lambda i, j: (i, 0, j)),
        ],
        out_specs=pl.BlockSpec((1, SUBLANES, ADA_TN), lambda i, j: (i, 0, j)),
        out_shape=jax.ShapeDtypeStruct((depth, SUBLANES, n), F32),
        compiler_params=_cparams(("parallel", "parallel")),
        name="ada_ln",
    )(c_pad, ada_w, ada_b.reshape(depth, 1, n))
    return out[:, :bsz].reshape(depth, bsz, 6, d)


def _inproj_kernel(x_ref, mod_ref, w_ref, wba_ref, proj_ref, ba_ref, h_scr):
    @pl.when(pl.program_id(2) == 0)
    def _():
        h = x_ref[0] * (1.0 + mod_ref[0, 1:2, :]) + mod_ref[0, 0:1, :]
        h_scr[...] = h.astype(BF16)
        ba_ref[0] = jnp.dot(h, wba_ref[...], precision=HIGHEST, preferred_element_type=F32)

    proj_ref[0] = jnp.dot(h_scr[...], w_ref[...], preferred_element_type=F32).astype(BF16)


def _dn_in_proj(x, mod, in_w):
    bsz, seq, d = x.shape
    n_main = in_w.shape[1] - 2 * DN_V_HEADS
    w_main = in_w[:, :n_main].astype(BF16)
    w_ba = jnp.zeros((d, LANES), F32).at[:, :2 * DN_V_HEADS].set(in_w[:, n_main:])
    tm = min(PROJ_TM, seq)
    return pl.pallas_call(
        _inproj_kernel,
        grid=(bsz, seq // tm, n_main // PROJ_TN),
        in_specs=[
            pl.BlockSpec((1, tm, d), lambda b, i, j: (b, i, 0)),
            pl.BlockSpec((1, 6, d), lambda b, i, j: (b, 0, 0)),
            pl.BlockSpec((d, PROJ_TN), lambda b, i, j: (0, j)),
            pl.BlockSpec((d, LANES), lambda b, i, j: (0, 0)),
        ],
        out_specs=[
            pl.BlockSpec((1, tm, PROJ_TN), lambda b, i, j: (b, i, j)),
            pl.BlockSpec((1, tm, LANES), lambda b, i, j: (b, i, 0)),
        ],
        out_shape=[
            jax.ShapeDtypeStruct((bsz, seq, n_main), BF16),
            jax.ShapeDtypeStruct((bsz, seq, LANES), F32),
        ],
        scratch_shapes=[pltpu.VMEM((tm, d), BF16)],
        compiler_params=_cparams(("parallel", "parallel", "arbitrary")),
        name="dn_in_proj",
    )(x, mod, w_main, w_ba)


def _dn_prep_kernel(q_ref, k_ref, v_ref, qh_ref, kh_ref, vh_ref, cw_ref, ba_ref, alog_ref, dt_ref,
                    qo_ref, ko_ref, vo_ref, g_ref, scr):
    ts = q_ref.shape[1]
    halo = qh_ref.shape[1]
    keep = (pl.program_id(1) > 0).astype(F32)

    def conv_silu(x_ref, h_ref, c0):
        width = x_ref.shape[2]
        scr[0:halo, 0:width] = h_ref[0].astype(F32) * keep
        scr[halo:halo + ts, 0:width] = x_ref[0].astype(F32)
        acc = None
        for j in range(DN_CONV):
            off = halo - (DN_CONV - 1) + j
            term = scr[off:off + ts, 0:width] * cw_ref[j:j + 1, c0:c0 + width]
            acc = term if acc is None else acc + term
        return acc * _sigmoid(acc)

    def l2norm_store(o_ref, x, scale):
        for h in range(x.shape[1] // DN_HEAD_DIM):
            xh = x[:, h * DN_HEAD_DIM:(h + 1) * DN_HEAD_DIM]
            ss = jnp.sum(xh * xh, -1, keepdims=True)
            o_ref[0, :, h * DN_HEAD_DIM:(h + 1) * DN_HEAD_DIM] = (
                xh * lax.rsqrt(ss + L2_EPS) * scale).astype(o_ref.dtype)

    kd = q_ref.shape[2]
    l2norm_store(qo_ref, conv_silu(q_ref, qh_ref, 0), DN_HEAD_DIM ** -0.5)
    l2norm_store(ko_ref, conv_silu(k_ref, kh_ref, kd), 1.0)
    vo_ref[0] = conv_silu(v_ref, vh_ref, 2 * kd).astype(vo_ref.dtype)

    ba = ba_ref[0]
    beta = _sigmoid(ba)
    zz = ba + dt_ref[...]
    softplus = jnp.maximum(zz, 0.0) + jnp.log1p(jnp.exp(-jnp.abs(zz)))
    g = -jnp.exp(alog_ref[...]) * softplus
    r = lax.broadcasted_iota(jnp.int32, (ts, ts), 0)
    c = lax.broadcasted_iota(jnp.int32, (ts, ts), 1)
    in_chunk_tril = ((r // DN_CHUNK == c // DN_CHUNK) & (c <= r)).astype(F32)
    gcum = jnp.dot(in_chunk_tril, g, precision=HIGHEST, preferred_element_type=F32)
    lane = lax.broadcasted_iota(jnp.int32, ba.shape, 1)
    g_ref[0] = jnp.where(lane < DN_V_HEADS, beta, gcum)


def _dn_prep(proj, ba, conv_w, a_log, dt_bias):
    bsz, seq, _ = proj.shape
    kd = DN_QK_HEADS * DN_HEAD_DIM
    vd = DN_V_HEADS * DN_HEAD_DIM
    ts = min(PREP_TS, seq)
    halo = BF16_SUBLANES
    hb = ts // halo
    alog_row = jnp.zeros((1, LANES), F32).at[0, DN_V_HEADS:2 * DN_V_HEADS].set(a_log)
    dt_row = jnp.zeros((1, LANES), F32).at[0, DN_V_HEADS:2 * DN_V_HEADS].set(dt_bias)

    def halo_map(col):
        return lambda b, i: (b, jnp.maximum(i * hb - 1, 0), col)

    return pl.pallas_call(
        _dn_prep_kernel,
        grid=(bsz, seq // ts),
        in_specs=[
            pl.BlockSpec((1, ts, kd), lambda b, i: (b, i, 0)),
            pl.BlockSpec((1, ts, kd), lambda b, i: (b, i, 1)),
            pl.BlockSpec((1, ts, vd), lambda b, i: (b, i, 1)),
            pl.BlockSpec((1, halo, kd), halo_map(0)),
            pl.BlockSpec((1, halo, kd), halo_map(1)),
            pl.BlockSpec((1, halo, vd), halo_map(1)),
            pl.BlockSpec((DN_CONV, 2 * kd + vd), lambda b, i: (0, 0)),
            pl.BlockSpec((1, ts, LANES), lambda b, i: (b, i, 0)),
            pl.BlockSpec((1, LANES), lambda b, i: (0, 0)),
            pl.BlockSpec((1, LANES), lambda b, i: (0, 0)),
        ],
        out_specs=[
            pl.BlockSpec((1, ts, kd), lambda b, i: (b, i, 0)),
            pl.BlockSpec((1, ts, kd), lambda b, i: (b, i, 0)),
            pl.BlockSpec((1, ts, vd), lambda b, i: (b, i, 0)),
            pl.BlockSpec((1, ts, LANES), lambda b, i: (b, i, 0)),
        ],
        out_shape=[
            jax.ShapeDtypeStruct((bsz, seq, kd), BF16),
            jax.ShapeDtypeStruct((bsz, seq, kd), BF16),
            jax.ShapeDtypeStruct((bsz, seq, vd), BF16),
            jax.ShapeDtypeStruct((bsz, seq, LANES), F32),
        ],
        scratch_shapes=[pltpu.VMEM((ts + halo, vd), F32)],
        compiler_params=_cparams(("parallel", "arbitrary")),
        name="dn_prep",
    )(proj, proj, proj, proj, proj, proj, conv_w, ba, alog_row, dt_row)


def _bmm(a, b):
    return lax.dot_general(a.astype(BF16), b.astype(BF16), (((2,), (1,)), ((0,), (0,))),
                           preferred_element_type=F32)


def _bmm_nt(a, b):
    return lax.dot_general(a.astype(BF16), b.astype(BF16), (((2,), (2,)), ((0,), (0,))),
                           preferred_element_type=F32)


def _bmm_tn(a, b):
    return lax.dot_general(a.astype(BF16), b.astype(BF16), (((1,), (1,)), ((0,), (0,))),
                           preferred_element_type=F32)


def _unit_lower_inverse(a_strict, eye):
    c = a_strict.shape[-1]
    b_pow = -a_strict
    p = eye + b_pow
    b_pow = _bmm(b_pow, b_pow)
    span = 2
    while 2 * span < c:
        both = _bmm(b_pow, jnp.concatenate([b_pow, p], axis=2))
        p = p + both[:, :, c:]
        b_pow = both[:, :, :c]
        span *= 2
    return p + _bmm(b_pow, p)


def _dn_chunk_kernel(q_ref, k_ref, v_ref, z_ref, g_ref, gt_ref, ow_ref, o_ref, s_ref):
    c_len = DN_CHUNK
    dh = DN_HEAD_DIM
    rep = DN_V_HEADS // DN_QK_HEADS

    @pl.when(pl.program_id(1) == 0)
    def _():
        s_ref[...] = jnp.zeros(s_ref.shape, F32)

    heads = range(DN_V_HEADS)
    ri = lax.broadcasted_iota(jnp.int32, (1, c_len, c_len), 1)
    ci = lax.broadcasted_iota(jnp.int32, (1, c_len, c_len), 2)
    causal = ci <= ri
    strict = ci < ri
    eye = (ci == ri).astype(F32)
    onorm = ow_ref[...]

    def head_cols(ref, rows, h):
        return ref[0, rows, h * dh:(h + 1) * dh]

    def per_v_head(t):
        return jnp.stack([t[h // rep] for h in heads])

    def chunk(c, carry):
        r0 = pl.multiple_of(c * c_len, c_len)
        rows = pl.ds(r0, c_len)
        gcols = g_ref[0, rows, :]
        grows = gt_ref[0, c]
        beta_c = jnp.stack([gcols[:, h:h + 1] for h in heads])
        g_c = jnp.stack([gcols[:, DN_V_HEADS + h:DN_V_HEADS + h + 1] for h in heads])
        g_r = jnp.stack([grows[DN_V_HEADS + h:DN_V_HEADS + h + 1, :] for h in heads])
        g_last = g_c[:, c_len - 1:c_len, :]

        qn = jnp.stack([head_cols(q_ref, rows, hq) for hq in range(DN_QK_HEADS)])
        kn = jnp.stack([head_cols(k_ref, rows, hq) for hq in range(DN_QK_HEADS)])
        qk_kk = _bmm_nt(jnp.concatenate([qn, kn], axis=1), kn)
        qk = per_v_head(qk_kk[:, :c_len])
        kk = per_v_head(qk_kk[:, c_len:])
        qf = per_v_head(qn).astype(F32)
        kf = per_v_head(kn).astype(F32)
        vf = jnp.stack([head_cols(v_ref, rows, h) for h in heads]).astype(F32)

        decay = jnp.where(causal, jnp.exp(g_c - g_r), 0.0)
        a = jnp.where(strict, beta_c * kk * decay, 0.0)
        t_inv = _unit_lower_inverse(a, eye)
        eg = jnp.exp(g_c)
        sol = _bmm(t_inv, jnp.concatenate([vf * beta_c, kf * (beta_c * eg)], axis=2))
        u = sol[:, :, :dh]
        w = sol[:, :, dh:]
        qkm = jnp.where(causal, qk * decay, 0.0)
        qg = qf * eg
        kdec = kf * jnp.exp(g_last - g_c)

        state = s_ref[...]
        ws = _bmm(jnp.concatenate([w, qg], axis=1), state)
        v_new = u - ws[:, :c_len]
        o = ws[:, c_len:] + _bmm(qkm, v_new)
        s_ref[...] = state * jnp.exp(g_last) + _bmm_tn(kdec, v_new)

        o = o * lax.rsqrt(jnp.mean(o * o, -1, keepdims=True) + RMS_EPS) * onorm
        for h in heads:
            zf = head_cols(z_ref, rows, h).astype(F32)
            o_ref[0, rows, h * dh:(h + 1) * dh] = (o[h] * (zf * _sigmoid(zf))).astype(o_ref.dtype)
        return carry

    lax.fori_loop(0, q_ref.shape[1] // c_len, chunk, 0)


def _dn_chunk(qn, kn, vv, proj, gates, onorm_w):
    bsz, seq, kd = qn.shape
    vd = vv.shape[2]
    n_chunks = seq // DN_CHUNK
    cb = min(CHUNKS_PER_STEP, n_chunks)
    rows = cb * DN_CHUNK
    gates_t = jnp.swapaxes(gates[:, :, :2 * DN_V_HEADS].reshape(bsz, n_chunks, DN_CHUNK, 2 * DN_V_HEADS), 2, 3)
    z_col = (2 * kd + vd) // vd
    return pl.pallas_call(
        _dn_chunk_kernel,
        grid=(bsz, n_chunks // cb),
        in_specs=[
            pl.BlockSpec((1, rows, kd), lambda b, n: (b, n, 0)),
            pl.BlockSpec((1, rows, kd), lambda b, n: (b, n, 0)),
            pl.BlockSpec((1, rows, vd), lambda b, n: (b, n, 0)),
            pl.BlockSpec((1, rows, vd), lambda b, n: (b, n, z_col)),
            pl.BlockSpec((1, rows, LANES), lambda b, n: (b, n, 0)),
            pl.BlockSpec((1, cb, 2 * DN_V_HEADS, DN_CHUNK), lambda b, n: (b, n, 0, 0)),
            pl.BlockSpec((1, DN_HEAD_DIM), lambda b, n: (0, 0)),
        ],
        out_specs=pl.BlockSpec((1, rows, vd), lambda b, n: (b, n, 0)),
        out_shape=jax.ShapeDtypeStruct((bsz, seq, vd), BF16),
        scratch_shapes=[pltpu.VMEM((DN_V_HEADS, DN_HEAD_DIM, DN_HEAD_DIM), F32)],
        compiler_params=_cparams(("parallel", "arbitrary")),
        name="dn_chunk",
    )(qn, kn, vv, proj, gates, gates_t, onorm_w.reshape(1, DN_HEAD_DIM))


def _residual_ln(x, y, gate_row, ln_g, ln_b):
    return _layer_norm(DEEPNORM_ALPHA * x + (1.0 + gate_row) * y, ln_g, ln_b)


def _route_store(h, rw_ref, rb_ref, idx_ref, gate_ref):
    logits = jnp.dot(h, rw_ref[...], precision=HIGHEST, preferred_element_type=F32) + rb_ref[...]
    lane = lax.broadcasted_iota(jnp.int32, logits.shape, 1).astype(F32)
    work = logits
    idx_out = jnp.zeros(logits.shape, F32)
    val_out = jnp.full(logits.shape, NEG_BIG, F32)
    for k in range(TOP_K):
        m = jnp.max(work, -1, keepdims=True)
        am = jnp.min(jnp.where(work == m, lane, float(LANES)), -1, keepdims=True)
        idx_out = jnp.where(lane == k, am, idx_out)
        val_out = jnp.where(lane == k, m, val_out)
        work = jnp.where(lane == am, NEG_BIG * 2.0, work)
    top = jnp.max(val_out, -1, keepdims=True)
    e = jnp.where(lane < TOP_K, jnp.exp(val_out - top), 0.0)
    idx_ref[...] = idx_out.astype(jnp.int32)
    gate_ref[...] = e / jnp.sum(e, -1, keepdims=True)


def _post_mixer_tail(x, y, mod_ref, lng_ref, lnb_ref, rw_ref, rb_ref, x1_ref, h3_ref, idx_ref, gate_ref):
    x1 = _residual_ln(x, y, mod_ref[0, 2:3, :], lng_ref[...], lnb_ref[...])
    x1_ref[...] = x1
    h2 = x1 * (1.0 + mod_ref[0, 4:5, :]) + mod_ref[0, 3:4, :]
    _rows_to_tiles(h3_ref, h2)
    _route_store(h2, rw_ref, rb_ref, idx_ref, gate_ref)


def _router_operands(router_w, router_b):
    d = router_w.shape[0]
    rw = jnp.zeros((d, LANES), F32).at[:, :N_EXPERTS].set(router_w)
    rb = jnp.full((1, LANES), NEG_BIG, F32).at[0, :N_EXPERTS].set(router_b)
    return rw, rb


def _post_out_specs(tm, d):
    return [
        pl.BlockSpec((tm, d), lambda i: (i, 0)),
        pl.BlockSpec((tm, d // LANES, LANES), lambda i: (i, 0, 0)),
        pl.BlockSpec((tm, LANES), lambda i: (i, 0)),
        pl.BlockSpec((tm, LANES), lambda i: (i, 0)),
    ]


def _post_out_shapes(t, d):
    return [
        jax.ShapeDtypeStruct((t, d), F32),
        jax.ShapeDtypeStruct((t, d // LANES, LANES), F32),
        jax.ShapeDtypeStruct((t, LANES), jnp.int32),
        jax.ShapeDtypeStruct((t, LANES), F32),
    ]


def _dn_out_kernel(o_ref, w_ref, x_ref, mod_ref, lng_ref, lnb_ref, rw_ref, rb_ref,
                   x1_ref, h3_ref, idx_ref, gate_ref):
    y = jnp.dot(o_ref[...], w_ref[...], preferred_element_type=F32)
    _post_mixer_tail(x_ref[...], y, mod_ref, lng_ref, lnb_ref, rw_ref, rb_ref,
                     x1_ref, h3_ref, idx_ref, gate_ref)


def _dn_out(o, out_w, x, mod, ln_g, ln_b, router_w, router_b):
    bsz, seq, d = x.shape
    t = bsz * seq
    vd = o.shape[2]
    tm = min(POST_TM, seq)
    rw, rb = _router_operands(router_w, router_b)
    per_b = seq // tm
    return pl.pallas_call(
        _dn_out_kernel,
        grid=(t // tm,),
        in_specs=[
            pl.BlockSpec((tm, vd), lambda i: (i, 0)),
            pl.BlockSpec((vd, d), lambda i: (0, 0)),
            pl.BlockSpec((tm, d), lambda i: (i, 0)),
            pl.BlockSpec((1, 6, d), lambda i: (i // per_b, 0, 0)),
            pl.BlockSpec((1, d), lambda i: (0, 0)),
            pl.BlockSpec((1, d), lambda i: (0, 0)),
            pl.BlockSpec((d, LANES), lambda i: (0, 0)),
            pl.BlockSpec((1, LANES), lambda i: (0, 0)),
        ],
        out_specs=_post_out_specs(tm, d),
        out_shape=_post_out_shapes(t, d),
        compiler_params=_cparams(("parallel",)),
        name="dn_out_ln_route",
    )(o.reshape(t, vd), out_w.astype(BF16), x.reshape(t, d), mod, ln_g.reshape(1, d), ln_b.reshape(1, d), rw, rb)


def _cf_pw1_kernel(x_ref, mod_ref, wa_ref, wb_ref, ba_ref, bb_ref, u_ref):
    h = (x_ref[0] * (1.0 + mod_ref[0, 1:2, :]) + mod_ref[0, 0:1, :]).astype(BF16)
    pa = jnp.dot(h, wa_ref[...], preferred_element_type=F32) + ba_ref[...]
    pb = jnp.dot(h, wb_ref[...], preferred_element_type=F32) + bb_ref[...]
    u_ref[0] = (pa * _sigmoid(pb)).astype(u_ref.dtype)


def _cf_pw1(x, mod, pw1_w, pw1_b):
    bsz, seq, d = x.shape
    inner = pw1_w.shape[1] // 2
    tm = min(PROJ_TM, seq)
    w = pw1_w.astype(BF16)
    b = pw1_b.reshape(1, 2 * inner)
    return pl.pallas_call(
        _cf_pw1_kernel,
        grid=(bsz, seq // tm),
        in_specs=[
            pl.BlockSpec((1, tm, d), lambda b_, i: (b_, i, 0)),
            pl.BlockSpec((1, 6, d), lambda b_, i: (b_, 0, 0)),
            pl.BlockSpec((d, inner), lambda b_, i: (0, 0)),
            pl.BlockSpec((d, inner), lambda b_, i: (0, 1)),
            pl.BlockSpec((1, inner), lambda b_, i: (0, 0)),
            pl.BlockSpec((1, inner), lambda b_, i: (0, 1)),
        ],
        out_specs=pl.BlockSpec((1, tm, inner), lambda b_, i: (b_, i, 0)),
        out_shape=jax.ShapeDtypeStruct((bsz, seq, inner), BF16),
        compiler_params=_cparams(("parallel", "parallel")),
        name="cf_pw1_glu",
    )(x, mod, w, w, b, b)


def _cf_tail_kernel(u_ref, uh_ref, dw_ref, dwb_ref, cg_ref, cb_ref, w2_ref, b2_ref,
                    x_ref, mod_ref, lng_ref, lnb_ref, rw_ref, rb_ref,
                    x1_ref, h3_ref, idx_ref, gate_ref, scr):
    tm = u_ref.shape[1]
    halo = uh_ref.shape[1]
    keep = (pl.program_id(1) > 0).astype(F32)
    scr[0:halo, :] = uh_ref[0].astype(F32) * keep
    scr[halo:halo + tm, :] = u_ref[0].astype(F32)
    acc = None
    for j in range(CF_KERNEL):
        off = halo - (CF_KERNEL - 1) + j
        term = scr[off:off + tm, :] * dw_ref[j:j + 1, :]
        acc = term if acc is None else acc + term
    conv = acc + dwb_ref[...]
    normed = _layer_norm(conv, cg_ref[...], cb_ref[...])
    act = normed * _sigmoid(normed)
    y = jnp.dot(act.astype(BF16), w2_ref[...], preferred_element_type=F32) + b2_ref[...]
    _post_mixer_tail(x_ref[0], y, mod_ref, lng_ref, lnb_ref, rw_ref, rb_ref,
                     x1_ref, h3_ref, idx_ref, gate_ref)


def _cf_tail(u, dw_w, dw_b, cf_ln_g, cf_ln_b, pw2_w, pw2_b, x, mod, ln_g, ln_b, router_w, router_b):
    bsz, seq, d = x.shape
    inner = u.shape[2]
    t = bsz * seq
    tm = min(CF_TM, seq)
    per_b = seq // tm
    hb = tm // CF_HALO
    rw, rb = _router_operands(router_w, router_b)
    row = lambda v: v.reshape(1, -1)
    const = lambda b, i: (0, 0)
    flat = lambda b, i: (b * per_b + i, 0)
    return pl.pallas_call(
        _cf_tail_kernel,
        grid=(bsz, per_b),
        in_specs=[
            pl.BlockSpec((1, tm, inner), lambda b, i: (b, i, 0)),
            pl.BlockSpec((1, CF_HALO, inner), lambda b, i: (b, jnp.maximum(i * hb - 1, 0), 0)),
            pl.BlockSpec((CF_KERNEL, inner), const),
            pl.BlockSpec((1, inner), const),
            pl.BlockSpec((1, inner), const),
            pl.BlockSpec((1, inner), const),
            pl.BlockSpec((inner, d), const),
            pl.BlockSpec((1, d), const),
            pl.BlockSpec((1, tm, d), lambda b, i: (b, i, 0)),
            pl.BlockSpec((1, 6, d), lambda b, i: (b, 0, 0)),
            pl.BlockSpec((1, d), const),
            pl.BlockSpec((1, d), const),
            pl.BlockSpec((d, LANES), const),
            pl.BlockSpec((1, LANES), const),
        ],
        out_specs=[
            pl.BlockSpec((tm, d), flat),
            pl.BlockSpec((tm, d // LANES, LANES), lambda b, i: (b * per_b + i, 0, 0)),
            pl.BlockSpec((tm, LANES), flat),
            pl.BlockSpec((tm, LANES), flat),
        ],
        out_shape=_post_out_shapes(t, d),
        scratch_shapes=[pltpu.VMEM((tm + CF_HALO, inner), F32)],
        compiler_params=_cparams(("parallel", "arbitrary")),
        name="cf_conv_ln_pw2_route",
    )(u, u, dw_w, row(dw_b), row(cf_ln_g), row(cf_ln_b), pw2_w.astype(BF16), row(pw2_b),
      x, mod, row(ln_g), row(ln_b), rw, rb)


def _rank_kernel(idx_ref, rank_ref, cnt_ref, carry):
    tt = idx_ref.shape[0]

    @pl.when(pl.program_id(0) == 0)
    def _():
        carry[...] = jnp.zeros(carry.shape, F32)

    idx = idx_ref[...]
    lane = lax.broadcasted_iota(jnp.int32, idx.shape, 1)
    sel = [lane == idx[:, k:k + 1] for k in range(TOP_K)]
    multi_hot = sel[0]
    for k in range(1, TOP_K):
        multi_hot = multi_hot | sel[k]
    mh = multi_hot.astype(BF16)
    r = lax.broadcasted_iota(jnp.int32, (tt, tt), 0)
    c = lax.broadcasted_iota(jnp.int32, (tt, tt), 1)
    before = jnp.dot((c < r).astype(BF16), mh, preferred_element_type=F32) + carry[...]
    rank = jnp.zeros(idx.shape, F32)
    for k in range(TOP_K):
        rk = jnp.sum(jnp.where(sel[k], before, 0.0), -1, keepdims=True)
        rank = jnp.where(lane == k, rk, rank)
    rank_ref[...] = rank.astype(jnp.int32)
    total = carry[...] + jnp.sum(mh.astype(F32), 0, keepdims=True)
    carry[...] = total
    cnt_ref[...] = total.astype(jnp.int32)


def _rank_and_count(idx):
    t = idx.shape[0]
    tt = min(RANK_TT, t)
    return pl.pallas_call(
        _rank_kernel,
        grid=(t // tt,),
        in_specs=[pl.BlockSpec((tt, LANES), lambda i: (i, 0))],
        out_specs=[
            pl.BlockSpec((tt, LANES), lambda i: (i, 0)),
            pl.BlockSpec((1, LANES), lambda i: (0, 0)),
        ],
        out_shape=[
            jax.ShapeDtypeStruct((t, LANES), jnp.int32),
            jax.ShapeDtypeStruct((1, LANES), jnp.int32),
        ],
        scratch_shapes=[pltpu.VMEM((1, LANES), F32)],
        compiler_params=_cparams(("arbitrary",)),
        name="moe_rank_count",
    )(idx)


def _num_expert_blocks(n_assign):
    bound = n_assign + N_EXPERTS * (EXPERT_TM - 1)
    return -(-bound // EXPERT_TM)


def _routing_tables(idx, rank, counts):
    t = idx.shape[0]
    nb = _num_expert_blocks(t * TOP_K)
    counts = counts[0, :N_EXPERTS]
    padded = (counts + EXPERT_TM - 1) // EXPERT_TM * EXPERT_TM
    pends = jnp.cumsum(padded)
    pstarts = pends - padded
    e = idx[:, :TOP_K]
    dest = (pstarts[e] + rank[:, :TOP_K]).astype(jnp.int32).reshape(t * TOP_K)
    block_row0 = jnp.arange(nb, dtype=jnp.int32) * EXPERT_TM
    segments_done = jnp.sum((pends[None, :] <= block_row0[:, None]).astype(jnp.int32), axis=1)
    block_e = jnp.minimum(segments_done, N_EXPERTS - 1)
    live = jnp.clip(counts[block_e] - (block_row0 - pstarts[block_e]), 0, EXPERT_TM).astype(jnp.int32)
    return dest, block_e, live


def _pad_fill_copies(live_ref, i, zeros, xb_hbm, sem):
    lv = live_ref[i]
    pad = EXPERT_TM - lv
    row0 = i * EXPERT_TM + lv
    copies = []
    piece = EXPERT_TM
    while piece >= 1:
        offset = pad & ~(2 * piece - 1)
        copy = pltpu.make_async_copy(zeros.at[pl.ds(0, piece)], xb_hbm.at[pl.ds(row0 + offset, piece)], sem)
        copies.append(((pad & piece) != 0, copy))
        piece //= 2
    return copies


def _dispatch_kernel(dest_ref, live_ref, h_ref, xb_hbm, zeros, sem, fill_sem):
    tt = h_ref.shape[0]
    base = pl.program_id(0) * tt
    n_blocks = xb_hbm.shape[0] // EXPERT_TM

    @pl.when(pl.program_id(0) == 0)
    def _():
        zeros[...] = jnp.zeros(zeros.shape, F32)

        def start_fill(i, carry):
            for live_bit, copy in _pad_fill_copies(live_ref, i, zeros, xb_hbm, fill_sem):
                pl.when(live_bit)(copy.start)
            return carry

        lax.fori_loop(0, n_blocks, start_fill, 0)

    def issue(t, carry):
        for k in range(TOP_K):
            pltpu.make_async_copy(h_ref.at[t], xb_hbm.at[dest_ref[(base + t) * TOP_K + k]], sem).start()
        return carry

    lax.fori_loop(0, tt, issue, 0)
    for _ in range(TOP_K):
        pltpu.make_async_copy(h_ref, xb_hbm.at[pl.ds(0, tt)], sem).wait()

    @pl.when(pl.program_id(0) == 0)
    def _():
        def wait_fill(i, carry):
            for live_bit, copy in _pad_fill_copies(live_ref, i, zeros, xb_hbm, fill_sem):
                pl.when(live_bit)(copy.wait)
            return carry

        lax.fori_loop(0, n_blocks, wait_fill, 0)


def _dispatch(dest, live, h3, n_rows):
    t, s, l = h3.shape
    return pl.pallas_call(
        _dispatch_kernel,
        grid_spec=pltpu.PrefetchScalarGridSpec(
            num_scalar_prefetch=2,
            grid=(t // DISPATCH_TT,),
            in_specs=[pl.BlockSpec((DISPATCH_TT, s, l), lambda i, dst, lv: (i, 0, 0))],
            out_specs=pl.BlockSpec(memory_space=pl.ANY),
            scratch_shapes=[pltpu.VMEM((EXPERT_TM, s, l), F32), pltpu.SemaphoreType.DMA,
                            pltpu.SemaphoreType.DMA],
        ),
        out_shape=jax.ShapeDtypeStruct((n_rows, s, l), F32),
        compiler_params=_cparams(("arbitrary",)),
        name="moe_dispatch",
    )(dest, live, h3)


def _expert_kernel(be_ref, live_ref, x_ref, w1_ref, b1_ref, w2_ref, b2_ref, y_ref, w1s, w2s):
    i = pl.program_id(0)
    live = live_ref[i]
    e = be_ref[i]
    e_prev = be_ref[jnp.maximum(i - 1, 0)]

    @pl.when((i == 0) | (e != e_prev))
    def _():
        w1s[...] = w1_ref[0, 0].astype(BF16)
        w2s[...] = w2_ref[0, 0].astype(BF16)

    @pl.when(live > 0)
    def _():
        dff = w2s.shape[0]
        x = _tiles_to_rows(x_ref)
        row = lax.broadcasted_iota(jnp.int32, (x.shape[0], 1), 0)
        x = jnp.where(row < live, x, 0.0).astype(BF16)
        hh = jnp.dot(x, w1s[...], preferred_element_type=F32) + b1_ref[0, 0]
        x_glu = jnp.minimum(hh[:, :dff], SWIGLU_LIMIT)
        x_lin = jnp.clip(hh[:, dff:], -SWIGLU_LIMIT, SWIGLU_LIMIT)
        act = x_glu * _sigmoid(SWIGLU_ALPHA * x_glu) * (x_lin + 1.0)
        y = jnp.dot(act.astype(BF16), w2s[...], preferred_element_type=F32) + b2_ref[0, 0]
        _rows_to_tiles(y_ref, y)

    @pl.when(live == 0)
    def _():
        y_ref[...] = jnp.zeros(y_ref.shape, F32)


def _experts(block_e, live, xb, layer, w1, b1, w2, b2):
    n_rows, s, l = xb.shape
    depth, n_e, d, two_f = w1.shape
    dff = two_f // 2
    nb = n_rows // EXPERT_TM
    return pl.pallas_call(
        _expert_kernel,
        grid_spec=pltpu.PrefetchScalarGridSpec(
            num_scalar_prefetch=2,
            grid=(nb,),
            in_specs=[
                pl.BlockSpec((EXPERT_TM, s, l), lambda i, be, lv: (i, 0, 0)),
                pl.BlockSpec((1, 1, d, two_f), lambda i, be, lv: (layer, be[i], 0, 0)),
                pl.BlockSpec((1, 1, 1, two_f), lambda i, be, lv: (layer, be[i], 0, 0)),
                pl.BlockSpec((1, 1, dff, d), lambda i, be, lv: (layer, be[i], 0, 0)),
                pl.BlockSpec((1, 1, 1, d), lambda i, be, lv: (layer, be[i], 0, 0)),
            ],
            out_specs=pl.BlockSpec((EXPERT_TM, s, l), lambda i, be, lv: (i, 0, 0)),
            scratch_shapes=[pltpu.VMEM((d, two_f), BF16), pltpu.VMEM((dff, d), BF16)],
        ),
        out_shape=jax.ShapeDtypeStruct((n_rows, s, l), F32),
        compiler_params=_cparams(("arbitrary",)),
        name="moe_experts",
    )(block_e, live, xb, w1, b1.reshape(depth, n_e, 1, two_f), w2, b2.reshape(depth, n_e, 1, d))


def _combine_kernel(dest_ref, yb_hbm, gate_ref, x_ref, mod_ref, lng_ref, lnb_ref, o_ref, buf, sem):
    tt = x_ref.shape[0]
    base = pl.program_id(0) * tt

    def issue(t, carry):
        for k in range(TOP_K):
            pltpu.make_async_copy(yb_hbm.at[dest_ref[(base + t) * TOP_K + k]], buf.at[k, t], sem).start()
        return carry

    lax.fori_loop(0, tt, issue, 0)
    for k in range(TOP_K):
        pltpu.make_async_copy(yb_hbm.at[pl.ds(0, tt)], buf.at[k], sem).wait()

    gates = gate_ref[...]
    y = None
    for k in range(TOP_K):
        term = gates[:, k:k + 1] * _tiles_to_rows(buf.at[k])
        y = term if y is None else y + term
    o_ref[...] = _residual_ln(x_ref[...], y, mod_ref[0, 5:6, :], lng_ref[...], lnb_ref[...])


def _combine(dest, yb, gates, x1, mod, ln_g, ln_b, seq):
    t, d = x1.shape
    _, s, l = yb.shape
    tt = min(COMBINE_TT, seq)
    per_b = seq // tt
    return pl.pallas_call(
        _combine_kernel,
        grid_spec=pltpu.PrefetchScalarGridSpec(
            num_scalar_prefetch=1,
            grid=(t // tt,),
            in_specs=[
                pl.BlockSpec(memory_space=pl.ANY),
                pl.BlockSpec((tt, LANES), lambda i, dst: (i, 0)),
                pl.BlockSpec((tt, d), lambda i, dst: (i, 0)),
                pl.BlockSpec((1, 6, d), lambda i, dst: (i // per_b, 0, 0)),
                pl.BlockSpec((1, d), lambda i, dst: (0, 0)),
                pl.BlockSpec((1, d), lambda i, dst: (0, 0)),
            ],
            out_specs=pl.BlockSpec((tt, d), lambda i, dst: (i, 0)),
            scratch_shapes=[pltpu.VMEM((TOP_K, tt, s, l), F32), pltpu.SemaphoreType.DMA],
        ),
        out_shape=jax.ShapeDtypeStruct((t, d), F32),
        compiler_params=_cparams(("arbitrary",)),
        name="moe_combine_ln",
    )(dest, yb, gates, x1, mod, ln_g.reshape(1, d), ln_b.reshape(1, d))


def _moe_block(x1, h3, idx, gates, mod, layer, w1, b1, w2, b2, ln_g, ln_b, seq):
    t = x1.shape[0]
    rank, counts = _rank_and_count(idx)
    dest, block_e, live = _routing_tables(idx, rank, counts)
    n_rows = _num_expert_blocks(t * TOP_K) * EXPERT_TM
    xb = _dispatch(dest, live, h3, n_rows)
    yb = _experts(block_e, live, xb, layer, w1, b1, w2, b2)
    return _combine(dest, yb, gates, x1, mod, ln_g, ln_b, seq)


def kernel(x, c, ada_w, ada_b, dn_in_w, dn_conv_w, dn_A_log, dn_dt_bias, dn_onorm_w, dn_out_w,
           cf_pw1_w, cf_pw1_b, cf_dw_w, cf_dw_b, cf_ln_g, cf_ln_b, cf_pw2_w, cf_pw2_b,
           ln1_g, ln1_b, router_w, router_b, e_w1, e_b1, e_w2, e_b2, ln2_g, ln2_b):
    bsz, seq, d = x.shape
    mods = _ada_ln(c, ada_w, ada_b)

    proj, ba = _dn_in_proj(x, mods[0], dn_in_w[0])
    qn, kn, vv, gates_dn = _dn_prep(proj, ba, dn_conv_w[0], dn_A_log[0], dn_dt_bias[0])
    o = _dn_chunk(qn, kn, vv, proj, gates_dn, dn_onorm_w[0])
    x1, h3, idx, gates = _dn_out(o, dn_out_w[0], x, mods[0], ln1_g[0], ln1_b[0], router_w[0], router_b[0])
    x2 = _moe_block(x1, h3, idx, gates, mods[0], 0, e_w1, e_b1, e_w2, e_b2, ln2_g[0], ln2_b[0], seq)

    x2 = x2.reshape(bsz, seq, d)
    u = _cf_pw1(x2, mods[1], cf_pw1_w[0], cf_pw1_b[0])
    x3, h3, idx, gates = _cf_tail(u, cf_dw_w[0], cf_dw_b[0], cf_ln_g[0], cf_ln_b[0], cf_pw2_w[0], cf_pw2_b[0],
                                  x2, mods[1], ln1_g[1], ln1_b[1], router_w[1], router_b[1])
    x4 = _moe_block(x3, h3, idx, gates, mods[1], 1, e_w1, e_b1, e_w2, e_b2, ln2_g[1], ln2_b[1], seq)
    return x4.reshape(bsz, seq, d)
```

```python
import functools

import jax
import jax.numpy as jnp
from jax import lax
from jax.experimental import pallas as pl
from jax.experimental.pallas import tpu as pltpu

F32 = jnp.float32
BF16 = jnp.bfloat16
HIGHEST = lax.Precision.HIGHEST

DEPTH = 2
DN_QK_HEADS = 8
DN_V_HEADS = 16
DN_HEAD_DIM = 128
DN_CONV = 4
DN_CHUNK = 64
CF_KERNEL = 31
N_EXPERTS = 32
TOP_K = 4
SWIGLU_LIMIT = 7.0
SWIGLU_ALPHA = 1.702
LN_EPS = 1e-5
RMS_EPS = 1e-6
L2_EPS = 1e-6
DEEPNORM_ALPHA = (2 * DEPTH) ** 0.25

LANES = 128
SUBLANES = 8
BF16_SUBLANES = 16
VMEM_LIMIT = 56 * 1024 * 1024

ADA_TN = 1536
PROJ_TM = 1024
PROJ_TN = 1024
PREP_TS = 512
CHUNKS_PER_STEP = 4
POST_TM = 512
RANK_TT = 512
DISPATCH_TT = 512
EXPERT_TM = 512
COMBINE_TT = 256
CF_TM = 512
CF_HALO = 32
CF_CONV_ROWS = 32

NEG_BIG = -1e30


def _cparams(sem):
    return pltpu.CompilerParams(dimension_semantics=sem, vmem_limit_bytes=VMEM_LIMIT)


def _sigmoid(x):
    return jax.nn.sigmoid(x)


def _layer_norm(v, g, b):
    mu = jnp.mean(v, -1, keepdims=True)
    d = v - mu
    var = jnp.mean(d * d, -1, keepdims=True)
    return d * lax.rsqrt(var + LN_EPS) * g + b


ROW_TILE = SUBLANES


def _rows_to_tiles(o_ref, val):
    rows = val.shape[0]
    for s in range(ROW_TILE):
        o_ref[pl.ds(s, rows, stride=ROW_TILE), :] = val[:, s * LANES:(s + 1) * LANES]


def _tiles_to_rows(x_ref):
    rows = x_ref.shape[0] // ROW_TILE
    return jnp.concatenate([x_ref[pl.ds(s, rows, stride=ROW_TILE), :] for s in range(ROW_TILE)], axis=1)


def _split_weight(w):
    k, n = w.shape
    hi = w.astype(BF16)
    lo = (w - hi.astype(F32)).astype(BF16)
    out = jnp.zeros((k, 2 * LANES), BF16)
    return out.at[:, :n].set(hi).at[:, LANES:LANES + n].set(lo)


def _narrow_dot(h, w_split_ref):
    m = h.shape[0]
    hi = h.astype(BF16)
    lo = (h - hi.astype(F32)).astype(BF16)
    out = jnp.dot(jnp.concatenate([hi, lo], axis=0), w_split_ref[...], preferred_element_type=F32)
    return out[:m, :LANES] + out[:m, LANES:] + out[m:, :LANES]


def _token_tile(ref, tok):
    return ref.at[pl.ds(pl.multiple_of(tok * ROW_TILE, ROW_TILE), ROW_TILE), :]


def _ada_kernel(c_ref, w_ref, b_ref, o_ref):
    c = c_ref[...]
    cond = c * _sigmoid(c)
    o_ref[0] = jnp.dot(cond, w_ref[0], precision=HIGHEST, preferred_element_type=F32) + b_ref[0]


def _ada_ln(c, ada_w, ada_b):
    depth, d, n = ada_w.shape
    bsz = c.shape[0]
    c_pad = jnp.zeros((SUBLANES, d), F32).at[:bsz].set(c)
    out = pl.pallas_call(
        _ada_kernel,
        grid=(depth, n // ADA_TN),
        in_specs=[
            pl.BlockSpec((SUBLANES, d), lambda i, j: (0, 0)),
            pl.BlockSpec((1, d, ADA_TN), lambda i, j: (i, 0, j)),
            pl.BlockSpec((1, 1, ADA_TN), lambda i, j: (i, 0, j)),
        ],
        out_specs=pl.BlockSpec((1, SUBLANES, ADA_TN), lambda i, j: (i, 0, j)),
        out_shape=jax.ShapeDtypeStruct((depth, SUBLANES, n), F32),
        compiler_params=_cparams(("parallel", "parallel")),
        name="ada_ln",
    )(c_pad, ada_w, ada_b.reshape(depth, 1, n))
    return out[:, :bsz].reshape(depth, bsz, 6, d)


def _inproj_kernel(x_ref, mod_ref, w_ref, wba_ref, proj_ref, ba_ref, h_scr):
    @pl.when(pl.program_id(2) == 0)
    def _():
        h = x_ref[0] * (1.0 + mod_ref[0, 1:2, :]) + mod_ref[0, 0:1, :]
        h_scr[...] = h.astype(BF16)
        ba_ref[0] = _narrow_dot(h, wba_ref)

    proj_ref[0] = jnp.dot(h_scr[...], w_ref[...], preferred_element_type=F32).astype(BF16)


def _dn_in_proj(x, mod, in_w):
    bsz, seq, d = x.shape
    n_main = in_w.shape[1] - 2 * DN_V_HEADS
    w_main = in_w[:, :n_main].astype(BF16)
    w_ba = _split_weight(in_w[:, n_main:])
    tm = min(PROJ_TM, seq)
    return pl.pallas_call(
        _inproj_kernel,
        grid=(bsz, seq // tm, n_main // PROJ_TN),
        in_specs=[
            pl.BlockSpec((1, tm, d), lambda b, i, j: (b, i, 0)),
            pl.BlockSpec((1, 6, d), lambda b, i, j: (b, 0, 0)),
            pl.BlockSpec((d, PROJ_TN), lambda b, i, j: (0, j)),
            pl.BlockSpec((d, 2 * LANES), lambda b, i, j: (0, 0)),
        ],
        out_specs=[
            pl.BlockSpec((1, tm, PROJ_TN), lambda b, i, j: (b, i, j)),
            pl.BlockSpec((1, tm, LANES), lambda b, i, j: (b, i, 0)),
        ],
        out_shape=[
            jax.ShapeDtypeStruct((bsz, seq, n_main), BF16),
            jax.ShapeDtypeStruct((bsz, seq, LANES), F32),
        ],
        scratch_shapes=[pltpu.VMEM((tm, d), BF16)],
        compiler_params=_cparams(("parallel", "parallel", "arbitrary")),
        name="dn_in_proj",
    )(x, mod, w_main, w_ba)


def _dn_prep_kernel(q_ref, k_ref, v_ref, qh_ref, kh_ref, vh_ref, cw_ref, ba_ref, alog_ref, dt_ref,
                    qo_ref, ko_ref, vo_ref, g_ref, scr):
    ts = q_ref.shape[1]
    halo = qh_ref.shape[1]
    keep = (pl.program_id(1) > 0).astype(F32)

    def conv_silu(x_ref, h_ref, c0):
        width = x_ref.shape[2]
        scr[0:halo, 0:width] = h_ref[0].astype(F32) * keep
        scr[halo:halo + ts, 0:width] = x_ref[0].astype(F32)
        acc = None
        for j in range(DN_CONV):
            off = halo - (DN_CONV - 1) + j
            term = scr[off:off + ts, 0:width] * cw_ref[j:j + 1, c0:c0 + width]
            acc = term if acc is None else acc + term
        return acc * _sigmoid(acc)

    def l2norm_store(o_ref, x, scale):
        for h in range(x.shape[1] // DN_HEAD_DIM):
            xh = x[:, h * DN_HEAD_DIM:(h + 1) * DN_HEAD_DIM]
            ss = jnp.sum(xh * xh, -1, keepdims=True)
            o_ref[0, :, h * DN_HEAD_DIM:(h + 1) * DN_HEAD_DIM] = (
                xh * lax.rsqrt(ss + L2_EPS) * scale).astype(o_ref.dtype)

    kd = q_ref.shape[2]
    l2norm_store(qo_ref, conv_silu(q_ref, qh_ref, 0), DN_HEAD_DIM ** -0.5)
    l2norm_store(ko_ref, conv_silu(k_ref, kh_ref, kd), 1.0)
    vo_ref[0] = conv_silu(v_ref, vh_ref, 2 * kd).astype(vo_ref.dtype)

    ba = ba_ref[0]
    beta = _sigmoid(ba)
    zz = ba + dt_ref[...]
    softplus = jnp.maximum(zz, 0.0) + jnp.log1p(jnp.exp(-jnp.abs(zz)))
    g = -jnp.exp(alog_ref[...]) * softplus
    r = lax.broadcasted_iota(jnp.int32, (ts, ts), 0)
    c = lax.broadcasted_iota(jnp.int32, (ts, ts), 1)
    in_chunk_tril = ((r // DN_CHUNK == c // DN_CHUNK) & (c <= r)).astype(BF16)
    g1 = g.astype(BF16)
    g2 = (g - g1.astype(F32)).astype(BF16)
    g3 = (g - g1.astype(F32) - g2.astype(F32)).astype(BF16)
    parts = jnp.dot(in_chunk_tril, jnp.concatenate([g1, g2, g3], axis=1), preferred_element_type=F32)
    gcum = parts[:, :LANES] + parts[:, LANES:2 * LANES] + parts[:, 2 * LANES:]
    lane = lax.broadcasted_iota(jnp.int32, ba.shape, 1)
    g_ref[0] = jnp.where(lane < DN_V_HEADS, beta, gcum)


def _dn_prep(proj, ba, conv_w, a_log, dt_bias):
    bsz, seq, _ = proj.shape
    kd = DN_QK_HEADS * DN_HEAD_DIM
    vd = DN_V_HEADS * DN_HEAD_DIM
    ts = min(PREP_TS, seq)
    halo = BF16_SUBLANES
    hb = ts // halo
    alog_row = jnp.zeros((1, LANES), F32).at[0, DN_V_HEADS:2 * DN_V_HEADS].set(a_log)
    dt_row = jnp.zeros((1, LANES), F32).at[0, DN_V_HEADS:2 * DN_V_HEADS].set(dt_bias)

    def halo_map(col):
        return lambda b, i: (b, jnp.maximum(i * hb - 1, 0), col)

    return pl.pallas_call(
        _dn_prep_kernel,
        grid=(bsz, seq // ts),
        in_specs=[
            pl.BlockSpec((1, ts, kd), lambda b, i: (b, i, 0)),
            pl.BlockSpec((1, ts, kd), lambda b, i: (b, i, 1)),
            pl.BlockSpec((1, ts, vd), lambda b, i: (b, i, 1)),
            pl.BlockSpec((1, halo, kd), halo_map(0)),
            pl.BlockSpec((1, halo, kd), halo_map(1)),
            pl.BlockSpec((1, halo, vd), halo_map(1)),
            pl.BlockSpec((DN_CONV, 2 * kd + vd), lambda b, i: (0, 0)),
            pl.BlockSpec((1, ts, LANES), lambda b, i: (b, i, 0)),
            pl.BlockSpec((1, LANES), lambda b, i: (0, 0)),
            pl.BlockSpec((1, LANES), lambda b, i: (0, 0)),
        ],
        out_specs=[
            pl.BlockSpec((1, ts, kd), lambda b, i: (b, i, 0)),
            pl.BlockSpec((1, ts, kd), lambda b, i: (b, i, 0)),
            pl.BlockSpec((1, ts, vd), lambda b, i: (b, i, 0)),
            pl.BlockSpec((1, ts, LANES), lambda b, i: (b, i, 0)),
        ],
        out_shape=[
            jax.ShapeDtypeStruct((bsz, seq, kd), BF16),
            jax.ShapeDtypeStruct((bsz, seq, kd), BF16),
            jax.ShapeDtypeStruct((bsz, seq, vd), BF16),
            jax.ShapeDtypeStruct((bsz, seq, LANES), F32),
        ],
        scratch_shapes=[pltpu.VMEM((ts + halo, vd), F32)],
        compiler_params=_cparams(("parallel", "arbitrary")),
        name="dn_prep",
    )(proj, proj, proj, proj, proj, proj, conv_w, ba, alog_row, dt_row)


def _bmm(a, b):
    return lax.dot_general(a.astype(BF16), b.astype(BF16), (((2,), (1,)), ((0,), (0,))),
                           preferred_element_type=F32)


def _bmm_nt(a, b):
    return lax.dot_general(a.astype(BF16), b.astype(BF16), (((2,), (2,)), ((0,), (0,))),
                           preferred_element_type=F32)


def _bmm_tn(a, b):
    return lax.dot_general(a.astype(BF16), b.astype(BF16), (((1,), (1,)), ((0,), (0,))),
                           preferred_element_type=F32)


def _unit_lower_inverse(a_strict, eye):
    c = a_strict.shape[-1]
    b_pow = -a_strict
    p = eye + b_pow
    b_pow = _bmm(b_pow, b_pow)
    span = 2
    while 2 * span < c:
        both = _bmm(b_pow, jnp.concatenate([b_pow, p], axis=2))
        p = p + both[:, :, c:]
        b_pow = both[:, :, :c]
        span *= 2
    return p + _bmm(b_pow, p)


def _dn_chunk_kernel(q_ref, k_ref, v_ref, z_ref, g_ref, gt_ref, ow_ref, o_ref, s_ref):
    c_len = DN_CHUNK
    dh = DN_HEAD_DIM
    rep = DN_V_HEADS // DN_QK_HEADS

    @pl.when(pl.program_id(1) == 0)
    def _():
        s_ref[...] = jnp.zeros(s_ref.shape, F32)

    heads = range(DN_V_HEADS)
    ri = lax.broadcasted_iota(jnp.int32, (1, c_len, c_len), 1)
    ci = lax.broadcasted_iota(jnp.int32, (1, c_len, c_len), 2)
    causal = ci <= ri
    strict = ci < ri
    eye = (ci == ri).astype(F32)
    onorm = ow_ref[...]

    def head_cols(ref, rows, h):
        return ref[0, rows, h * dh:(h + 1) * dh]

    def per_v_head(t):
        return jnp.stack([t[h // rep] for h in heads])

    def chunk(c, carry):
        r0 = pl.multiple_of(c * c_len, c_len)
        rows = pl.ds(r0, c_len)
        gcols = g_ref[0, rows, :]
        grows = gt_ref[0, c]
        beta_c = jnp.stack([gcols[:, h:h + 1] for h in heads])
        g_c = jnp.stack([gcols[:, DN_V_HEADS + h:DN_V_HEADS + h + 1] for h in heads])
        g_r = jnp.stack([grows[DN_V_HEADS + h:DN_V_HEADS + h + 1, :] for h in heads])
        g_last = g_c[:, c_len - 1:c_len, :]

        qn = jnp.stack([head_cols(q_ref, rows, hq) for hq in range(DN_QK_HEADS)])
        kn = jnp.stack([head_cols(k_ref, rows, hq) for hq in range(DN_QK_HEADS)])
        qk_kk = _bmm_nt(jnp.concatenate([qn, kn], axis=1), kn)
        qk = per_v_head(qk_kk[:, :c_len])
        kk = per_v_head(qk_kk[:, c_len:])
        qf = per_v_head(qn).astype(F32)
        kf = per_v_head(kn).astype(F32)
        vf = jnp.stack([head_cols(v_ref, rows, h) for h in heads]).astype(F32)

        decay = jnp.where(causal, jnp.exp(g_c - g_r), 0.0)
        a = jnp.where(strict, beta_c * kk * decay, 0.0)
        t_inv = _unit_lower_inverse(a, eye)
        eg = jnp.exp(g_c)
        sol = _bmm(t_inv, jnp.concatenate([vf * beta_c, kf * (beta_c * eg)], axis=2))
        u = sol[:, :, :dh]
        w = sol[:, :, dh:]
        qkm = jnp.where(causal, qk * decay, 0.0)
        qg = qf * eg
        kdec = kf * jnp.exp(g_last - g_c)

        state = s_ref[...]
        ws = _bmm(jnp.concatenate([w, qg], axis=1), state)
        v_new = u - ws[:, :c_len]
        o = ws[:, c_len:] + _bmm(qkm, v_new)
        s_ref[...] = state * jnp.exp(g_last) + _bmm_tn(kdec, v_new)

        o = o * lax.rsqrt(jnp.mean(o * o, -1, keepdims=True) + RMS_EPS) * onorm
        for h in heads:
            zf = head_cols(z_ref, rows, h).astype(F32)
            o_ref[0, rows, h * dh:(h + 1) * dh] = (o[h] * (zf * _sigmoid(zf))).astype(o_ref.dtype)
        return carry

    lax.fori_loop(0, q_ref.shape[1] // c_len, chunk, 0)


def _dn_chunk(qn, kn, vv, proj, gates, onorm_w):
    bsz, seq, kd = qn.shape
    vd = vv.shape[2]
    n_chunks = seq // DN_CHUNK
    cb = min(CHUNKS_PER_STEP, n_chunks)
    rows = cb * DN_CHUNK
    gates_t = jnp.swapaxes(gates[:, :, :2 * DN_V_HEADS].reshape(bsz, n_chunks, DN_CHUNK, 2 * DN_V_HEADS), 2, 3)
    z_col = (2 * kd + vd) // vd
    return pl.pallas_call(
        _dn_chunk_kernel,
        grid=(bsz, n_chunks // cb),
        in_specs=[
            pl.BlockSpec((1, rows, kd), lambda b, n: (b, n, 0)),
            pl.BlockSpec((1, rows, kd), lambda b, n: (b, n, 0)),
            pl.BlockSpec((1, rows, vd), lambda b, n: (b, n, 0)),
            pl.BlockSpec((1, rows, vd), lambda b, n: (b, n, z_col)),
            pl.BlockSpec((1, rows, LANES), lambda b, n: (b, n, 0)),
            pl.BlockSpec((1, cb, 2 * DN_V_HEADS, DN_CHUNK), lambda b, n: (b, n, 0, 0)),
            pl.BlockSpec((1, DN_HEAD_DIM), lambda b, n: (0, 0)),
        ],
        out_specs=pl.BlockSpec((1, rows, vd), lambda b, n: (b, n, 0)),
        out_shape=jax.ShapeDtypeStruct((bsz, seq, vd), BF16),
        scratch_shapes=[pltpu.VMEM((DN_V_HEADS, DN_HEAD_DIM, DN_HEAD_DIM), F32)],
        compiler_params=_cparams(("parallel", "arbitrary")),
        name="dn_chunk",
    )(qn, kn, vv, proj, gates, gates_t, onorm_w.reshape(1, DN_HEAD_DIM))


def _residual_ln(x, y, gate_row, ln_g, ln_b):
    return _layer_norm(DEEPNORM_ALPHA * x + (1.0 + gate_row) * y, ln_g, ln_b)


def _route_store(h, rw_ref, rb_ref, idx_ref, gate_ref):
    logits = _narrow_dot(h, rw_ref) + rb_ref[...]
    lane = lax.broadcasted_iota(jnp.int32, logits.shape, 1).astype(F32)
    work = logits
    idx_out = jnp.zeros(logits.shape, F32)
    val_out = jnp.full(logits.shape, NEG_BIG, F32)
    for k in range(TOP_K):
        m = jnp.max(work, -1, keepdims=True)
        am = jnp.min(jnp.where(work == m, lane, float(LANES)), -1, keepdims=True)
        idx_out = jnp.where(lane == k, am, idx_out)
        val_out = jnp.where(lane == k, m, val_out)
        work = jnp.where(lane == am, NEG_BIG * 2.0, work)
    top = jnp.max(val_out, -1, keepdims=True)
    e = jnp.where(lane < TOP_K, jnp.exp(val_out - top), 0.0)
    idx_ref[...] = idx_out.astype(jnp.int32)
    gate_ref[...] = e / jnp.sum(e, -1, keepdims=True)


def _post_mixer_tail(x, y, mod_ref, lng_ref, lnb_ref, rw_ref, rb_ref, x1_ref, h3_ref, idx_ref, gate_ref):
    x1 = _residual_ln(x, y, mod_ref[0, 2:3, :], lng_ref[...], lnb_ref[...])
    x1_ref[...] = x1
    h2 = x1 * (1.0 + mod_ref[0, 4:5, :]) + mod_ref[0, 3:4, :]
    _rows_to_tiles(h3_ref, h2)
    _route_store(h2, rw_ref, rb_ref, idx_ref, gate_ref)


def _router_operands(router_w, router_b):
    rb = jnp.full((1, LANES), NEG_BIG, F32).at[0, :N_EXPERTS].set(router_b)
    return _split_weight(router_w), rb


def _post_out_specs(tm, d):
    return [
        pl.BlockSpec((tm, d), lambda i: (i, 0)),
        pl.BlockSpec((tm * ROW_TILE, LANES), lambda i: (i, 0)),
        pl.BlockSpec((tm, LANES), lambda i: (i, 0)),
        pl.BlockSpec((tm, LANES), lambda i: (i, 0)),
    ]


def _post_out_shapes(t, d):
    assert d == ROW_TILE * LANES
    return [
        jax.ShapeDtypeStruct((t, d), F32),
        jax.ShapeDtypeStruct((t * ROW_TILE, LANES), F32),
        jax.ShapeDtypeStruct((t, LANES), jnp.int32),
        jax.ShapeDtypeStruct((t, LANES), F32),
    ]


def _dn_out_kernel(o_ref, w_ref, x_ref, mod_ref, lng_ref, lnb_ref, rw_ref, rb_ref,
                   x1_ref, h3_ref, idx_ref, gate_ref):
    y = jnp.dot(o_ref[...], w_ref[...], preferred_element_type=F32)
    _post_mixer_tail(x_ref[...], y, mod_ref, lng_ref, lnb_ref, rw_ref, rb_ref,
                     x1_ref, h3_ref, idx_ref, gate_ref)


def _dn_out(o, out_w, x, mod, ln_g, ln_b, router_w, router_b):
    bsz, seq, d = x.shape
    t = bsz * seq
    vd = o.shape[2]
    tm = min(POST_TM, seq)
    rw, rb = _router_operands(router_w, router_b)
    per_b = seq // tm
    return pl.pallas_call(
        _dn_out_kernel,
        grid=(t // tm,),
        in_specs=[
            pl.BlockSpec((tm, vd), lambda i: (i, 0)),
            pl.BlockSpec((vd, d), lambda i: (0, 0)),
            pl.BlockSpec((tm, d), lambda i: (i, 0)),
            pl.BlockSpec((1, 6, d), lambda i: (i // per_b, 0, 0)),
            pl.BlockSpec((1, d), lambda i: (0, 0)),
            pl.BlockSpec((1, d), lambda i: (0, 0)),
            pl.BlockSpec((d, 2 * LANES), lambda i: (0, 0)),
            pl.BlockSpec((1, LANES), lambda i: (0, 0)),
        ],
        out_specs=_post_out_specs(tm, d),
        out_shape=_post_out_shapes(t, d),
        compiler_params=_cparams(("parallel",)),
        name="dn_out_ln_route",
    )(o.reshape(t, vd), out_w.astype(BF16), x.reshape(t, d), mod, ln_g.reshape(1, d), ln_b.reshape(1, d), rw, rb)


def _cf_pw1_kernel(x_ref, mod_ref, wa_ref, wb_ref, ba_ref, bb_ref, u_ref):
    h = (x_ref[0] * (1.0 + mod_ref[0, 1:2, :]) + mod_ref[0, 0:1, :]).astype(BF16)
    pa = jnp.dot(h, wa_ref[...], preferred_element_type=F32) + ba_ref[...]
    pb = jnp.dot(h, wb_ref[...], preferred_element_type=F32) + bb_ref[...]
    _rows_to_tiles(u_ref, pa * _sigmoid(pb))


def _cf_pw1(x, mod, pw1_w, pw1_b):
    bsz, seq, d = x.shape
    inner = pw1_w.shape[1] // 2
    assert inner == ROW_TILE * LANES
    tm = min(PROJ_TM, seq)
    per_b = seq // tm
    w = pw1_w.astype(BF16)
    b = pw1_b.reshape(1, 2 * inner)
    return pl.pallas_call(
        _cf_pw1_kernel,
        grid=(bsz, seq // tm),
        in_specs=[
            pl.BlockSpec((1, tm, d), lambda b_, i: (b_, i, 0)),
            pl.BlockSpec((1, 6, d), lambda b_, i: (b_, 0, 0)),
            pl.BlockSpec((d, inner), lambda b_, i: (0, 0)),
            pl.BlockSpec((d, inner), lambda b_, i: (0, 1)),
            pl.BlockSpec((1, inner), lambda b_, i: (0, 0)),
            pl.BlockSpec((1, inner), lambda b_, i: (0, 1)),
        ],
        out_specs=pl.BlockSpec((tm * ROW_TILE, LANES), lambda b_, i: (b_ * per_b + i, 0)),
        out_shape=jax.ShapeDtypeStruct((bsz * seq * ROW_TILE, LANES), F32),
        compiler_params=_cparams(("parallel", "parallel")),
        name="cf_pw1_glu",
    )(x, mod, w, w, b, b)


def _cf_tail_kernel(u_ref, uh_ref, dw_ref, dwb_ref, cg_ref, cb_ref, w2_ref, b2_ref,
                    x_ref, mod_ref, lng_ref, lnb_ref, rw_ref, rb_ref,
                    x1_ref, h3_ref, idx_ref, gate_ref, scr, conv_scr):
    tm = u_ref.shape[0] // ROW_TILE
    halo = uh_ref.shape[0] // ROW_TILE
    keep = (pl.program_id(1) > 0).astype(F32)
    scr[0:halo * ROW_TILE, :] = uh_ref[...] * keep
    scr[halo * ROW_TILE:(halo + tm) * ROW_TILE, :] = u_ref[...]
    first = halo - (CF_KERNEL - 1)
    blk = CF_CONV_ROWS * ROW_TILE

    def conv_tokens(tb, carry):
        r0 = pl.multiple_of(tb * blk, blk)
        acc = jnp.concatenate([dwb_ref[...]] * CF_CONV_ROWS, axis=0)
        for j in range(CF_KERNEL):
            w_tile = dw_ref[j * ROW_TILE:(j + 1) * ROW_TILE, :]
            w_blk = jnp.concatenate([w_tile] * CF_CONV_ROWS, axis=0)
            acc = acc + scr[pl.ds(r0 + (first + j) * ROW_TILE, blk), :] * w_blk
        conv_scr[pl.ds(r0, blk), :] = acc
        return carry

    lax.fori_loop(0, tm // CF_CONV_ROWS, conv_tokens, 0)
    conv = _tiles_to_rows(conv_scr)
    normed = _layer_norm(conv, cg_ref[...], cb_ref[...])
    act = normed * _sigmoid(normed)
    y = jnp.dot(act.astype(BF16), w2_ref[...], preferred_element_type=F32) + b2_ref[...]
    _post_mixer_tail(x_ref[0], y, mod_ref, lng_ref, lnb_ref, rw_ref, rb_ref,
                     x1_ref, h3_ref, idx_ref, gate_ref)


def _cf_tail(u, dw_w, dw_b, cf_ln_g, cf_ln_b, pw2_w, pw2_b, x, mod, ln_g, ln_b, router_w, router_b):
    bsz, seq, d = x.shape
    inner = dw_w.shape[1]
    t = bsz * seq
    tm = min(CF_TM, seq)
    per_b = seq // tm
    hb = tm // CF_HALO
    halos_per_b = seq // CF_HALO
    rw, rb = _router_operands(router_w, router_b)
    row = lambda v: v.reshape(1, -1)
    tiles = lambda v: v.reshape(-1, LANES)
    const = lambda b, i: (0, 0)
    flat = lambda b, i: (b * per_b + i, 0)
    return pl.pallas_call(
        _cf_tail_kernel,
        grid=(bsz, per_b),
        in_specs=[
            pl.BlockSpec((tm * ROW_TILE, LANES), flat),
            pl.BlockSpec((CF_HALO * ROW_TILE, LANES),
                         lambda b, i: (b * halos_per_b + jnp.maximum(i * hb - 1, 0), 0)),
            pl.BlockSpec((CF_KERNEL * ROW_TILE, LANES), const),
            pl.BlockSpec((ROW_TILE, LANES), const),
            pl.BlockSpec((1, inner), const),
            pl.BlockSpec((1, inner), const),
            pl.BlockSpec((inner, d), const),
            pl.BlockSpec((1, d), const),
            pl.BlockSpec((1, tm, d), lambda b, i: (b, i, 0)),
            pl.BlockSpec((1, 6, d), lambda b, i: (b, 0, 0)),
            pl.BlockSpec((1, d), const),
            pl.BlockSpec((1, d), const),
            pl.BlockSpec((d, 2 * LANES), const),
            pl.BlockSpec((1, LANES), const),
        ],
        out_specs=[
            pl.BlockSpec((tm, d), flat),
            pl.BlockSpec((tm * ROW_TILE, LANES), flat),
            pl.BlockSpec((tm, LANES), flat),
            pl.BlockSpec((tm, LANES), flat),
        ],
        out_shape=_post_out_shapes(t, d),
        scratch_shapes=[pltpu.VMEM(((tm + CF_HALO) * ROW_TILE, LANES), F32),
                        pltpu.VMEM((tm * ROW_TILE, LANES), F32)],
        compiler_params=_cparams(("parallel", "arbitrary")),
        name="cf_conv_ln_pw2_route",
    )(u, u, tiles(dw_w), tiles(dw_b), row(cf_ln_g), row(cf_ln_b), pw2_w.astype(BF16), row(pw2_b),
      x, mod, row(ln_g), row(ln_b), rw, rb)


def _rank_kernel(idx_ref, rank_ref, cnt_ref, carry):
    tt = idx_ref.shape[0]

    @pl.when(pl.program_id(0) == 0)
    def _():
        carry[...] = jnp.zeros(carry.shape, F32)

    idx = idx_ref[...]
    lane = lax.broadcasted_iota(jnp.int32, idx.shape, 1)
    sel = [lane == idx[:, k:k + 1] for k in range(TOP_K)]
    multi_hot = sel[0]
    for k in range(1, TOP_K):
        multi_hot = multi_hot | sel[k]
    mh = multi_hot.astype(BF16)
    r = lax.broadcasted_iota(jnp.int32, (tt, tt), 0)
    c = lax.broadcasted_iota(jnp.int32, (tt, tt), 1)
    before = jnp.dot((c < r).astype(BF16), mh, preferred_element_type=F32) + carry[...]
    rank = jnp.zeros(idx.shape, F32)
    for k in range(TOP_K):
        rk = jnp.sum(jnp.where(sel[k], before, 0.0), -1, keepdims=True)
        rank = jnp.where(lane == k, rk, rank)
    rank_ref[...] = rank.astype(jnp.int32)
    total = carry[...] + jnp.sum(mh.astype(F32), 0, keepdims=True)
    carry[...] = total
    cnt_ref[...] = total.astype(jnp.int32)


def _rank_and_count(idx):
    t = idx.shape[0]
    tt = min(RANK_TT, t)
    return pl.pallas_call(
        _rank_kernel,
        grid=(t // tt,),
        in_specs=[pl.BlockSpec((tt, LANES), lambda i: (i, 0))],
        out_specs=[
            pl.BlockSpec((tt, LANES), lambda i: (i, 0)),
            pl.BlockSpec((1, LANES), lambda i: (0, 0)),
        ],
        out_shape=[
            jax.ShapeDtypeStruct((t, LANES), jnp.int32),
            jax.ShapeDtypeStruct((1, LANES), jnp.int32),
        ],
        scratch_shapes=[pltpu.VMEM((1, LANES), F32)],
        compiler_params=_cparams(("arbitrary",)),
        name="moe_rank_count",
    )(idx)


def _num_expert_blocks(n_assign):
    bound = n_assign + N_EXPERTS * (EXPERT_TM - 1)
    return -(-bound // EXPERT_TM)


def _routing_tables(idx, rank, counts):
    t = idx.shape[0]
    nb = _num_expert_blocks(t * TOP_K)
    counts = counts[0, :N_EXPERTS]
    padded = (counts + EXPERT_TM - 1) // EXPERT_TM * EXPERT_TM
    pends = jnp.cumsum(padded)
    pstarts = pends - padded
    e = idx[:, :TOP_K]
    dest = (pstarts[e] + rank[:, :TOP_K]).astype(jnp.int32).reshape(t * TOP_K)
    block_row0 = jnp.arange(nb, dtype=jnp.int32) * EXPERT_TM
    segments_done = jnp.sum((pends[None, :] <= block_row0[:, None]).astype(jnp.int32), axis=1)
    block_e = jnp.minimum(segments_done, N_EXPERTS - 1)
    live = jnp.clip(counts[block_e] - (block_row0 - pstarts[block_e]), 0, EXPERT_TM).astype(jnp.int32)
    return dest, block_e, live


def _pad_fill_copies(live_ref, i, zeros, xb_hbm, sem):
    lv = live_ref[i]
    pad = EXPERT_TM - lv
    row0 = i * EXPERT_TM + lv
    copies = []
    piece = EXPERT_TM
    while piece >= 1:
        offset = pad & ~(2 * piece - 1)
        start = pl.multiple_of((row0 + offset) * ROW_TILE, ROW_TILE)
        copy = pltpu.make_async_copy(zeros.at[pl.ds(0, piece * ROW_TILE), :],
                                     xb_hbm.at[pl.ds(start, piece * ROW_TILE), :], sem)
        copies.append(((pad & piece) != 0, copy))
        piece //= 2
    return copies


def _dispatch_kernel(dest_ref, live_ref, h_ref, xb_hbm, zeros, sem, fill_sem):
    tt = h_ref.shape[0] // ROW_TILE
    base = pl.program_id(0) * tt
    n_blocks = xb_hbm.shape[0] // (EXPERT_TM * ROW_TILE)

    @pl.when(pl.program_id(0) == 0)
    def _():
        zeros[...] = jnp.zeros(zeros.shape, F32)

        def start_fill(i, carry):
            for live_bit, copy in _pad_fill_copies(live_ref, i, zeros, xb_hbm, fill_sem):
                pl.when(live_bit)(copy.start)
            return carry

        lax.fori_loop(0, n_blocks, start_fill, 0)

    def issue(t, carry):
        for k in range(TOP_K):
            pltpu.make_async_copy(_token_tile(h_ref, t),
                                  _token_tile(xb_hbm, dest_ref[(base + t) * TOP_K + k]), sem).start()
        return carry

    lax.fori_loop(0, tt, issue, 0)
    for _ in range(TOP_K):
        pltpu.make_async_copy(h_ref, xb_hbm.at[pl.ds(0, tt * ROW_TILE), :], sem).wait()

    @pl.when(pl.program_id(0) == 0)
    def _():
        def wait_fill(i, carry):
            for live_bit, copy in _pad_fill_copies(live_ref, i, zeros, xb_hbm, fill_sem):
                pl.when(live_bit)(copy.wait)
            return carry

        lax.fori_loop(0, n_blocks, wait_fill, 0)


def _dispatch(dest, live, h3, n_rows):
    t = h3.shape[0] // ROW_TILE
    return pl.pallas_call(
        _dispatch_kernel,
        grid_spec=pltpu.PrefetchScalarGridSpec(
            num_scalar_prefetch=2,
            grid=(t // DISPATCH_TT,),
            in_specs=[pl.BlockSpec((DISPATCH_TT * ROW_TILE, LANES), lambda i, dst, lv: (i, 0))],
            out_specs=pl.BlockSpec(memory_space=pl.ANY),
            scratch_shapes=[pltpu.VMEM((EXPERT_TM * ROW_TILE, LANES), F32), pltpu.SemaphoreType.DMA,
                            pltpu.SemaphoreType.DMA],
        ),
        out_shape=jax.ShapeDtypeStruct((n_rows * ROW_TILE, LANES), F32),
        compiler_params=_cparams(("arbitrary",)),
        name="moe_dispatch",
    )(dest, live, h3)


def _expert_kernel(be_ref, live_ref, x_ref, w1_ref, b1_ref, w2_ref, b2_ref, y_ref, w1s, w2s):
    i = pl.program_id(0)
    live = live_ref[i]
    e = be_ref[i]
    e_prev = be_ref[jnp.maximum(i - 1, 0)]

    @pl.when((i == 0) | (e != e_prev))
    def _():
        w1s[...] = w1_ref[0, 0].astype(BF16)
        w2s[...] = w2_ref[0, 0].astype(BF16)

    @pl.when(live > 0)
    def _():
        dff = w2s.shape[0]
        x = _tiles_to_rows(x_ref)
        row = lax.broadcasted_iota(jnp.int32, (x.shape[0], 1), 0)
        x = jnp.where(row < live, x, 0.0).astype(BF16)
        hh = jnp.dot(x, w1s[...], preferred_element_type=F32) + b1_ref[0, 0]
        x_glu = jnp.minimum(hh[:, :dff], SWIGLU_LIMIT)
        x_lin = jnp.clip(hh[:, dff:], -SWIGLU_LIMIT, SWIGLU_LIMIT)
        act = x_glu * _sigmoid(SWIGLU_ALPHA * x_glu) * (x_lin + 1.0)
        y = jnp.dot(act.astype(BF16), w2s[...], preferred_element_type=F32) + b2_ref[0, 0]
        _rows_to_tiles(y_ref, y)

    @pl.when(live == 0)
    def _():
        y_ref[...] = jnp.zeros(y_ref.shape, F32)


def _experts(block_e, live, xb, layer, w1, b1, w2, b2):
    depth, n_e, d, two_f = w1.shape
    dff = two_f // 2
    block_rows = EXPERT_TM * ROW_TILE
    nb = xb.shape[0] // block_rows
    return pl.pallas_call(
        _expert_kernel,
        grid_spec=pltpu.PrefetchScalarGridSpec(
            num_scalar_prefetch=2,
            grid=(nb,),
            in_specs=[
                pl.BlockSpec((block_rows, LANES), lambda i, be, lv: (i, 0)),
                pl.BlockSpec((1, 1, d, two_f), lambda i, be, lv: (layer, be[i], 0, 0)),
                pl.BlockSpec((1, 1, 1, two_f), lambda i, be, lv: (layer, be[i], 0, 0)),
                pl.BlockSpec((1, 1, dff, d), lambda i, be, lv: (layer, be[i], 0, 0)),
                pl.BlockSpec((1, 1, 1, d), lambda i, be, lv: (layer, be[i], 0, 0)),
            ],
            out_specs=pl.BlockSpec((block_rows, LANES), lambda i, be, lv: (i, 0)),
            scratch_shapes=[pltpu.VMEM((d, two_f), BF16), pltpu.VMEM((dff, d), BF16)],
        ),
        out_shape=jax.ShapeDtypeStruct(xb.shape, F32),
        compiler_params=_cparams(("arbitrary",)),
        name="moe_experts",
    )(block_e, live, xb, w1, b1.reshape(depth, n_e, 1, two_f), w2, b2.reshape(depth, n_e, 1, d))


def _combine_kernel(dest_ref, yb_hbm, gate_ref, x_ref, mod_ref, lng_ref, lnb_ref, o_ref, buf, sem):
    tt = x_ref.shape[0]
    step = pl.program_id(0)
    n_steps = pl.num_programs(0)

    def start_gather(tile, slot):
        def issue(t, carry):
            for k in range(TOP_K):
                src = _token_tile(yb_hbm, dest_ref[(tile * tt + t) * TOP_K + k])
                pltpu.make_async_copy(src, _token_tile(buf.at[slot, k], t), sem.at[slot]).start()
            return carry

        lax.fori_loop(0, tt, issue, 0)

    slot = step % 2

    @pl.when(step == 0)
    def _():
        start_gather(step, slot)

    @pl.when(step + 1 < n_steps)
    def _():
        start_gather(step + 1, 1 - slot)

    for k in range(TOP_K):
        pltpu.make_async_copy(yb_hbm.at[pl.ds(0, tt * ROW_TILE), :], buf.at[slot, k], sem.at[slot]).wait()

    gates = gate_ref[...]
    y = None
    for k in range(TOP_K):
        term = gates[:, k:k + 1] * _tiles_to_rows(buf.at[slot, k])
        y = term if y is None else y + term
    o_ref[...] = _residual_ln(x_ref[...], y, mod_ref[0, 5:6, :], lng_ref[...], lnb_ref[...])


def _combine(dest, yb, gates, x1, mod, ln_g, ln_b, seq):
    t, d = x1.shape
    tt = min(COMBINE_TT, seq)
    per_b = seq // tt
    return pl.pallas_call(
        _combine_kernel,
        grid_spec=pltpu.PrefetchScalarGridSpec(
            num_scalar_prefetch=1,
            grid=(t // tt,),
            in_specs=[
                pl.BlockSpec(memory_space=pl.ANY),
                pl.BlockSpec((tt, LANES), lambda i, dst: (i, 0)),
                pl.BlockSpec((tt, d), lambda i, dst: (i, 0)),
                pl.BlockSpec((1, 6, d), lambda i, dst: (i // per_b, 0, 0)),
                pl.BlockSpec((1, d), lambda i, dst: (0, 0)),
                pl.BlockSpec((1, d), lambda i, dst: (0, 0)),
            ],
            out_specs=pl.BlockSpec((tt, d), lambda i, dst: (i, 0)),
            scratch_shapes=[pltpu.VMEM((2, TOP_K, tt * ROW_TILE, LANES), F32), pltpu.SemaphoreType.DMA((2,))],
        ),
        out_shape=jax.ShapeDtypeStruct((t, d), F32),
        compiler_params=_cparams(("arbitrary",)),
        name="moe_combine_ln",
    )(dest, yb, gates, x1, mod, ln_g.reshape(1, d), ln_b.reshape(1, d))


def _moe_block(x1, h3, idx, gates, mod, layer, w1, b1, w2, b2, ln_g, ln_b, seq):
    t = x1.shape[0]
    rank, counts = _rank_and_count(idx)
    dest, block_e, live = _routing_tables(idx, rank, counts)
    n_rows = _num_expert_blocks(t * TOP_K) * EXPERT_TM
    xb = _dispatch(dest, live, h3, n_rows)
    yb = _experts(block_e, live, xb, layer, w1, b1, w2, b2)
    return _combine(dest, yb, gates, x1, mod, ln_g, ln_b, seq)


def kernel(x, c, ada_w, ada_b, dn_in_w, dn_conv_w, dn_A_log, dn_dt_bias, dn_onorm_w, dn_out_w,
           cf_pw1_w, cf_pw1_b, cf_dw_w, cf_dw_b, cf_ln_g, cf_ln_b, cf_pw2_w, cf_pw2_b,
           ln1_g, ln1_b, router_w, router_b, e_w1, e_b1, e_w2, e_b2, ln2_g, ln2_b):
    bsz, seq, d = x.shape
    mods = _ada_ln(c, ada_w, ada_b)

    proj, ba = _dn_in_proj(x, mods[0], dn_in_w[0])
    qn, kn, vv, gates_dn = _dn_prep(proj, ba, dn_conv_w[0], dn_A_log[0], dn_dt_bias[0])
    o = _dn_chunk(qn, kn, vv, proj, gates_dn, dn_onorm_w[0])
    x1, h3, idx, gates = _dn_out(o, dn_out_w[0], x, mods[0], ln1_g[0], ln1_b[0], router_w[0], router_b[0])
    x2 = _moe_block(x1, h3, idx, gates, mods[0], 0, e_w1, e_b1, e_w2, e_b2, ln2_g[0], ln2_b[0], seq)

    x2 = x2.reshape(bsz, seq, d)
    u = _cf_pw1(x2, mods[1], cf_pw1_w[0], cf_pw1_b[0])
    x3, h3, idx, gates = _cf_tail(u, cf_dw_w[0], cf_dw_b[0], cf_ln_g[0], cf_ln_b[0], cf_pw2_w[0], cf_pw2_b[0],
                                  x2, mods[1], ln1_g[1], ln1_b[1], router_w[1], router_b[1])
    x4 = _moe_block(x3, h3, idx, gates, mods[1], 1, e_w1, e_b1, e_w2, e_b2, ln2_g[1], ln2_b[1], seq)
    return x4.reshape(bsz, seq, d)
```

```python
import functools

import jax
import jax.numpy as jnp
from jax import lax
from jax.experimental import pallas as pl
from jax.experimental.pallas import tpu as pltpu

F32 = jnp.float32
BF16 = jnp.bfloat16
HIGHEST = lax.Precision.HIGHEST

DEPTH = 2
DN_QK_HEADS = 8
DN_V_HEADS = 16
DN_HEAD_DIM = 128
DN_CONV = 4
DN_CHUNK = 64
CF_KERNEL = 31
N_EXPERTS = 32
TOP_K = 4
SWIGLU_LIMIT = 7.0
SWIGLU_ALPHA = 1.702
LN_EPS = 1e-5
RMS_EPS = 1e-6
L2_EPS = 1e-6
DEEPNORM_ALPHA = (2 * DEPTH) ** 0.25

LANES = 128
SUBLANES = 8
BF16_SUBLANES = 16
VMEM_LIMIT = 56 * 1024 * 1024

ADA_TN = 1536
PROJ_TM = 1024
PROJ_TN = 1024
PREP_TS = 512
CHUNKS_PER_STEP = 4
DN_CHUNK_UNROLL = 2
POST_TM = 512
RANK_TT = 512
DISPATCH_TT = 512
EXPERT_TM = 512
COMBINE_TT = 256
CF_TM = 512
CF_HALO = 32
CF_CONV_ROWS = 32

NEG_BIG = -1e30


def _cparams(sem):
    return pltpu.CompilerParams(dimension_semantics=sem, vmem_limit_bytes=VMEM_LIMIT)


def _sigmoid(x):
    return jax.nn.sigmoid(x)


def _layer_norm(v, g, b):
    mu = jnp.mean(v, -1, keepdims=True)
    d = v - mu
    var = jnp.mean(d * d, -1, keepdims=True)
    return d * lax.rsqrt(var + LN_EPS) * g + b


ROW_TILE = SUBLANES


def _rows_to_tiles(o_ref, val):
    rows = val.shape[0]
    for s in range(ROW_TILE):
        o_ref[pl.ds(s, rows, stride=ROW_TILE), :] = val[:, s * LANES:(s + 1) * LANES]


def _tiles_to_rows(x_ref):
    rows = x_ref.shape[0] // ROW_TILE
    return jnp.concatenate([x_ref[pl.ds(s, rows, stride=ROW_TILE), :] for s in range(ROW_TILE)], axis=1)


def _split_weight(w):
    k, n = w.shape
    hi = w.astype(BF16)
    lo = (w - hi.astype(F32)).astype(BF16)
    out = jnp.zeros((k, 2 * LANES), BF16)
    return out.at[:, :n].set(hi).at[:, LANES:LANES + n].set(lo)


def _narrow_dot(h, w_split_ref):
    m = h.shape[0]
    hi = h.astype(BF16)
    lo = (h - hi.astype(F32)).astype(BF16)
    out = jnp.dot(jnp.concatenate([hi, lo], axis=0), w_split_ref[...], preferred_element_type=F32)
    return out[:m, :LANES] + out[:m, LANES:] + out[m:, :LANES]


def _token_tile(ref, tok):
    return ref.at[pl.ds(pl.multiple_of(tok * ROW_TILE, ROW_TILE), ROW_TILE), :]


def _ada_kernel(c_ref, w_ref, b_ref, o_ref):
    c = c_ref[...]
    cond = c * _sigmoid(c)
    o_ref[0] = jnp.dot(cond, w_ref[0], precision=HIGHEST, preferred_element_type=F32) + b_ref[0]


def _ada_ln(c, ada_w, ada_b):
    depth, d, n = ada_w.shape
    bsz = c.shape[0]
    c_pad = jnp.zeros((SUBLANES, d), F32).at[:bsz].set(c)
    out = pl.pallas_call(
        _ada_kernel,
        grid=(depth, n // ADA_TN),
        in_specs=[
            pl.BlockSpec((SUBLANES, d), lambda i, j: (0, 0)),
            pl.BlockSpec((1, d, ADA_TN), lambda i, j: (i, 0, j)),
            pl.BlockSpec((1, 1, ADA_TN), lambda i, j: (i, 0, j)),
        ],
        out_specs=pl.BlockSpec((1, SUBLANES, ADA_TN), lambda i, j: (i, 0, j)),
        out_shape=jax.ShapeDtypeStruct((depth, SUBLANES, n), F32),
        compiler_params=_cparams(("parallel", "parallel")),
        name="ada_ln",
    )(c_pad, ada_w, ada_b.reshape(depth, 1, n))
    return out[:, :bsz].reshape(depth, bsz, 6, d)


def _inproj_kernel(x_ref, mod_ref, w_ref, wba_ref, proj_ref, ba_ref, h_scr):
    @pl.when(pl.program_id(2) == 0)
    def _():
        h = x_ref[0] * (1.0 + mod_ref[0, 1:2, :]) + mod_ref[0, 0:1, :]
        h_scr[...] = h.astype(BF16)
        ba_ref[0] = _narrow_dot(h, wba_ref)

    proj_ref[0] = jnp.dot(h_scr[...], w_ref[...], preferred_element_type=F32).astype(BF16)


def _dn_in_proj(x, mod, in_w):
    bsz, seq, d = x.shape
    n_main = in_w.shape[1] - 2 * DN_V_HEADS
    w_main = in_w[:, :n_main].astype(BF16)
    w_ba = _split_weight(in_w[:, n_main:])
    tm = min(PROJ_TM, seq)
    return pl.pallas_call(
        _inproj_kernel,
        grid=(bsz, seq // tm, n_main // PROJ_TN),
        in_specs=[
            pl.BlockSpec((1, tm, d), lambda b, i, j: (b, i, 0)),
            pl.BlockSpec((1, 6, d), lambda b, i, j: (b, 0, 0)),
            pl.BlockSpec((d, PROJ_TN), lambda b, i, j: (0, j)),
            pl.BlockSpec((d, 2 * LANES), lambda b, i, j: (0, 0)),
        ],
        out_specs=[
            pl.BlockSpec((1, tm, PROJ_TN), lambda b, i, j: (b, i, j)),
            pl.BlockSpec((1, tm, LANES), lambda b, i, j: (b, i, 0)),
        ],
        out_shape=[
            jax.ShapeDtypeStruct((bsz, seq, n_main), BF16),
            jax.ShapeDtypeStruct((bsz, seq, LANES), F32),
        ],
        scratch_shapes=[pltpu.VMEM((tm, d), BF16)],
        compiler_params=_cparams(("parallel", "parallel", "arbitrary")),
        name="dn_in_proj",
    )(x, mod, w_main, w_ba)


def _dn_prep_kernel(q_ref, k_ref, v_ref, qh_ref, kh_ref, vh_ref, cw_ref, ba_ref, alog_ref, dt_ref,
                    qo_ref, ko_ref, vo_ref, g_ref, scr):
    ts = q_ref.shape[1]
    halo = qh_ref.shape[1]
    keep = (pl.program_id(1) > 0).astype(F32)

    def conv_silu(x_ref, h_ref, c0):
        width = x_ref.shape[2]
        scr[0:halo, 0:width] = h_ref[0].astype(F32) * keep
        scr[halo:halo + ts, 0:width] = x_ref[0].astype(F32)
        acc = None
        for j in range(DN_CONV):
            off = halo - (DN_CONV - 1) + j
            term = scr[off:off + ts, 0:width] * cw_ref[j:j + 1, c0:c0 + width]
            acc = term if acc is None else acc + term
        return acc * _sigmoid(acc)

    def l2norm_store(o_ref, x, scale):
        for h in range(x.shape[1] // DN_HEAD_DIM):
            xh = x[:, h * DN_HEAD_DIM:(h + 1) * DN_HEAD_DIM]
            ss = jnp.sum(xh * xh, -1, keepdims=True)
            o_ref[0, :, h * DN_HEAD_DIM:(h + 1) * DN_HEAD_DIM] = (
                xh * lax.rsqrt(ss + L2_EPS) * scale).astype(o_ref.dtype)

    kd = q_ref.shape[2]
    l2norm_store(qo_ref, conv_silu(q_ref, qh_ref, 0), DN_HEAD_DIM ** -0.5)
    l2norm_store(ko_ref, conv_silu(k_ref, kh_ref, kd), 1.0)
    vo_ref[0] = conv_silu(v_ref, vh_ref, 2 * kd).astype(vo_ref.dtype)

    ba = ba_ref[0]
    beta = _sigmoid(ba)
    zz = ba + dt_ref[...]
    softplus = jnp.maximum(zz, 0.0) + jnp.log1p(jnp.exp(-jnp.abs(zz)))
    g = -jnp.exp(alog_ref[...]) * softplus
    r = lax.broadcasted_iota(jnp.int32, (ts, ts), 0)
    c = lax.broadcasted_iota(jnp.int32, (ts, ts), 1)
    in_chunk_tril = ((r // DN_CHUNK == c // DN_CHUNK) & (c <= r)).astype(BF16)
    g1 = g.astype(BF16)
    g2 = (g - g1.astype(F32)).astype(BF16)
    g3 = (g - g1.astype(F32) - g2.astype(F32)).astype(BF16)
    parts = jnp.dot(in_chunk_tril, jnp.concatenate([g1, g2, g3], axis=1), preferred_element_type=F32)
    gcum = parts[:, :LANES] + parts[:, LANES:2 * LANES] + parts[:, 2 * LANES:]
    lane = lax.broadcasted_iota(jnp.int32, ba.shape, 1)
    g_ref[0] = jnp.where(lane < DN_V_HEADS, beta, gcum)


def _dn_prep(proj, ba, conv_w, a_log, dt_bias):
    bsz, seq, _ = proj.shape
    kd = DN_QK_HEADS * DN_HEAD_DIM
    vd = DN_V_HEADS * DN_HEAD_DIM
    ts = min(PREP_TS, seq)
    halo = BF16_SUBLANES
    hb = ts // halo
    alog_row = jnp.zeros((1, LANES), F32).at[0, DN_V_HEADS:2 * DN_V_HEADS].set(a_log)
    dt_row = jnp.zeros((1, LANES), F32).at[0, DN_V_HEADS:2 * DN_V_HEADS].set(dt_bias)

    def halo_map(col):
        return lambda b, i: (b, jnp.maximum(i * hb - 1, 0), col)

    return pl.pallas_call(
        _dn_prep_kernel,
        grid=(bsz, seq // ts),
        in_specs=[
            pl.BlockSpec((1, ts, kd), lambda b, i: (b, i, 0)),
            pl.BlockSpec((1, ts, kd), lambda b, i: (b, i, 1)),
            pl.BlockSpec((1, ts, vd), lambda b, i: (b, i, 1)),
            pl.BlockSpec((1, halo, kd), halo_map(0)),
            pl.BlockSpec((1, halo, kd), halo_map(1)),
            pl.BlockSpec((1, halo, vd), halo_map(1)),
            pl.BlockSpec((DN_CONV, 2 * kd + vd), lambda b, i: (0, 0)),
            pl.BlockSpec((1, ts, LANES), lambda b, i: (b, i, 0)),
            pl.BlockSpec((1, LANES), lambda b, i: (0, 0)),
            pl.BlockSpec((1, LANES), lambda b, i: (0, 0)),
        ],
        out_specs=[
            pl.BlockSpec((1, ts, kd), lambda b, i: (b, i, 0)),
            pl.BlockSpec((1, ts, kd), lambda b, i: (b, i, 0)),
            pl.BlockSpec((1, ts, vd), lambda b, i: (b, i, 0)),
            pl.BlockSpec((1, ts, LANES), lambda b, i: (b, i, 0)),
        ],
        out_shape=[
            jax.ShapeDtypeStruct((bsz, seq, kd), BF16),
            jax.ShapeDtypeStruct((bsz, seq, kd), BF16),
            jax.ShapeDtypeStruct((bsz, seq, vd), BF16),
            jax.ShapeDtypeStruct((bsz, seq, LANES), F32),
        ],
        scratch_shapes=[pltpu.VMEM((ts + halo, vd), F32)],
        compiler_params=_cparams(("parallel", "arbitrary")),
        name="dn_prep",
    )(proj, proj, proj, proj, proj, proj, conv_w, ba, alog_row, dt_row)


def _bmm(a, b):
    return lax.dot_general(a.astype(BF16), b.astype(BF16), (((2,), (1,)), ((0,), (0,))),
                           preferred_element_type=F32)


def _bmm_nt(a, b):
    return lax.dot_general(a.astype(BF16), b.astype(BF16), (((2,), (2,)), ((0,), (0,))),
                           preferred_element_type=F32)


def _bmm_tn(a, b):
    return lax.dot_general(a.astype(BF16), b.astype(BF16), (((1,), (1,)), ((0,), (0,))),
                           preferred_element_type=F32)


def _unit_lower_inverse_wide(a_twice, upper, eye_upper):
    c = a_twice.shape[1]
    x = jnp.where(upper, eye_upper, -a_twice)
    span = 1
    while span < c:
        x = _bmm(x[:, :, :c], x) + jnp.where(upper, x, 0.0)
        span *= 2
    return x


def _dn_chunk_kernel(q_ref, k_ref, v_ref, z_ref, g_ref, gt_ref, ow_ref, o_ref, s_ref):
    c_len = DN_CHUNK
    dh = DN_HEAD_DIM
    rep = DN_V_HEADS // DN_QK_HEADS

    @pl.when(pl.program_id(1) == 0)
    def _():
        s_ref[...] = jnp.zeros(s_ref.shape, F32)

    heads = range(DN_V_HEADS)
    n_heads = DN_V_HEADS
    ri = lax.broadcasted_iota(jnp.int32, (1, c_len, 2 * c_len), 1)
    lane = lax.broadcasted_iota(jnp.int32, (1, c_len, 2 * c_len), 2)
    upper = lane >= c_len
    ci = jnp.where(upper, lane - c_len, lane)
    causal = ci <= ri
    strict = ci < ri
    eye_upper = ((ci == ri) & upper).astype(F32)
    onorm = ow_ref[...]

    def head_cols(ref, rows, h):
        return ref[0, rows, h * dh:(h + 1) * dh]

    def per_v_head(t):
        return jnp.stack([t[h // rep] for h in heads])

    def state_free_part(c):
        r0 = pl.multiple_of(c * c_len, c_len)
        rows = pl.ds(r0, c_len)
        gcols = g_ref[0, rows, :]
        grows = gt_ref[0, c]
        wide = (n_heads, c_len, dh)
        beta_b = jnp.broadcast_to(jnp.stack([gcols[:, h:h + 1] for h in heads]), wide)
        g_b = jnp.broadcast_to(
            jnp.stack([gcols[:, DN_V_HEADS + h:DN_V_HEADS + h + 1] for h in heads]), wide)
        g_r = jnp.stack([grows[DN_V_HEADS + h:DN_V_HEADS + h + 1, :] for h in heads])
        g_r = jnp.concatenate([g_r, g_r], axis=2)
        g_last = g_b[:, c_len - 1:c_len, :]

        qn = jnp.stack([head_cols(q_ref, rows, hq) for hq in range(DN_QK_HEADS)])
        kn = jnp.stack([head_cols(k_ref, rows, hq) for hq in range(DN_QK_HEADS)])
        qk_kk = _bmm_nt(jnp.concatenate([qn, kn], axis=1), jnp.concatenate([kn, kn], axis=1))
        qk = per_v_head(qk_kk[:, :c_len])
        kk = per_v_head(qk_kk[:, c_len:])
        qf = per_v_head(qn).astype(F32)
        kf = per_v_head(kn).astype(F32)
        vf = jnp.stack([head_cols(v_ref, rows, h) for h in heads]).astype(F32)

        decay = jnp.where(causal, jnp.exp(g_b - g_r), 0.0)
        a_twice = jnp.where(strict, beta_b * kk * decay, 0.0)
        x_inv = _unit_lower_inverse_wide(a_twice, upper, eye_upper)
        eg = jnp.exp(g_b)
        rhs = jnp.concatenate([vf * beta_b, kf * (beta_b * eg)], axis=2)
        sol = _bmm(x_inv, jnp.concatenate([jnp.zeros_like(rhs), rhs], axis=1))
        u = sol[:, :, :dh]
        w = sol[:, :, dh:]
        qkm = jnp.where(causal, qk * decay, 0.0)[:, :, :c_len]
        qg = qf * eg
        kdec = kf * jnp.exp(g_last - g_b)
        return rows, jnp.concatenate([w, qg], axis=1), u, qkm, kdec, jnp.exp(g_last)

    def state_part(rows, w_qg, u, qkm, kdec, chunk_decay):
        state = s_ref[...]
        ws = _bmm(w_qg, state)
        v_new = u - ws[:, :c_len]
        o = ws[:, c_len:] + _bmm(qkm, v_new)
        s_ref[...] = state * chunk_decay + _bmm_tn(kdec, v_new)

        o = o * lax.rsqrt(jnp.mean(o * o, -1, keepdims=True) + RMS_EPS) * onorm
        for h in heads:
            zf = head_cols(z_ref, rows, h).astype(F32)
            o_ref[0, rows, h * dh:(h + 1) * dh] = (o[h] * (zf * _sigmoid(zf))).astype(o_ref.dtype)

    def chunk_group(i, carry):
        prepared = [state_free_part(i * DN_CHUNK_UNROLL + j) for j in range(DN_CHUNK_UNROLL)]
        for args in prepared:
            state_part(*args)
        return carry

    lax.fori_loop(0, q_ref.shape[1] // (c_len * DN_CHUNK_UNROLL), chunk_group, 0)


def _dn_chunk(qn, kn, vv, proj, gates, onorm_w):
    bsz, seq, kd = qn.shape
    vd = vv.shape[2]
    n_chunks = seq // DN_CHUNK
    cb = min(CHUNKS_PER_STEP, n_chunks)
    rows = cb * DN_CHUNK
    gates_t = jnp.swapaxes(gates[:, :, :2 * DN_V_HEADS].reshape(bsz, n_chunks, DN_CHUNK, 2 * DN_V_HEADS), 2, 3)
    z_col = (2 * kd + vd) // vd
    return pl.pallas_call(
        _dn_chunk_kernel,
        grid=(bsz, n_chunks // cb),
        in_specs=[
            pl.BlockSpec((1, rows, kd), lambda b, n: (b, n, 0)),
            pl.BlockSpec((1, rows, kd), lambda b, n: (b, n, 0)),
            pl.BlockSpec((1, rows, vd), lambda b, n: (b, n, 0)),
            pl.BlockSpec((1, rows, vd), lambda b, n: (b, n, z_col)),
            pl.BlockSpec((1, rows, LANES), lambda b, n: (b, n, 0)),
            pl.BlockSpec((1, cb, 2 * DN_V_HEADS, DN_CHUNK), lambda b, n: (b, n, 0, 0)),
            pl.BlockSpec((1, DN_HEAD_DIM), lambda b, n: (0, 0)),
        ],
        out_specs=pl.BlockSpec((1, rows, vd), lambda b, n: (b, n, 0)),
        out_shape=jax.ShapeDtypeStruct((bsz, seq, vd), BF16),
        scratch_shapes=[pltpu.VMEM((DN_V_HEADS, DN_HEAD_DIM, DN_HEAD_DIM), F32)],
        compiler_params=_cparams(("parallel", "arbitrary")),
        name="dn_chunk",
    )(qn, kn, vv, proj, gates, gates_t, onorm_w.reshape(1, DN_HEAD_DIM))


def _residual_ln(x, y, gate_row, ln_g, ln_b):
    return _layer_norm(DEEPNORM_ALPHA * x + (1.0 + gate_row) * y, ln_g, ln_b)


def _route_store(h, rw_ref, rb_ref, idx_ref, gate_ref):
    logits = _narrow_dot(h, rw_ref) + rb_ref[...]
    lane = lax.broadcasted_iota(jnp.int32, logits.shape, 1).astype(F32)
    work = logits
    idx_out = jnp.zeros(logits.shape, F32)
    val_out = jnp.full(logits.shape, NEG_BIG, F32)
    for k in range(TOP_K):
        m = jnp.max(work, -1, keepdims=True)
        am = jnp.min(jnp.where(work == m, lane, float(LANES)), -1, keepdims=True)
        idx_out = jnp.where(lane == k, am, idx_out)
        val_out = jnp.where(lane == k, m, val_out)
        work = jnp.where(lane == am, NEG_BIG * 2.0, work)
    top = jnp.max(val_out, -1, keepdims=True)
    e = jnp.where(lane < TOP_K, jnp.exp(val_out - top), 0.0)
    idx_ref[...] = idx_out.astype(jnp.int32)
    gate_ref[...] = e / jnp.sum(e, -1, keepdims=True)


def _post_mixer_tail(x, y, mod_ref, lng_ref, lnb_ref, rw_ref, rb_ref, x1_ref, h3_ref, idx_ref, gate_ref):
    x1 = _residual_ln(x, y, mod_ref[0, 2:3, :], lng_ref[...], lnb_ref[...])
    x1_ref[...] = x1
    h2 = x1 * (1.0 + mod_ref[0, 4:5, :]) + mod_ref[0, 3:4, :]
    _rows_to_tiles(h3_ref, h2)
    _route_store(h2, rw_ref, rb_ref, idx_ref, gate_ref)


def _router_operands(router_w, router_b):
    rb = jnp.full((1, LANES), NEG_BIG, F32).at[0, :N_EXPERTS].set(router_b)
    return _split_weight(router_w), rb


def _post_out_specs(tm, d):
    return [
        pl.BlockSpec((tm, d), lambda i: (i, 0)),
        pl.BlockSpec((tm * ROW_TILE, LANES), lambda i: (i, 0)),
        pl.BlockSpec((tm, LANES), lambda i: (i, 0)),
        pl.BlockSpec((tm, LANES), lambda i: (i, 0)),
    ]


def _post_out_shapes(t, d):
    assert d == ROW_TILE * LANES
    return [
        jax.ShapeDtypeStruct((t, d), F32),
        jax.ShapeDtypeStruct((t * ROW_TILE, LANES), F32),
        jax.ShapeDtypeStruct((t, LANES), jnp.int32),
        jax.ShapeDtypeStruct((t, LANES), F32),
    ]


def _dn_out_kernel(o_ref, w_ref, x_ref, mod_ref, lng_ref, lnb_ref, rw_ref, rb_ref,
                   x1_ref, h3_ref, idx_ref, gate_ref):
    y = jnp.dot(o_ref[...], w_ref[...], preferred_element_type=F32)
    _post_mixer_tail(x_ref[...], y, mod_ref, lng_ref, lnb_ref, rw_ref, rb_ref,
                     x1_ref, h3_ref, idx_ref, gate_ref)


def _dn_out(o, out_w, x, mod, ln_g, ln_b, router_w, router_b):
    bsz, seq, d = x.shape
    t = bsz * seq
    vd = o.shape[2]
    tm = min(POST_TM, seq)
    rw, rb = _router_operands(router_w, router_b)
    per_b = seq // tm
    return pl.pallas_call(
        _dn_out_kernel,
        grid=(t // tm,),
        in_specs=[
            pl.BlockSpec((tm, vd), lambda i: (i, 0)),
            pl.BlockSpec((vd, d), lambda i: (0, 0)),
            pl.BlockSpec((tm, d), lambda i: (i, 0)),
            pl.BlockSpec((1, 6, d), lambda i: (i // per_b, 0, 0)),
            pl.BlockSpec((1, d), lambda i: (0, 0)),
            pl.BlockSpec((1, d), lambda i: (0, 0)),
            pl.BlockSpec((d, 2 * LANES), lambda i: (0, 0)),
            pl.BlockSpec((1, LANES), lambda i: (0, 0)),
        ],
        out_specs=_post_out_specs(tm, d),
        out_shape=_post_out_shapes(t, d),
        compiler_params=_cparams(("parallel",)),
        name="dn_out_ln_route",
    )(o.reshape(t, vd), out_w.astype(BF16), x.reshape(t, d), mod, ln_g.reshape(1, d), ln_b.reshape(1, d), rw, rb)


def _cf_pw1_kernel(x_ref, mod_ref, wa_ref, wb_ref, ba_ref, bb_ref, u_ref):
    h = (x_ref[0] * (1.0 + mod_ref[0, 1:2, :]) + mod_ref[0, 0:1, :]).astype(BF16)
    pa = jnp.dot(h, wa_ref[...], preferred_element_type=F32) + ba_ref[...]
    pb = jnp.dot(h, wb_ref[...], preferred_element_type=F32) + bb_ref[...]
    _rows_to_tiles(u_ref, pa * _sigmoid(pb))


def _cf_pw1(x, mod, pw1_w, pw1_b):
    bsz, seq, d = x.shape
    inner = pw1_w.shape[1] // 2
    assert inner == ROW_TILE * LANES
    tm = min(PROJ_TM, seq)
    per_b = seq // tm
    w = pw1_w.astype(BF16)
    b = pw1_b.reshape(1, 2 * inner)
    return pl.pallas_call(
        _cf_pw1_kernel,
        grid=(bsz, seq // tm),
        in_specs=[
            pl.BlockSpec((1, tm, d), lambda b_, i: (b_, i, 0)),
            pl.BlockSpec((1, 6, d), lambda b_, i: (b_, 0, 0)),
            pl.BlockSpec((d, inner), lambda b_, i: (0, 0)),
            pl.BlockSpec((d, inner), lambda b_, i: (0, 1)),
            pl.BlockSpec((1, inner), lambda b_, i: (0, 0)),
            pl.BlockSpec((1, inner), lambda b_, i: (0, 1)),
        ],
        out_specs=pl.BlockSpec((tm * ROW_TILE, LANES), lambda b_, i: (b_ * per_b + i, 0)),
        out_shape=jax.ShapeDtypeStruct((bsz * seq * ROW_TILE, LANES), F32),
        compiler_params=_cparams(("parallel", "parallel")),
        name="cf_pw1_glu",
    )(x, mod, w, w, b, b)


def _cf_tail_kernel(u_ref, uh_ref, dw_ref, dwb_ref, cg_ref, cb_ref, w2_ref, b2_ref,
                    x_ref, mod_ref, lng_ref, lnb_ref, rw_ref, rb_ref,
                    x1_ref, h3_ref, idx_ref, gate_ref, scr, conv_scr):
    tm = u_ref.shape[0] // ROW_TILE
    halo = uh_ref.shape[0] // ROW_TILE
    keep = (pl.program_id(1) > 0).astype(F32)
    scr[0:halo * ROW_TILE, :] = uh_ref[...] * keep
    scr[halo * ROW_TILE:(halo + tm) * ROW_TILE, :] = u_ref[...]
    first = halo - (CF_KERNEL - 1)
    blk = CF_CONV_ROWS * ROW_TILE

    def conv_tokens(tb, carry):
        r0 = pl.multiple_of(tb * blk, blk)
        acc = jnp.concatenate([dwb_ref[...]] * CF_CONV_ROWS, axis=0)
        for j in range(CF_KERNEL):
            w_tile = dw_ref[j * ROW_TILE:(j + 1) * ROW_TILE, :]
            w_blk = jnp.concatenate([w_tile] * CF_CONV_ROWS, axis=0)
            acc = acc + scr[pl.ds(r0 + (first + j) * ROW_TILE, blk), :] * w_blk
        conv_scr[pl.ds(r0, blk), :] = acc
        return carry

    lax.fori_loop(0, tm // CF_CONV_ROWS, conv_tokens, 0)
    conv = _tiles_to_rows(conv_scr)
    normed = _layer_norm(conv, cg_ref[...], cb_ref[...])
    act = normed * _sigmoid(normed)
    y = jnp.dot(act.astype(BF16), w2_ref[...], preferred_element_type=F32) + b2_ref[...]
    _post_mixer_tail(x_ref[0], y, mod_ref, lng_ref, lnb_ref, rw_ref, rb_ref,
                     x1_ref, h3_ref, idx_ref, gate_ref)


def _cf_tail(u, dw_w, dw_b, cf_ln_g, cf_ln_b, pw2_w, pw2_b, x, mod, ln_g, ln_b, router_w, router_b):
    bsz, seq, d = x.shape
    inner = dw_w.shape[1]
    t = bsz * seq
    tm = min(CF_TM, seq)
    per_b = seq // tm
    hb = tm // CF_HALO
    halos_per_b = seq // CF_HALO
    rw, rb = _router_operands(router_w, router_b)
    row = lambda v: v.reshape(1, -1)
    tiles = lambda v: v.reshape(-1, LANES)
    const = lambda b, i: (0, 0)
    flat = lambda b, i: (b * per_b + i, 0)
    return pl.pallas_call(
        _cf_tail_kernel,
        grid=(bsz, per_b),
        in_specs=[
            pl.BlockSpec((tm * ROW_TILE, LANES), flat),
            pl.BlockSpec((CF_HALO * ROW_TILE, LANES),
                         lambda b, i: (b * halos_per_b + jnp.maximum(i * hb - 1, 0), 0)),
            pl.BlockSpec((CF_KERNEL * ROW_TILE, LANES), const),
            pl.BlockSpec((ROW_TILE, LANES), const),
            pl.BlockSpec((1, inner), const),
            pl.BlockSpec((1, inner), const),
            pl.BlockSpec((inner, d), const),
            pl.BlockSpec((1, d), const),
            pl.BlockSpec((1, tm, d), lambda b, i: (b, i, 0)),
            pl.BlockSpec((1, 6, d), lambda b, i: (b, 0, 0)),
            pl.BlockSpec((1, d), const),
            pl.BlockSpec((1, d), const),
            pl.BlockSpec((d, 2 * LANES), const),
            pl.BlockSpec((1, LANES), const),
        ],
        out_specs=[
            pl.BlockSpec((tm, d), flat),
            pl.BlockSpec((tm * ROW_TILE, LANES), flat),
            pl.BlockSpec((tm, LANES), flat),
            pl.BlockSpec((tm, LANES), flat),
        ],
        out_shape=_post_out_shapes(t, d),
        scratch_shapes=[pltpu.VMEM(((tm + CF_HALO) * ROW_TILE, LANES), F32),
                        pltpu.VMEM((tm * ROW_TILE, LANES), F32)],
        compiler_params=_cparams(("parallel", "arbitrary")),
        name="cf_conv_ln_pw2_route",
    )(u, u, tiles(dw_w), tiles(dw_b), row(cf_ln_g), row(cf_ln_b), pw2_w.astype(BF16), row(pw2_b),
      x, mod, row(ln_g), row(ln_b), rw, rb)


def _rank_kernel(idx_ref, rank_ref, cnt_ref, carry):
    tt = idx_ref.shape[0]

    @pl.when(pl.program_id(0) == 0)
    def _():
        carry[...] = jnp.zeros(carry.shape, F32)

    idx = idx_ref[...]
    lane = lax.broadcasted_iota(jnp.int32, idx.shape, 1)
    sel = [lane == idx[:, k:k + 1] for k in range(TOP_K)]
    multi_hot = sel[0]
    for k in range(1, TOP_K):
        multi_hot = multi_hot | sel[k]
    mh = multi_hot.astype(BF16)
    r = lax.broadcasted_iota(jnp.int32, (tt, tt), 0)
    c = lax.broadcasted_iota(jnp.int32, (tt, tt), 1)
    before = jnp.dot((c < r).astype(BF16), mh, preferred_element_type=F32) + carry[...]
    rank = jnp.zeros(idx.shape, F32)
    for k in range(TOP_K):
        rk = jnp.sum(jnp.where(sel[k], before, 0.0), -1, keepdims=True)
        rank = jnp.where(lane == k, rk, rank)
    rank_ref[...] = rank.astype(jnp.int32)
    total = carry[...] + jnp.sum(mh.astype(F32), 0, keepdims=True)
    carry[...] = total
    cnt_ref[...] = total.astype(jnp.int32)


def _rank_and_count(idx):
    t = idx.shape[0]
    tt = min(RANK_TT, t)
    return pl.pallas_call(
        _rank_kernel,
        grid=(t // tt,),
        in_specs=[pl.BlockSpec((tt, LANES), lambda i: (i, 0))],
        out_specs=[
            pl.BlockSpec((tt, LANES), lambda i: (i, 0)),
            pl.BlockSpec((1, LANES), lambda i: (0, 0)),
        ],
        out_shape=[
            jax.ShapeDtypeStruct((t, LANES), jnp.int32),
            jax.ShapeDtypeStruct((1, LANES), jnp.int32),
        ],
        scratch_shapes=[pltpu.VMEM((1, LANES), F32)],
        compiler_params=_cparams(("arbitrary",)),
        name="moe_rank_count",
    )(idx)


def _num_expert_blocks(n_assign):
    bound = n_assign + N_EXPERTS * (EXPERT_TM - 1)
    return -(-bound // EXPERT_TM)


def _routing_tables(idx, rank, counts):
    t = idx.shape[0]
    nb = _num_expert_blocks(t * TOP_K)
    counts = counts[0, :N_EXPERTS]
    padded = (counts + EXPERT_TM - 1) // EXPERT_TM * EXPERT_TM
    pends = jnp.cumsum(padded)
    pstarts = pends - padded
    e = idx[:, :TOP_K]
    dest = (pstarts[e] + rank[:, :TOP_K]).astype(jnp.int32).reshape(t * TOP_K)
    block_row0 = jnp.arange(nb, dtype=jnp.int32) * EXPERT_TM
    segments_done = jnp.sum((pends[None, :] <= block_row0[:, None]).astype(jnp.int32), axis=1)
    block_e = jnp.minimum(segments_done, N_EXPERTS - 1)
    live = jnp.clip(counts[block_e] - (block_row0 - pstarts[block_e]), 0, EXPERT_TM).astype(jnp.int32)
    return dest, block_e, live


def _expert_run_tables(block_e):
    nb = block_e.shape[0]
    pos = jnp.arange(nb, dtype=jnp.int32)
    first = jnp.concatenate([jnp.ones((1,), jnp.int32), (block_e[1:] != block_e[:-1]).astype(jnp.int32)])
    slot = (jnp.cumsum(first) - 1) % 2
    start_at_or_after = lax.cummin(jnp.where(first == 1, pos, nb), reverse=True)
    next_start = jnp.concatenate([start_at_or_after[1:], jnp.full((1,), nb, jnp.int32)])
    next_e = jnp.where(next_start < nb, block_e[jnp.minimum(next_start, nb - 1)], -1)
    return first, slot.astype(jnp.int32), next_e.astype(jnp.int32)


def _pad_fill_copies(live_ref, i, zeros, xb_hbm, sem):
    lv = live_ref[i]
    pad = EXPERT_TM - lv
    row0 = i * EXPERT_TM + lv
    copies = []
    piece = EXPERT_TM
    while piece >= 1:
        offset = pad & ~(2 * piece - 1)
        start = pl.multiple_of((row0 + offset) * ROW_TILE, ROW_TILE)
        copy = pltpu.make_async_copy(zeros.at[pl.ds(0, piece * ROW_TILE), :],
                                     xb_hbm.at[pl.ds(start, piece * ROW_TILE), :], sem)
        copies.append(((pad & piece) != 0, copy))
        piece //= 2
    return copies


def _dispatch_kernel(dest_ref, live_ref, h_ref, xb_hbm, zeros, sem, fill_sem):
    tt = h_ref.shape[0] // ROW_TILE
    base = pl.program_id(0) * tt
    n_blocks = xb_hbm.shape[0] // (EXPERT_TM * ROW_TILE)

    @pl.when(pl.program_id(0) == 0)
    def _():
        zeros[...] = jnp.zeros(zeros.shape, F32)

        def start_fill(i, carry):
            for live_bit, copy in _pad_fill_copies(live_ref, i, zeros, xb_hbm, fill_sem):
                pl.when(live_bit)(copy.start)
            return carry

        lax.fori_loop(0, n_blocks, start_fill, 0)

    def issue(t, carry):
        for k in range(TOP_K):
            pltpu.make_async_copy(_token_tile(h_ref, t),
                                  _token_tile(xb_hbm, dest_ref[(base + t) * TOP_K + k]), sem).start()
        return carry

    lax.fori_loop(0, tt, issue, 0)
    for _ in range(TOP_K):
        pltpu.make_async_copy(h_ref, xb_hbm.at[pl.ds(0, tt * ROW_TILE), :], sem).wait()

    @pl.when(pl.program_id(0) == 0)
    def _():
        def wait_fill(i, carry):
            for live_bit, copy in _pad_fill_copies(live_ref, i, zeros, xb_hbm, fill_sem):
                pl.when(live_bit)(copy.wait)
            return carry

        lax.fori_loop(0, n_blocks, wait_fill, 0)


def _dispatch(dest, live, h3, n_rows):
    t = h3.shape[0] // ROW_TILE
    return pl.pallas_call(
        _dispatch_kernel,
        grid_spec=pltpu.PrefetchScalarGridSpec(
            num_scalar_prefetch=2,
            grid=(t // DISPATCH_TT,),
            in_specs=[pl.BlockSpec((DISPATCH_TT * ROW_TILE, LANES), lambda i, dst, lv: (i, 0))],
            out_specs=pl.BlockSpec(memory_space=pl.ANY),
            scratch_shapes=[pltpu.VMEM((EXPERT_TM * ROW_TILE, LANES), F32), pltpu.SemaphoreType.DMA,
                            pltpu.SemaphoreType.DMA],
        ),
        out_shape=jax.ShapeDtypeStruct((n_rows * ROW_TILE, LANES), F32),
        compiler_params=_cparams(("arbitrary",)),
        name="moe_dispatch",
    )(dest, live, h3)


def _expert_kernel(be_ref, live_ref, first_ref, slot_ref, next_ref, x_ref, w1_hbm, b1_ref, w2_hbm, b2_ref,
                   y_ref, w1f, w2f, w1s, w2s, sem, *, layer):
    i = pl.program_id(0)
    live = live_ref[i]

    def weight_copies(e, slot):
        return (pltpu.make_async_copy(w1_hbm.at[layer, e], w1f.at[slot], sem.at[0, slot]),
                pltpu.make_async_copy(w2_hbm.at[layer, e], w2f.at[slot], sem.at[1, slot]))

    @pl.when(i == 0)
    def _():
        for copy in weight_copies(be_ref[0], 0):
            copy.start()

    @pl.when(first_ref[i] == 1)
    def _():
        slot = slot_ref[i]
        for copy in weight_copies(be_ref[i], slot):
            copy.wait()
        nxt = next_ref[i]

        @pl.when(nxt >= 0)
        def _():
            for copy in weight_copies(nxt, 1 - slot):
                copy.start()

        w1s[...] = w1f[slot].astype(BF16)
        w2s[...] = w2f[slot].astype(BF16)

    @pl.when(live > 0)
    def _():
        dff = w2s.shape[0]
        x = _tiles_to_rows(x_ref)
        row = lax.broadcasted_iota(jnp.int32, (x.shape[0], 1), 0)
        x = jnp.where(row < live, x, 0.0).astype(BF16)
        hh = jnp.dot(x, w1s[...], preferred_element_type=F32) + b1_ref[0, 0]
        x_glu = jnp.minimum(hh[:, :dff], SWIGLU_LIMIT)
        x_lin = jnp.clip(hh[:, dff:], -SWIGLU_LIMIT, SWIGLU_LIMIT)
        act = x_glu * _sigmoid(SWIGLU_ALPHA * x_glu) * (x_lin + 1.0)
        y = jnp.dot(act.astype(BF16), w2s[...], preferred_element_type=F32) + b2_ref[0, 0]
        _rows_to_tiles(y_ref, y)

    @pl.when(live == 0)
    def _():
        y_ref[...] = jnp.zeros(y_ref.shape, F32)


def _experts(block_e, live, xb, layer, w1, b1, w2, b2):
    depth, n_e, d, two_f = w1.shape
    dff = two_f // 2
    block_rows = EXPERT_TM * ROW_TILE
    nb = xb.shape[0] // block_rows
    first, slot, next_e = _expert_run_tables(block_e)
    bias_map = lambda i, be, *_: (layer, be[i], 0, 0)
    return pl.pallas_call(
        functools.partial(_expert_kernel, layer=layer),
        grid_spec=pltpu.PrefetchScalarGridSpec(
            num_scalar_prefetch=5,
            grid=(nb,),
            in_specs=[
                pl.BlockSpec((block_rows, LANES), lambda i, *_: (i, 0)),
                pl.BlockSpec(memory_space=pl.ANY),
                pl.BlockSpec((1, 1, 1, two_f), bias_map),
                pl.BlockSpec(memory_space=pl.ANY),
                pl.BlockSpec((1, 1, 1, d), bias_map),
            ],
            out_specs=pl.BlockSpec((block_rows, LANES), lambda i, *_: (i, 0)),
            scratch_shapes=[pltpu.VMEM((2, d, two_f), F32), pltpu.VMEM((2, dff, d), F32),
                            pltpu.VMEM((d, two_f), BF16), pltpu.VMEM((dff, d), BF16),
                            pltpu.SemaphoreType.DMA((2, 2))],
        ),
        out_shape=jax.ShapeDtypeStruct(xb.shape, F32),
        compiler_params=_cparams(("arbitrary",)),
        name="moe_experts",
    )(block_e, live, first, slot, next_e, xb, w1, b1.reshape(depth, n_e, 1, two_f), w2,
      b2.reshape(depth, n_e, 1, d))


def _combine_kernel(dest_ref, yb_hbm, gate_ref, x_ref, mod_ref, lng_ref, lnb_ref, o_ref, buf, sem):
    tt = x_ref.shape[0]
    step = pl.program_id(0)
    n_steps = pl.num_programs(0)

    def start_gather(tile, slot):
        def issue(t, carry):
            for k in range(TOP_K):
                src = _token_tile(yb_hbm, dest_ref[(tile * tt + t) * TOP_K + k])
                pltpu.make_async_copy(src, _token_tile(buf.at[slot, k], t), sem.at[slot]).start()
            return carry

        lax.fori_loop(0, tt, issue, 0)

    slot = step % 2

    @pl.when(step == 0)
    def _():
        start_gather(step, slot)

    @pl.when(step + 1 < n_steps)
    def _():
        start_gather(step + 1, 1 - slot)

    for k in range(TOP_K):
        pltpu.make_async_copy(yb_hbm.at[pl.ds(0, tt * ROW_TILE), :], buf.at[slot, k], sem.at[slot]).wait()

    gates = gate_ref[...]
    y = None
    for k in range(TOP_K):
        term = gates[:, k:k + 1] * _tiles_to_rows(buf.at[slot, k])
        y = term if y is None else y + term
    o_ref[...] = _residual_ln(x_ref[...], y, mod_ref[0, 5:6, :], lng_ref[...], lnb_ref[...])


def _combine(dest, yb, gates, x1, mod, ln_g, ln_b, seq):
    t, d = x1.shape
    tt = min(COMBINE_TT, seq)
    per_b = seq // tt
    return pl.pallas_call(
        _combine_kernel,
        grid_spec=pltpu.PrefetchScalarGridSpec(
            num_scalar_prefetch=1,
            grid=(t // tt,),
            in_specs=[
                pl.BlockSpec(memory_space=pl.ANY),
                pl.BlockSpec((tt, LANES), lambda i, dst: (i, 0)),
                pl.BlockSpec((tt, d), lambda i, dst: (i, 0)),
                pl.BlockSpec((1, 6, d), lambda i, dst: (i // per_b, 0, 0)),
                pl.BlockSpec((1, d), lambda i, dst: (0, 0)),
                pl.BlockSpec((1, d), lambda i, dst: (0, 0)),
            ],
            out_specs=pl.BlockSpec((tt, d), lambda i, dst: (i, 0)),
            scratch_shapes=[pltpu.VMEM((2, TOP_K, tt * ROW_TILE, LANES), F32), pltpu.SemaphoreType.DMA((2,))],
        ),
        out_shape=jax.ShapeDtypeStruct((t, d), F32),
        compiler_params=_cparams(("arbitrary",)),
        name="moe_combine_ln",
    )(dest, yb, gates, x1, mod, ln_g.reshape(1, d), ln_b.reshape(1, d))


def _moe_block(x1, h3, idx, gates, mod, layer, w1, b1, w2, b2, ln_g, ln_b, seq):
    t = x1.shape[0]
    rank, counts = _rank_and_count(idx)
    dest, block_e, live = _routing_tables(idx, rank, counts)
    n_rows = _num_expert_blocks(t * TOP_K) * EXPERT_TM
    xb = _dispatch(dest, live, h3, n_rows)
    yb = _experts(block_e, live, xb, layer, w1, b1, w2, b2)
    return _combine(dest, yb, gates, x1, mod, ln_g, ln_b, seq)


def kernel(x, c, ada_w, ada_b, dn_in_w, dn_conv_w, dn_A_log, dn_dt_bias, dn_onorm_w, dn_out_w,
           cf_pw1_w, cf_pw1_b, cf_dw_w, cf_dw_b, cf_ln_g, cf_ln_b, cf_pw2_w, cf_pw2_b,
           ln1_g, ln1_b, router_w, router_b, e_w1, e_b1, e_w2, e_b2, ln2_g, ln2_b):
    bsz, seq, d = x.shape
    mods = _ada_ln(c, ada_w, ada_b)

    proj, ba = _dn_in_proj(x, mods[0], dn_in_w[0])
    qn, kn, vv, gates_dn = _dn_prep(proj, ba, dn_conv_w[0], dn_A_log[0], dn_dt_bias[0])
    o = _dn_chunk(qn, kn, vv, proj, gates_dn, dn_onorm_w[0])
    x1, h3, idx, gates = _dn_out(o, dn_out_w[0], x, mods[0], ln1_g[0], ln1_b[0], router_w[0], router_b[0])
    x2 = _moe_block(x1, h3, idx, gates, mods[0], 0, e_w1, e_b1, e_w2, e_b2, ln2_g[0], ln2_b[0], seq)

    x2 = x2.reshape(bsz, seq, d)
    u = _cf_pw1(x2, mods[1], cf_pw1_w[0], cf_pw1_b[0])
    x3, h3, idx, gates = _cf_tail(u, cf_dw_w[0], cf_dw_b[0], cf_ln_g[0], cf_ln_b[0], cf_pw2_w[0], cf_pw2_b[0],
                                  x2, mods[1], ln1_g[1], ln1_b[1], router_w[1], router_b[1])
    x4 = _moe_block(x3, h3, idx, gates, mods[1], 1, e_w1, e_b1, e_w2, e_b2, ln2_g[1], ln2_b[1], seq)
    return x4.reshape(bsz, seq, d)
```

```python
import functools

import jax
import jax.numpy as jnp
from jax import lax
from jax.experimental import pallas as pl
from jax.experimental.pallas import tpu as pltpu

F32 = jnp.float32
BF16 = jnp.bfloat16
HIGHEST = lax.Precision.HIGHEST

DEPTH = 2
DN_QK_HEADS = 8
DN_V_HEADS = 16
DN_HEAD_DIM = 128
DN_CONV = 4
DN_CHUNK = 64
CF_KERNEL = 31
N_EXPERTS = 32
TOP_K = 4
SWIGLU_LIMIT = 7.0
SWIGLU_ALPHA = 1.702
LN_EPS = 1e-5
RMS_EPS = 1e-6
L2_EPS = 1e-6
DEEPNORM_ALPHA = (2 * DEPTH) ** 0.25

LANES = 128
SUBLANES = 8
BF16_SUBLANES = 16
VMEM_LIMIT = 56 * 1024 * 1024

ADA_TN = 1536
PROJ_TM = 1024
PROJ_TN = 1024
PREP_TS = 512
CHUNKS_PER_STEP = 4
DN_CHUNK_UNROLL = 2
POST_TM = 512
RANK_TT = 512
DISPATCH_TT = 512
EXPERT_TM = 512
COMBINE_TT = 256
CF_TM = 512
CF_HALO = 32
CF_CONV_ROWS = 32

NEG_BIG = -1e30


def _cparams(sem):
    return pltpu.CompilerParams(dimension_semantics=sem, vmem_limit_bytes=VMEM_LIMIT)


def _sigmoid(x):
    return jax.nn.sigmoid(x)


def _layer_norm(v, g, b):
    mu = jnp.mean(v, -1, keepdims=True)
    d = v - mu
    var = jnp.mean(d * d, -1, keepdims=True)
    return d * lax.rsqrt(var + LN_EPS) * g + b


ROW_TILE = SUBLANES


def _rows_to_tiles(o_ref, val):
    rows = val.shape[0]
    for s in range(ROW_TILE):
        o_ref[pl.ds(s, rows, stride=ROW_TILE), :] = val[:, s * LANES:(s + 1) * LANES]


def _tiles_to_rows(x_ref):
    rows = x_ref.shape[0] // ROW_TILE
    return jnp.concatenate([x_ref[pl.ds(s, rows, stride=ROW_TILE), :] for s in range(ROW_TILE)], axis=1)


def _split_weight(w):
    k, n = w.shape
    hi = w.astype(BF16)
    lo = (w - hi.astype(F32)).astype(BF16)
    out = jnp.zeros((k, 2 * LANES), BF16)
    return out.at[:, :n].set(hi).at[:, LANES:LANES + n].set(lo)


def _narrow_dot(h, w_split_ref):
    m = h.shape[0]
    hi = h.astype(BF16)
    lo = (h - hi.astype(F32)).astype(BF16)
    out = jnp.dot(jnp.concatenate([hi, lo], axis=0), w_split_ref[...], preferred_element_type=F32)
    return out[:m, :LANES] + out[:m, LANES:] + out[m:, :LANES]


def _token_tile(ref, tok):
    return ref.at[pl.ds(pl.multiple_of(tok * ROW_TILE, ROW_TILE), ROW_TILE), :]


def _ada_kernel(c_ref, w_ref, b_ref, o_ref):
    c = c_ref[...]
    cond = c * _sigmoid(c)
    o_ref[0] = jnp.dot(cond, w_ref[0], precision=HIGHEST, preferred_element_type=F32) + b_ref[0]


def _ada_ln(c, ada_w, ada_b):
    depth, d, n = ada_w.shape
    bsz = c.shape[0]
    c_pad = jnp.zeros((SUBLANES, d), F32).at[:bsz].set(c)
    out = pl.pallas_call(
        _ada_kernel,
        grid=(depth, n // ADA_TN),
        in_specs=[
            pl.BlockSpec((SUBLANES, d), lambda i, j: (0, 0)),
            pl.BlockSpec((1, d, ADA_TN), lambda i, j: (i, 0, j)),
            pl.BlockSpec((1, 1, ADA_TN), lambda i, j: (i, 0, j)),
        ],
        out_specs=pl.BlockSpec((1, SUBLANES, ADA_TN), lambda i, j: (i, 0, j)),
        out_shape=jax.ShapeDtypeStruct((depth, SUBLANES, n), F32),
        compiler_params=_cparams(("parallel", "parallel")),
        name="ada_ln",
    )(c_pad, ada_w, ada_b.reshape(depth, 1, n))
    return out[:, :bsz].reshape(depth, bsz, 6, d)


def _inproj_kernel(x_ref, mod_ref, w_ref, wba_ref, proj_ref, ba_ref, h_scr):
    @pl.when(pl.program_id(2) == 0)
    def _():
        h = x_ref[0] * (1.0 + mod_ref[0, 1:2, :]) + mod_ref[0, 0:1, :]
        h_scr[...] = h.astype(BF16)
        ba_ref[0] = _narrow_dot(h, wba_ref)

    proj_ref[0] = jnp.dot(h_scr[...], w_ref[...], preferred_element_type=F32).astype(BF16)


def _dn_in_proj(x, mod, in_w):
    bsz, seq, d = x.shape
    n_main = in_w.shape[1] - 2 * DN_V_HEADS
    w_main = in_w[:, :n_main].astype(BF16)
    w_ba = _split_weight(in_w[:, n_main:])
    tm = min(PROJ_TM, seq)
    return pl.pallas_call(
        _inproj_kernel,
        grid=(bsz, seq // tm, n_main // PROJ_TN),
        in_specs=[
            pl.BlockSpec((1, tm, d), lambda b, i, j: (b, i, 0)),
            pl.BlockSpec((1, 6, d), lambda b, i, j: (b, 0, 0)),
            pl.BlockSpec((d, PROJ_TN), lambda b, i, j: (0, j)),
            pl.BlockSpec((d, 2 * LANES), lambda b, i, j: (0, 0)),
        ],
        out_specs=[
            pl.BlockSpec((1, tm, PROJ_TN), lambda b, i, j: (b, i, j)),
            pl.BlockSpec((1, tm, LANES), lambda b, i, j: (b, i, 0)),
        ],
        out_shape=[
            jax.ShapeDtypeStruct((bsz, seq, n_main), BF16),
            jax.ShapeDtypeStruct((bsz, seq, LANES), F32),
        ],
        scratch_shapes=[pltpu.VMEM((tm, d), BF16)],
        compiler_params=_cparams(("parallel", "parallel", "arbitrary")),
        name="dn_in_proj",
    )(x, mod, w_main, w_ba)


def _dn_prep_kernel(q_ref, k_ref, v_ref, qh_ref, kh_ref, vh_ref, cw_ref, ba_ref, alog_ref, dt_ref,
                    qo_ref, ko_ref, vo_ref, g_ref, scr):
    ts = q_ref.shape[1]
    halo = qh_ref.shape[1]
    keep = (pl.program_id(1) > 0).astype(F32)

    def conv_silu(x_ref, h_ref, c0):
        width = x_ref.shape[2]
        scr[0:halo, 0:width] = h_ref[0].astype(F32) * keep
        scr[halo:halo + ts, 0:width] = x_ref[0].astype(F32)
        acc = None
        for j in range(DN_CONV):
            off = halo - (DN_CONV - 1) + j
            term = scr[off:off + ts, 0:width] * cw_ref[j:j + 1, c0:c0 + width]
            acc = term if acc is None else acc + term
        return acc * _sigmoid(acc)

    def l2norm_store(o_ref, x, scale):
        for h in range(x.shape[1] // DN_HEAD_DIM):
            xh = x[:, h * DN_HEAD_DIM:(h + 1) * DN_HEAD_DIM]
            ss = jnp.sum(xh * xh, -1, keepdims=True)
            o_ref[0, :, h * DN_HEAD_DIM:(h + 1) * DN_HEAD_DIM] = (
                xh * lax.rsqrt(ss + L2_EPS) * scale).astype(o_ref.dtype)

    kd = q_ref.shape[2]
    l2norm_store(qo_ref, conv_silu(q_ref, qh_ref, 0), DN_HEAD_DIM ** -0.5)
    l2norm_store(ko_ref, conv_silu(k_ref, kh_ref, kd), 1.0)
    vo_ref[0] = conv_silu(v_ref, vh_ref, 2 * kd).astype(vo_ref.dtype)

    ba = ba_ref[0]
    beta = _sigmoid(ba)
    zz = ba + dt_ref[...]
    softplus = jnp.maximum(zz, 0.0) + jnp.log1p(jnp.exp(-jnp.abs(zz)))
    g = -jnp.exp(alog_ref[...]) * softplus
    r = lax.broadcasted_iota(jnp.int32, (ts, ts), 0)
    c = lax.broadcasted_iota(jnp.int32, (ts, ts), 1)
    in_chunk_tril = ((r // DN_CHUNK == c // DN_CHUNK) & (c <= r)).astype(BF16)
    g1 = g.astype(BF16)
    g2 = (g - g1.astype(F32)).astype(BF16)
    g3 = (g - g1.astype(F32) - g2.astype(F32)).astype(BF16)
    parts = jnp.dot(in_chunk_tril, jnp.concatenate([g1, g2, g3], axis=1), preferred_element_type=F32)
    gcum = parts[:, :LANES] + parts[:, LANES:2 * LANES] + parts[:, 2 * LANES:]
    lane = lax.broadcasted_iota(jnp.int32, ba.shape, 1)
    g_ref[0] = jnp.where(lane < DN_V_HEADS, beta, gcum)


def _dn_prep(proj, ba, conv_w, a_log, dt_bias):
    bsz, seq, _ = proj.shape
    kd = DN_QK_HEADS * DN_HEAD_DIM
    vd = DN_V_HEADS * DN_HEAD_DIM
    ts = min(PREP_TS, seq)
    halo = BF16_SUBLANES
    hb = ts // halo
    alog_row = jnp.zeros((1, LANES), F32).at[0, DN_V_HEADS:2 * DN_V_HEADS].set(a_log)
    dt_row = jnp.zeros((1, LANES), F32).at[0, DN_V_HEADS:2 * DN_V_HEADS].set(dt_bias)

    def halo_map(col):
        return lambda b, i: (b, jnp.maximum(i * hb - 1, 0), col)

    return pl.pallas_call(
        _dn_prep_kernel,
        grid=(bsz, seq // ts),
        in_specs=[
            pl.BlockSpec((1, ts, kd), lambda b, i: (b, i, 0)),
            pl.BlockSpec((1, ts, kd), lambda b, i: (b, i, 1)),
            pl.BlockSpec((1, ts, vd), lambda b, i: (b, i, 1)),
            pl.BlockSpec((1, halo, kd), halo_map(0)),
            pl.BlockSpec((1, halo, kd), halo_map(1)),
            pl.BlockSpec((1, halo, vd), halo_map(1)),
            pl.BlockSpec((DN_CONV, 2 * kd + vd), lambda b, i: (0, 0)),
            pl.BlockSpec((1, ts, LANES), lambda b, i: (b, i, 0)),
            pl.BlockSpec((1, LANES), lambda b, i: (0, 0)),
            pl.BlockSpec((1, LANES), lambda b, i: (0, 0)),
        ],
        out_specs=[
            pl.BlockSpec((1, ts, kd), lambda b, i: (b, i, 0)),
            pl.BlockSpec((1, ts, kd), lambda b, i: (b, i, 0)),
            pl.BlockSpec((1, ts, vd), lambda b, i: (b, i, 0)),
            pl.BlockSpec((1, ts, LANES), lambda b, i: (b, i, 0)),
        ],
        out_shape=[
            jax.ShapeDtypeStruct((bsz, seq, kd), BF16),
            jax.ShapeDtypeStruct((bsz, seq, kd), BF16),
            jax.ShapeDtypeStruct((bsz, seq, vd), BF16),
            jax.ShapeDtypeStruct((bsz, seq, LANES), F32),
        ],
        scratch_shapes=[pltpu.VMEM((ts + halo, vd), F32)],
        compiler_params=_cparams(("parallel", "arbitrary")),
        name="dn_prep",
    )(proj, proj, proj, proj, proj, proj, conv_w, ba, alog_row, dt_row)


def _bmm(a, b):
    return lax.dot_general(a.astype(BF16), b.astype(BF16), (((2,), (1,)), ((0,), (0,))),
                           preferred_element_type=F32)


def _bmm_nt(a, b):
    return lax.dot_general(a.astype(BF16), b.astype(BF16), (((2,), (2,)), ((0,), (0,))),
                           preferred_element_type=F32)


def _bmm_tn(a, b):
    return lax.dot_general(a.astype(BF16), b.astype(BF16), (((1,), (1,)), ((0,), (0,))),
                           preferred_element_type=F32)


def _unit_lower_inverse_wide(a_twice, upper, eye_upper):
    c = a_twice.shape[1]
    x = jnp.where(upper, eye_upper, -a_twice)
    span = 1
    while span < c:
        x = _bmm(x[:, :, :c], x) + jnp.where(upper, x, 0.0)
        span *= 2
    return x


def _dn_chunk_kernel(q_ref, k_ref, v_ref, z_ref, g_ref, gt_ref, ow_ref, o_ref, s_ref):
    c_len = DN_CHUNK
    dh = DN_HEAD_DIM
    rep = DN_V_HEADS // DN_QK_HEADS

    @pl.when(pl.program_id(1) == 0)
    def _():
        s_ref[...] = jnp.zeros(s_ref.shape, F32)

    heads = range(DN_V_HEADS)
    n_heads = DN_V_HEADS
    ri = lax.broadcasted_iota(jnp.int32, (1, c_len, 2 * c_len), 1)
    lane = lax.broadcasted_iota(jnp.int32, (1, c_len, 2 * c_len), 2)
    upper = lane >= c_len
    ci = jnp.where(upper, lane - c_len, lane)
    causal = ci <= ri
    strict = ci < ri
    eye_upper = ((ci == ri) & upper).astype(F32)
    onorm = ow_ref[...]

    def head_cols(ref, rows, h):
        return ref[0, rows, h * dh:(h + 1) * dh]

    def per_v_head(t):
        return jnp.stack([t[h // rep] for h in heads])

    def state_free_part(c):
        r0 = pl.multiple_of(c * c_len, c_len)
        rows = pl.ds(r0, c_len)
        gcols = g_ref[0, rows, :]
        grows = gt_ref[0, c]
        wide = (n_heads, c_len, dh)
        beta_b = jnp.broadcast_to(jnp.stack([gcols[:, h:h + 1] for h in heads]), wide)
        g_b = jnp.broadcast_to(
            jnp.stack([gcols[:, DN_V_HEADS + h:DN_V_HEADS + h + 1] for h in heads]), wide)
        g_r = jnp.stack([grows[DN_V_HEADS + h:DN_V_HEADS + h + 1, :] for h in heads])
        g_r = jnp.concatenate([g_r, g_r], axis=2)
        g_last = g_b[:, c_len - 1:c_len, :]

        qn = jnp.stack([head_cols(q_ref, rows, hq) for hq in range(DN_QK_HEADS)])
        kn = jnp.stack([head_cols(k_ref, rows, hq) for hq in range(DN_QK_HEADS)])
        qk_kk = _bmm_nt(jnp.concatenate([qn, kn], axis=1), jnp.concatenate([kn, kn], axis=1))
        qk = per_v_head(qk_kk[:, :c_len])
        kk = per_v_head(qk_kk[:, c_len:])
        qf = per_v_head(qn).astype(F32)
        kf = per_v_head(kn).astype(F32)
        vf = jnp.stack([head_cols(v_ref, rows, h) for h in heads]).astype(F32)

        decay = jnp.where(causal, jnp.exp(g_b - g_r), 0.0)
        a_twice = jnp.where(strict, beta_b * kk * decay, 0.0)
        x_inv = _unit_lower_inverse_wide(a_twice, upper, eye_upper)
        eg = jnp.exp(g_b)
        rhs = jnp.concatenate([vf * beta_b, kf * (beta_b * eg)], axis=2)
        sol = _bmm(x_inv, jnp.concatenate([jnp.zeros_like(rhs), rhs], axis=1))
        u = sol[:, :, :dh]
        w = sol[:, :, dh:]
        qkm = jnp.where(causal, qk * decay, 0.0)[:, :, :c_len]
        qg = qf * eg
        kdec = kf * jnp.exp(g_last - g_b)
        return rows, jnp.concatenate([w, qg], axis=1), u, qkm, kdec, jnp.exp(g_last)

    def state_part(rows, w_qg, u, qkm, kdec, chunk_decay):
        state = s_ref[...]
        ws = _bmm(w_qg, state)
        v_new = u - ws[:, :c_len]
        o = ws[:, c_len:] + _bmm(qkm, v_new)
        s_ref[...] = state * chunk_decay + _bmm_tn(kdec, v_new)

        o = o * lax.rsqrt(jnp.mean(o * o, -1, keepdims=True) + RMS_EPS) * onorm
        for h in heads:
            zf = head_cols(z_ref, rows, h).astype(F32)
            o_ref[0, rows, h * dh:(h + 1) * dh] = (o[h] * (zf * _sigmoid(zf))).astype(o_ref.dtype)

    def chunk_group(i, carry):
        prepared = [state_free_part(i * DN_CHUNK_UNROLL + j) for j in range(DN_CHUNK_UNROLL)]
        for args in prepared:
            state_part(*args)
        return carry

    lax.fori_loop(0, q_ref.shape[1] // (c_len * DN_CHUNK_UNROLL), chunk_group, 0)


def _dn_chunk(qn, kn, vv, proj, gates, onorm_w):
    bsz, seq, kd = qn.shape
    vd = vv.shape[2]
    n_chunks = seq // DN_CHUNK
    cb = min(CHUNKS_PER_STEP, n_chunks)
    rows = cb * DN_CHUNK
    gates_t = jnp.swapaxes(gates[:, :, :2 * DN_V_HEADS].reshape(bsz, n_chunks, DN_CHUNK, 2 * DN_V_HEADS), 2, 3)
    z_col = (2 * kd + vd) // vd
    return pl.pallas_call(
        _dn_chunk_kernel,
        grid=(bsz, n_chunks // cb),
        in_specs=[
            pl.BlockSpec((1, rows, kd), lambda b, n: (b, n, 0)),
            pl.BlockSpec((1, rows, kd), lambda b, n: (b, n, 0)),
            pl.BlockSpec((1, rows, vd), lambda b, n: (b, n, 0)),
            pl.BlockSpec((1, rows, vd), lambda b, n: (b, n, z_col)),
            pl.BlockSpec((1, rows, LANES), lambda b, n: (b, n, 0)),
            pl.BlockSpec((1, cb, 2 * DN_V_HEADS, DN_CHUNK), lambda b, n: (b, n, 0, 0)),
            pl.BlockSpec((1, DN_HEAD_DIM), lambda b, n: (0, 0)),
        ],
        out_specs=pl.BlockSpec((1, rows, vd), lambda b, n: (b, n, 0)),
        out_shape=jax.ShapeDtypeStruct((bsz, seq, vd), BF16),
        scratch_shapes=[pltpu.VMEM((DN_V_HEADS, DN_HEAD_DIM, DN_HEAD_DIM), F32)],
        compiler_params=_cparams(("parallel", "arbitrary")),
        name="dn_chunk",
    )(qn, kn, vv, proj, gates, gates_t, onorm_w.reshape(1, DN_HEAD_DIM))


def _residual_ln(x, y, gate_row, ln_g, ln_b):
    return _layer_norm(DEEPNORM_ALPHA * x + (1.0 + gate_row) * y, ln_g, ln_b)


def _route_store(h, rw_ref, rb_ref, idx_ref, gate_ref):
    logits = _narrow_dot(h, rw_ref) + rb_ref[...]
    lane = lax.broadcasted_iota(jnp.int32, logits.shape, 1).astype(F32)
    work = logits
    idx_out = jnp.zeros(logits.shape, F32)
    val_out = jnp.full(logits.shape, NEG_BIG, F32)
    for k in range(TOP_K):
        m = jnp.max(work, -1, keepdims=True)
        am = jnp.min(jnp.where(work == m, lane, float(LANES)), -1, keepdims=True)
        idx_out = jnp.where(lane == k, am, idx_out)
        val_out = jnp.where(lane == k, m, val_out)
        work = jnp.where(lane == am, NEG_BIG * 2.0, work)
    top = jnp.max(val_out, -1, keepdims=True)
    e = jnp.where(lane < TOP_K, jnp.exp(val_out - top), 0.0)
    idx_ref[...] = idx_out.astype(jnp.int32)
    gate_ref[...] = e / jnp.sum(e, -1, keepdims=True)


def _post_mixer_tail(x, y, mod_ref, lng_ref, lnb_ref, rw_ref, rb_ref, x1_ref, h3_ref, idx_ref, gate_ref):
    x1 = _residual_ln(x, y, mod_ref[0, 2:3, :], lng_ref[...], lnb_ref[...])
    x1_ref[...] = x1
    h2 = x1 * (1.0 + mod_ref[0, 4:5, :]) + mod_ref[0, 3:4, :]
    _rows_to_tiles(h3_ref, h2)
    _route_store(h2, rw_ref, rb_ref, idx_ref, gate_ref)


def _router_operands(router_w, router_b):
    rb = jnp.full((1, LANES), NEG_BIG, F32).at[0, :N_EXPERTS].set(router_b)
    return _split_weight(router_w), rb


def _post_out_specs(tm, d):
    return [
        pl.BlockSpec((tm, d), lambda i: (i, 0)),
        pl.BlockSpec((tm * ROW_TILE, LANES), lambda i: (i, 0)),
        pl.BlockSpec((tm, LANES), lambda i: (i, 0)),
        pl.BlockSpec((tm, LANES), lambda i: (i, 0)),
    ]


def _post_out_shapes(t, d):
    assert d == ROW_TILE * LANES
    return [
        jax.ShapeDtypeStruct((t, d), F32),
        jax.ShapeDtypeStruct((t * ROW_TILE, LANES), F32),
        jax.ShapeDtypeStruct((t, LANES), jnp.int32),
        jax.ShapeDtypeStruct((t, LANES), F32),
    ]


def _dn_out_kernel(o_ref, w_ref, x_ref, mod_ref, lng_ref, lnb_ref, rw_ref, rb_ref,
                   x1_ref, h3_ref, idx_ref, gate_ref):
    y = jnp.dot(o_ref[...], w_ref[...], preferred_element_type=F32)
    _post_mixer_tail(x_ref[...], y, mod_ref, lng_ref, lnb_ref, rw_ref, rb_ref,
                     x1_ref, h3_ref, idx_ref, gate_ref)


def _dn_out(o, out_w, x, mod, ln_g, ln_b, router_w, router_b):
    bsz, seq, d = x.shape
    t = bsz * seq
    vd = o.shape[2]
    tm = min(POST_TM, seq)
    rw, rb = _router_operands(router_w, router_b)
    per_b = seq // tm
    return pl.pallas_call(
        _dn_out_kernel,
        grid=(t // tm,),
        in_specs=[
            pl.BlockSpec((tm, vd), lambda i: (i, 0)),
            pl.BlockSpec((vd, d), lambda i: (0, 0)),
            pl.BlockSpec((tm, d), lambda i: (i, 0)),
            pl.BlockSpec((1, 6, d), lambda i: (i // per_b, 0, 0)),
            pl.BlockSpec((1, d), lambda i: (0, 0)),
            pl.BlockSpec((1, d), lambda i: (0, 0)),
            pl.BlockSpec((d, 2 * LANES), lambda i: (0, 0)),
            pl.BlockSpec((1, LANES), lambda i: (0, 0)),
        ],
        out_specs=_post_out_specs(tm, d),
        out_shape=_post_out_shapes(t, d),
        compiler_params=_cparams(("parallel",)),
        name="dn_out_ln_route",
    )(o.reshape(t, vd), out_w.astype(BF16), x.reshape(t, d), mod, ln_g.reshape(1, d), ln_b.reshape(1, d), rw, rb)


def _cf_pw1_kernel(x_ref, mod_ref, wa_ref, wb_ref, ba_ref, bb_ref, u_ref):
    h = (x_ref[0] * (1.0 + mod_ref[0, 1:2, :]) + mod_ref[0, 0:1, :]).astype(BF16)
    pa = jnp.dot(h, wa_ref[...], preferred_element_type=F32) + ba_ref[...]
    pb = jnp.dot(h, wb_ref[...], preferred_element_type=F32) + bb_ref[...]
    _rows_to_tiles(u_ref, pa * _sigmoid(pb))


def _cf_pw1(x, mod, pw1_w, pw1_b):
    bsz, seq, d = x.shape
    inner = pw1_w.shape[1] // 2
    assert inner == ROW_TILE * LANES
    tm = min(PROJ_TM, seq)
    per_b = seq // tm
    w = pw1_w.astype(BF16)
    b = pw1_b.reshape(1, 2 * inner)
    return pl.pallas_call(
        _cf_pw1_kernel,
        grid=(bsz, seq // tm),
        in_specs=[
            pl.BlockSpec((1, tm, d), lambda b_, i: (b_, i, 0)),
            pl.BlockSpec((1, 6, d), lambda b_, i: (b_, 0, 0)),
            pl.BlockSpec((d, inner), lambda b_, i: (0, 0)),
            pl.BlockSpec((d, inner), lambda b_, i: (0, 1)),
            pl.BlockSpec((1, inner), lambda b_, i: (0, 0)),
            pl.BlockSpec((1, inner), lambda b_, i: (0, 1)),
        ],
        out_specs=pl.BlockSpec((tm * ROW_TILE, LANES), lambda b_, i: (b_ * per_b + i, 0)),
        out_shape=jax.ShapeDtypeStruct((bsz * seq * ROW_TILE, LANES), F32),
        compiler_params=_cparams(("parallel", "parallel")),
        name="cf_pw1_glu",
    )(x, mod, w, w, b, b)


def _cf_tail_kernel(u_ref, uh_ref, dw_ref, dwb_ref, cg_ref, cb_ref, w2_ref, b2_ref,
                    x_ref, mod_ref, lng_ref, lnb_ref, rw_ref, rb_ref,
                    x1_ref, h3_ref, idx_ref, gate_ref, scr, conv_scr):
    tm = u_ref.shape[0] // ROW_TILE
    halo = uh_ref.shape[0] // ROW_TILE
    keep = (pl.program_id(1) > 0).astype(F32)
    scr[0:halo * ROW_TILE, :] = uh_ref[...] * keep
    scr[halo * ROW_TILE:(halo + tm) * ROW_TILE, :] = u_ref[...]
    first = halo - (CF_KERNEL - 1)
    blk = CF_CONV_ROWS * ROW_TILE

    def conv_tokens(tb, carry):
        r0 = pl.multiple_of(tb * blk, blk)
        acc = jnp.concatenate([dwb_ref[...]] * CF_CONV_ROWS, axis=0)
        for j in range(CF_KERNEL):
            w_tile = dw_ref[j * ROW_TILE:(j + 1) * ROW_TILE, :]
            w_blk = jnp.concatenate([w_tile] * CF_CONV_ROWS, axis=0)
            acc = acc + scr[pl.ds(r0 + (first + j) * ROW_TILE, blk), :] * w_blk
        conv_scr[pl.ds(r0, blk), :] = acc
        return carry

    lax.fori_loop(0, tm // CF_CONV_ROWS, conv_tokens, 0)
    conv = _tiles_to_rows(conv_scr)
    normed = _layer_norm(conv, cg_ref[...], cb_ref[...])
    act = normed * _sigmoid(normed)
    y = jnp.dot(act.astype(BF16), w2_ref[...], preferred_element_type=F32) + b2_ref[...]
    _post_mixer_tail(x_ref[0], y, mod_ref, lng_ref, lnb_ref, rw_ref, rb_ref,
                     x1_ref, h3_ref, idx_ref, gate_ref)


def _cf_tail(u, dw_w, dw_b, cf_ln_g, cf_ln_b, pw2_w, pw2_b, x, mod, ln_g, ln_b, router_w, router_b):
    bsz, seq, d = x.shape
    inner = dw_w.shape[1]
    t = bsz * seq
    tm = min(CF_TM, seq)
    per_b = seq // tm
    hb = tm // CF_HALO
    halos_per_b = seq // CF_HALO
    rw, rb = _router_operands(router_w, router_b)
    row = lambda v: v.reshape(1, -1)
    tiles = lambda v: v.reshape(-1, LANES)
    const = lambda b, i: (0, 0)
    flat = lambda b, i: (b * per_b + i, 0)
    return pl.pallas_call(
        _cf_tail_kernel,
        grid=(bsz, per_b),
        in_specs=[
            pl.BlockSpec((tm * ROW_TILE, LANES), flat),
            pl.BlockSpec((CF_HALO * ROW_TILE, LANES),
                         lambda b, i: (b * halos_per_b + jnp.maximum(i * hb - 1, 0), 0)),
            pl.BlockSpec((CF_KERNEL * ROW_TILE, LANES), const),
            pl.BlockSpec((ROW_TILE, LANES), const),
            pl.BlockSpec((1, inner), const),
            pl.BlockSpec((1, inner), const),
            pl.BlockSpec((inner, d), const),
            pl.BlockSpec((1, d), const),
            pl.BlockSpec((1, tm, d), lambda b, i: (b, i, 0)),
            pl.BlockSpec((1, 6, d), lambda b, i: (b, 0, 0)),
            pl.BlockSpec((1, d), const),
            pl.BlockSpec((1, d), const),
            pl.BlockSpec((d, 2 * LANES), const),
            pl.BlockSpec((1, LANES), const),
        ],
        out_specs=[
            pl.BlockSpec((tm, d), flat),
            pl.BlockSpec((tm * ROW_TILE, LANES), flat),
            pl.BlockSpec((tm, LANES), flat),
            pl.BlockSpec((tm, LANES), flat),
        ],
        out_shape=_post_out_shapes(t, d),
        scratch_shapes=[pltpu.VMEM(((tm + CF_HALO) * ROW_TILE, LANES), F32),
                        pltpu.VMEM((tm * ROW_TILE, LANES), F32)],
        compiler_params=_cparams(("parallel", "arbitrary")),
        name="cf_conv_ln_pw2_route",
    )(u, u, tiles(dw_w), tiles(dw_b), row(cf_ln_g), row(cf_ln_b), pw2_w.astype(BF16), row(pw2_b),
      x, mod, row(ln_g), row(ln_b), rw, rb)


def _rank_kernel(idx_ref, rank_t_ref, idx_t_ref, cnt_ref, carry):
    tt = idx_ref.shape[0]

    @pl.when(pl.program_id(0) == 0)
    def _():
        carry[...] = jnp.zeros(carry.shape, F32)

    idx = idx_ref[...]
    lane = lax.broadcasted_iota(jnp.int32, idx.shape, 1)
    sel = [lane == idx[:, k:k + 1] for k in range(TOP_K)]
    multi_hot = sel[0]
    for k in range(1, TOP_K):
        multi_hot = multi_hot | sel[k]
    mh = multi_hot.astype(BF16)
    r = lax.broadcasted_iota(jnp.int32, (tt, tt), 0)
    c = lax.broadcasted_iota(jnp.int32, (tt, tt), 1)
    before = jnp.dot((c < r).astype(BF16), mh, preferred_element_type=F32) + carry[...]
    rank = jnp.zeros(idx.shape, F32)
    for k in range(TOP_K):
        rk = jnp.sum(jnp.where(sel[k], before, 0.0), -1, keepdims=True)
        rank = jnp.where(lane == k, rk, rank)
    rank_t_ref[...] = jnp.transpose(rank)[:SUBLANES].astype(jnp.int32)
    idx_t_ref[...] = jnp.transpose(idx.astype(F32))[:SUBLANES].astype(jnp.int32)
    total = carry[...] + jnp.sum(mh.astype(F32), 0, keepdims=True)
    carry[...] = total
    cnt_ref[...] = total.astype(jnp.int32)


def _rank_and_count(idx):
    t = idx.shape[0]
    tt = min(RANK_TT, t)
    return pl.pallas_call(
        _rank_kernel,
        grid=(t // tt,),
        in_specs=[pl.BlockSpec((tt, LANES), lambda i: (i, 0))],
        out_specs=[
            pl.BlockSpec((SUBLANES, tt), lambda i: (0, i)),
            pl.BlockSpec((SUBLANES, tt), lambda i: (0, i)),
            pl.BlockSpec((1, LANES), lambda i: (0, 0)),
        ],
        out_shape=[
            jax.ShapeDtypeStruct((SUBLANES, t), jnp.int32),
            jax.ShapeDtypeStruct((SUBLANES, t), jnp.int32),
            jax.ShapeDtypeStruct((1, LANES), jnp.int32),
        ],
        scratch_shapes=[pltpu.VMEM((1, LANES), F32)],
        compiler_params=_cparams(("arbitrary",)),
        name="moe_rank_count",
    )(idx)


def _dest_kernel(pstart_ref, idx_t_ref, rank_t_ref, dest_t_ref):
    idx = idx_t_ref[...]
    base = jnp.zeros(idx.shape, jnp.int32)
    for e in range(N_EXPERTS):
        base = jnp.where(idx == e, pstart_ref[e], base)
    dest_t_ref[...] = base + rank_t_ref[...]


def _dest_rows(pstarts, idx_t, rank_t):
    t = idx_t.shape[1]
    dest_t = pl.pallas_call(
        _dest_kernel,
        grid_spec=pltpu.PrefetchScalarGridSpec(
            num_scalar_prefetch=1,
            grid=(1,),
            in_specs=[pl.BlockSpec((SUBLANES, t), lambda i, ps: (0, 0)),
                      pl.BlockSpec((SUBLANES, t), lambda i, ps: (0, 0))],
            out_specs=pl.BlockSpec((SUBLANES, t), lambda i, ps: (0, 0)),
        ),
        out_shape=jax.ShapeDtypeStruct((SUBLANES, t), jnp.int32),
        compiler_params=_cparams(("arbitrary",)),
        name="moe_dest_rows",
    )(pstarts, idx_t, rank_t)
    return dest_t[:TOP_K].reshape(TOP_K * t)


def _num_expert_blocks(n_assign):
    bound = n_assign + N_EXPERTS * (EXPERT_TM - 1)
    return -(-bound // EXPERT_TM)


def _routing_tables(n_tokens, counts):
    nb = _num_expert_blocks(n_tokens * TOP_K)
    counts = counts[0, :N_EXPERTS]
    padded = (counts + EXPERT_TM - 1) // EXPERT_TM * EXPERT_TM
    pends = jnp.cumsum(padded)
    pstarts = (pends - padded).astype(jnp.int32)
    block_row0 = jnp.arange(nb, dtype=jnp.int32) * EXPERT_TM
    segments_done = jnp.sum((pends[None, :] <= block_row0[:, None]).astype(jnp.int32), axis=1)
    block_e = jnp.minimum(segments_done, N_EXPERTS - 1)
    live = jnp.clip(counts[block_e] - (block_row0 - pstarts[block_e]), 0, EXPERT_TM).astype(jnp.int32)
    return pstarts, block_e, live


def _expert_run_tables(block_e):
    nb = block_e.shape[0]
    pos = jnp.arange(nb, dtype=jnp.int32)
    first = jnp.concatenate([jnp.ones((1,), jnp.int32), (block_e[1:] != block_e[:-1]).astype(jnp.int32)])
    slot = (jnp.cumsum(first) - 1) % 2
    start_at_or_after = lax.cummin(jnp.where(first == 1, pos, nb), reverse=True)
    next_start = jnp.concatenate([start_at_or_after[1:], jnp.full((1,), nb, jnp.int32)])
    next_e = jnp.where(next_start < nb, block_e[jnp.minimum(next_start, nb - 1)], -1)
    return first, slot.astype(jnp.int32), next_e.astype(jnp.int32)


def _pad_fill_copies(live_ref, i, zeros, xb_hbm, sem):
    lv = live_ref[i]
    pad = EXPERT_TM - lv
    row0 = i * EXPERT_TM + lv
    copies = []
    piece = EXPERT_TM
    while piece >= 1:
        offset = pad & ~(2 * piece - 1)
        start = pl.multiple_of((row0 + offset) * ROW_TILE, ROW_TILE)
        copy = pltpu.make_async_copy(zeros.at[pl.ds(0, piece * ROW_TILE), :],
                                     xb_hbm.at[pl.ds(start, piece * ROW_TILE), :], sem)
        copies.append(((pad & piece) != 0, copy))
        piece //= 2
    return copies


def _dispatch_kernel(dest_ref, live_ref, h_ref, xb_hbm, zeros, sem, fill_sem):
    tt = h_ref.shape[0] // ROW_TILE
    base = pl.program_id(0) * tt
    n_tokens = pl.num_programs(0) * tt
    n_blocks = xb_hbm.shape[0] // (EXPERT_TM * ROW_TILE)

    @pl.when(pl.program_id(0) == 0)
    def _():
        zeros[...] = jnp.zeros(zeros.shape, F32)

        def start_fill(i, carry):
            for live_bit, copy in _pad_fill_copies(live_ref, i, zeros, xb_hbm, fill_sem):
                pl.when(live_bit)(copy.start)
            return carry

        lax.fori_loop(0, n_blocks, start_fill, 0)

    def issue(t, carry):
        for k in range(TOP_K):
            pltpu.make_async_copy(_token_tile(h_ref, t),
                                  _token_tile(xb_hbm, dest_ref[k * n_tokens + base + t]), sem).start()
        return carry

    lax.fori_loop(0, tt, issue, 0)
    for _ in range(TOP_K):
        pltpu.make_async_copy(h_ref, xb_hbm.at[pl.ds(0, tt * ROW_TILE), :], sem).wait()

    @pl.when(pl.program_id(0) == 0)
    def _():
        def wait_fill(i, carry):
            for live_bit, copy in _pad_fill_copies(live_ref, i, zeros, xb_hbm, fill_sem):
                pl.when(live_bit)(copy.wait)
            return carry

        lax.fori_loop(0, n_blocks, wait_fill, 0)


def _dispatch(dest, live, h3, n_rows):
    t = h3.shape[0] // ROW_TILE
    return pl.pallas_call(
        _dispatch_kernel,
        grid_spec=pltpu.PrefetchScalarGridSpec(
            num_scalar_prefetch=2,
            grid=(t // DISPATCH_TT,),
            in_specs=[pl.BlockSpec((DISPATCH_TT * ROW_TILE, LANES), lambda i, dst, lv: (i, 0))],
            out_specs=pl.BlockSpec(memory_space=pl.ANY),
            scratch_shapes=[pltpu.VMEM((EXPERT_TM * ROW_TILE, LANES), F32), pltpu.SemaphoreType.DMA,
                            pltpu.SemaphoreType.DMA],
        ),
        out_shape=jax.ShapeDtypeStruct((n_rows * ROW_TILE, LANES), F32),
        compiler_params=_cparams(("arbitrary",)),
        name="moe_dispatch",
    )(dest, live, h3)


def _expert_kernel(be_ref, live_ref, first_ref, slot_ref, next_ref, x_ref, w1_hbm, b1_ref, w2_hbm, b2_ref,
                   y_ref, w1f, w2f, w1s, w2s, sem, *, layer):
    i = pl.program_id(0)
    live = live_ref[i]

    def weight_copies(e, slot):
        return (pltpu.make_async_copy(w1_hbm.at[layer, e], w1f.at[slot], sem.at[0, slot]),
                pltpu.make_async_copy(w2_hbm.at[layer, e], w2f.at[slot], sem.at[1, slot]))

    @pl.when(i == 0)
    def _():
        for copy in weight_copies(be_ref[0], 0):
            copy.start()

    @pl.when(first_ref[i] == 1)
    def _():
        slot = slot_ref[i]
        for copy in weight_copies(be_ref[i], slot):
            copy.wait()
        nxt = next_ref[i]

        @pl.when(nxt >= 0)
        def _():
            for copy in weight_copies(nxt, 1 - slot):
                copy.start()

        w1s[...] = w1f[slot].astype(BF16)
        w2s[...] = w2f[slot].astype(BF16)

    @pl.when(live > 0)
    def _():
        dff = w2s.shape[0]
        x = _tiles_to_rows(x_ref)
        row = lax.broadcasted_iota(jnp.int32, (x.shape[0], 1), 0)
        x = jnp.where(row < live, x, 0.0).astype(BF16)
        hh = jnp.dot(x, w1s[...], preferred_element_type=F32) + b1_ref[0, 0]
        x_glu = jnp.minimum(hh[:, :dff], SWIGLU_LIMIT)
        x_lin = jnp.clip(hh[:, dff:], -SWIGLU_LIMIT, SWIGLU_LIMIT)
        act = x_glu * _sigmoid(SWIGLU_ALPHA * x_glu) * (x_lin + 1.0)
        y = jnp.dot(act.astype(BF16), w2s[...], preferred_element_type=F32) + b2_ref[0, 0]
        _rows_to_tiles(y_ref, y)

    @pl.when(live == 0)
    def _():
        y_ref[...] = jnp.zeros(y_ref.shape, F32)


def _experts(block_e, live, xb, layer, w1, b1, w2, b2):
    depth, n_e, d, two_f = w1.shape
    dff = two_f // 2
    block_rows = EXPERT_TM * ROW_TILE
    nb = xb.shape[0] // block_rows
    first, slot, next_e = _expert_run_tables(block_e)
    bias_map = lambda i, be, *_: (layer, be[i], 0, 0)
    return pl.pallas_call(
        functools.partial(_expert_kernel, layer=layer),
        grid_spec=pltpu.PrefetchScalarGridSpec(
            num_scalar_prefetch=5,
            grid=(nb,),
            in_specs=[
                pl.BlockSpec((block_rows, LANES), lambda i, *_: (i, 0)),
                pl.BlockSpec(memory_space=pl.ANY),
                pl.BlockSpec((1, 1, 1, two_f), bias_map),
                pl.BlockSpec(memory_space=pl.ANY),
                pl.BlockSpec((1, 1, 1, d), bias_map),
            ],
            out_specs=pl.BlockSpec((block_rows, LANES), lambda i, *_: (i, 0)),
            scratch_shapes=[pltpu.VMEM((2, d, two_f), F32), pltpu.VMEM((2, dff, d), F32),
                            pltpu.VMEM((d, two_f), BF16), pltpu.VMEM((dff, d), BF16),
                            pltpu.SemaphoreType.DMA((2, 2))],
        ),
        out_shape=jax.ShapeDtypeStruct(xb.shape, F32),
        compiler_params=_cparams(("arbitrary",)),
        name="moe_experts",
    )(block_e, live, first, slot, next_e, xb, w1, b1.reshape(depth, n_e, 1, two_f), w2,
      b2.reshape(depth, n_e, 1, d))


def _combine_kernel(dest_ref, yb_hbm, gate_ref, x_ref, mod_ref, lng_ref, lnb_ref, o_ref, buf, sem):
    tt = x_ref.shape[0]
    step = pl.program_id(0)
    n_steps = pl.num_programs(0)
    n_tokens = n_steps * tt

    def start_gather(tile, slot):
        def issue(t, carry):
            for k in range(TOP_K):
                src = _token_tile(yb_hbm, dest_ref[k * n_tokens + tile * tt + t])
                pltpu.make_async_copy(src, _token_tile(buf.at[slot, k], t), sem.at[slot]).start()
            return carry

        lax.fori_loop(0, tt, issue, 0)

    slot = step % 2

    @pl.when(step == 0)
    def _():
        start_gather(step, slot)

    @pl.when(step + 1 < n_steps)
    def _():
        start_gather(step + 1, 1 - slot)

    for k in range(TOP_K):
        pltpu.make_async_copy(yb_hbm.at[pl.ds(0, tt * ROW_TILE), :], buf.at[slot, k], sem.at[slot]).wait()

    gates = gate_ref[...]
    y = None
    for k in range(TOP_K):
        term = gates[:, k:k + 1] * _tiles_to_rows(buf.at[slot, k])
        y = term if y is None else y + term
    o_ref[...] = _residual_ln(x_ref[...], y, mod_ref[0, 5:6, :], lng_ref[...], lnb_ref[...])


def _combine(dest, yb, gates, x1, mod, ln_g, ln_b, seq):
    t, d = x1.shape
    tt = min(COMBINE_TT, seq)
    per_b = seq // tt
    return pl.pallas_call(
        _combine_kernel,
        grid_spec=pltpu.PrefetchScalarGridSpec(
            num_scalar_prefetch=1,
            grid=(t // tt,),
            in_specs=[
                pl.BlockSpec(memory_space=pl.ANY),
                pl.BlockSpec((tt, LANES), lambda i, dst: (i, 0)),
                pl.BlockSpec((tt, d), lambda i, dst: (i, 0)),
                pl.BlockSpec((1, 6, d), lambda i, dst: (i // per_b, 0, 0)),
                pl.BlockSpec((1, d), lambda i, dst: (0, 0)),
                pl.BlockSpec((1, d), lambda i, dst: (0, 0)),
            ],
            out_specs=pl.BlockSpec((tt, d), lambda i, dst: (i, 0)),
            scratch_shapes=[pltpu.VMEM((2, TOP_K, tt * ROW_TILE, LANES), F32), pltpu.SemaphoreType.DMA((2,))],
        ),
        out_shape=jax.ShapeDtypeStruct((t, d), F32),
        compiler_params=_cparams(("arbitrary",)),
        name="moe_combine_ln",
    )(dest, yb, gates, x1, mod, ln_g.reshape(1, d), ln_b.reshape(1, d))


def _moe_block(x1, h3, idx, gates, mod, layer, w1, b1, w2, b2, ln_g, ln_b, seq):
    t = x1.shape[0]
    rank_t, idx_t, counts = _rank_and_count(idx)
    pstarts, block_e, live = _routing_tables(t, counts)
    dest = _dest_rows(pstarts, idx_t, rank_t)
    n_rows = _num_expert_blocks(t * TOP_K) * EXPERT_TM
    xb = _dispatch(dest, live, h3, n_rows)
    yb = _experts(block_e, live, xb, layer, w1, b1, w2, b2)
    return _combine(dest, yb, gates, x1, mod, ln_g, ln_b, seq)


def kernel(x, c, ada_w, ada_b, dn_in_w, dn_conv_w, dn_A_log, dn_dt_bias, dn_onorm_w, dn_out_w,
           cf_pw1_w, cf_pw1_b, cf_dw_w, cf_dw_b, cf_ln_g, cf_ln_b, cf_pw2_w, cf_pw2_b,
           ln1_g, ln1_b, router_w, router_b, e_w1, e_b1, e_w2, e_b2, ln2_g, ln2_b):
    bsz, seq, d = x.shape
    mods = _ada_ln(c, ada_w, ada_b)

    proj, ba = _dn_in_proj(x, mods[0], dn_in_w[0])
    qn, kn, vv, gates_dn = _dn_prep(proj, ba, dn_conv_w[0], dn_A_log[0], dn_dt_bias[0])
    o = _dn_chunk(qn, kn, vv, proj, gates_dn, dn_onorm_w[0])
    x1, h3, idx, gates = _dn_out(o, dn_out_w[0], x, mods[0], ln1_g[0], ln1_b[0], router_w[0], router_b[0])
    x2 = _moe_block(x1, h3, idx, gates, mods[0], 0, e_w1, e_b1, e_w2, e_b2, ln2_g[0], ln2_b[0], seq)

    x2 = x2.reshape(bsz, seq, d)
    u = _cf_pw1(x2, mods[1], cf_pw1_w[0], cf_pw1_b[0])
    x3, h3, idx, gates = _cf_tail(u, cf_dw_w[0], cf_dw_b[0], cf_ln_g[0], cf_ln_b[0], cf_pw2_w[0], cf_pw2_b[0],
                                  x2, mods[1], ln1_g[1], ln1_b[1], router_w[1], router_b[1])
    x4 = _moe_block(x3, h3, idx, gates, mods[1], 1, e_w1, e_b1, e_w2, e_b2, ln2_g[1], ln2_b[1], seq)
    return x4.reshape(bsz, seq, d)
```

```python
import functools

import jax
import jax.numpy as jnp
from jax import lax
from jax.experimental import pallas as pl
from jax.experimental.pallas import tpu as pltpu

F32 = jnp.float32
BF16 = jnp.bfloat16
HIGHEST = lax.Precision.HIGHEST

DEPTH = 2
DN_QK_HEADS = 8
DN_V_HEADS = 16
DN_HEAD_DIM = 128
DN_CONV = 4
DN_CHUNK = 64
CF_KERNEL = 31
N_EXPERTS = 32
TOP_K = 4
SWIGLU_LIMIT = 7.0
SWIGLU_ALPHA = 1.702
LN_EPS = 1e-5
RMS_EPS = 1e-6
L2_EPS = 1e-6
DEEPNORM_ALPHA = (2 * DEPTH) ** 0.25

LANES = 128
SUBLANES = 8
BF16_SUBLANES = 16
VMEM_LIMIT = 56 * 1024 * 1024

ADA_TN = 1536
PROJ_TM = 1024
PROJ_TN = 2048
PREP_TS = 512
CHUNKS_PER_STEP = 4
DN_CHUNK_UNROLL = 2
POST_TM = 512
RANK_TT = 512
DISPATCH_TT = 512
EXPERT_TM = 512
COMBINE_TT = 256
CF_TM = 512
CF_HALO = 32
CF_CONV_ROWS = 32

DEST_SLOT_PAD = 1
NEG_BIG = -1e30


def _cparams(sem):
    return pltpu.CompilerParams(dimension_semantics=sem, vmem_limit_bytes=VMEM_LIMIT)


def _sigmoid(x):
    return jax.nn.sigmoid(x)


def _layer_norm(v, g, b):
    mu = jnp.mean(v, -1, keepdims=True)
    d = v - mu
    var = jnp.mean(d * d, -1, keepdims=True)
    return d * lax.rsqrt(var + LN_EPS) * g + b


ROW_TILE = SUBLANES


def _rows_to_tiles(o_ref, val):
    rows = val.shape[0]
    for s in range(ROW_TILE):
        o_ref[pl.ds(s, rows, stride=ROW_TILE), :] = val[:, s * LANES:(s + 1) * LANES]


def _tiles_to_rows(x_ref):
    rows = x_ref.shape[0] // ROW_TILE
    return jnp.concatenate([x_ref[pl.ds(s, rows, stride=ROW_TILE), :] for s in range(ROW_TILE)], axis=1)


def _split_weight(w):
    k, n = w.shape
    hi = w.astype(BF16)
    lo = (w - hi.astype(F32)).astype(BF16)
    out = jnp.zeros((k, 2 * LANES), BF16)
    return out.at[:, :n].set(hi).at[:, LANES:LANES + n].set(lo)


def _narrow_dot(h, w_split_ref):
    m = h.shape[0]
    hi = h.astype(BF16)
    lo = (h - hi.astype(F32)).astype(BF16)
    out = jnp.dot(jnp.concatenate([hi, lo], axis=0), w_split_ref[...], preferred_element_type=F32)
    return out[:m, :LANES] + out[:m, LANES:] + out[m:, :LANES]


def _token_tile(ref, tok):
    return ref.at[pl.ds(pl.multiple_of(tok * ROW_TILE, ROW_TILE), ROW_TILE), :]


def _ada_kernel(c_ref, w_ref, b_ref, o_ref):
    c = c_ref[...]
    cond = c * _sigmoid(c)
    o_ref[0] = jnp.dot(cond, w_ref[0], precision=HIGHEST, preferred_element_type=F32) + b_ref[0]


def _ada_ln(c, ada_w, ada_b):
    depth, d, n = ada_w.shape
    bsz = c.shape[0]
    c_pad = jnp.zeros((SUBLANES, d), F32).at[:bsz].set(c)
    out = pl.pallas_call(
        _ada_kernel,
        grid=(depth, n // ADA_TN),
        in_specs=[
            pl.BlockSpec((SUBLANES, d), lambda i, j: (0, 0)),
            pl.BlockSpec((1, d, ADA_TN), lambda i, j: (i, 0, j)),
            pl.BlockSpec((1, 1, ADA_TN), lambda i, j: (i, 0, j)),
        ],
        out_specs=pl.BlockSpec((1, SUBLANES, ADA_TN), lambda i, j: (i, 0, j)),
        out_shape=jax.ShapeDtypeStruct((depth, SUBLANES, n), F32),
        compiler_params=_cparams(("parallel", "parallel")),
        name="ada_ln",
    )(c_pad, ada_w, ada_b.reshape(depth, 1, n))
    return out[:, :bsz].reshape(depth, bsz, 6, d)


def _inproj_kernel(x_ref, mod_ref, w_ref, wba_ref, proj_ref, ba_ref, h_scr):
    @pl.when(pl.program_id(2) == 0)
    def _():
        h = x_ref[0] * (1.0 + mod_ref[0, 1:2, :]) + mod_ref[0, 0:1, :]
        h_scr[...] = h.astype(BF16)
        ba_ref[0] = _narrow_dot(h, wba_ref)

    proj_ref[0] = jnp.dot(h_scr[...], w_ref[...], preferred_element_type=F32).astype(BF16)


def _dn_in_proj(x, mod, in_w):
    bsz, seq, d = x.shape
    n_main = in_w.shape[1] - 2 * DN_V_HEADS
    w_main = in_w[:, :n_main].astype(BF16)
    w_ba = _split_weight(in_w[:, n_main:])
    tm = min(PROJ_TM, seq)
    return pl.pallas_call(
        _inproj_kernel,
        grid=(bsz, seq // tm, n_main // PROJ_TN),
        in_specs=[
            pl.BlockSpec((1, tm, d), lambda b, i, j: (b, i, 0)),
            pl.BlockSpec((1, 6, d), lambda b, i, j: (b, 0, 0)),
            pl.BlockSpec((d, PROJ_TN), lambda b, i, j: (0, j)),
            pl.BlockSpec((d, 2 * LANES), lambda b, i, j: (0, 0)),
        ],
        out_specs=[
            pl.BlockSpec((1, tm, PROJ_TN), lambda b, i, j: (b, i, j)),
            pl.BlockSpec((1, tm, LANES), lambda b, i, j: (b, i, 0)),
        ],
        out_shape=[
            jax.ShapeDtypeStruct((bsz, seq, n_main), BF16),
            jax.ShapeDtypeStruct((bsz, seq, LANES), F32),
        ],
        scratch_shapes=[pltpu.VMEM((tm, d), BF16)],
        compiler_params=_cparams(("parallel", "parallel", "arbitrary")),
        name="dn_in_proj",
    )(x, mod, w_main, w_ba)


def _dn_prep_kernel(q_ref, k_ref, v_ref, qh_ref, kh_ref, vh_ref, cw_ref, ba_ref, alog_ref, dt_ref,
                    qo_ref, ko_ref, vo_ref, g_ref, scr):
    ts = q_ref.shape[1]
    halo = qh_ref.shape[1]
    keep = (pl.program_id(1) > 0).astype(F32)

    def conv_silu(x_ref, h_ref, c0):
        width = x_ref.shape[2]
        scr[0:halo, 0:width] = h_ref[0].astype(F32) * keep
        scr[halo:halo + ts, 0:width] = x_ref[0].astype(F32)
        acc = None
        for j in range(DN_CONV):
            off = halo - (DN_CONV - 1) + j
            term = scr[off:off + ts, 0:width] * cw_ref[j:j + 1, c0:c0 + width]
            acc = term if acc is None else acc + term
        return acc * _sigmoid(acc)

    def l2norm_store(o_ref, x, scale):
        for h in range(x.shape[1] // DN_HEAD_DIM):
            xh = x[:, h * DN_HEAD_DIM:(h + 1) * DN_HEAD_DIM]
            ss = jnp.sum(xh * xh, -1, keepdims=True)
            o_ref[0, :, h * DN_HEAD_DIM:(h + 1) * DN_HEAD_DIM] = (
                xh * lax.rsqrt(ss + L2_EPS) * scale).astype(o_ref.dtype)

    kd = q_ref.shape[2]
    l2norm_store(qo_ref, conv_silu(q_ref, qh_ref, 0), DN_HEAD_DIM ** -0.5)
    l2norm_store(ko_ref, conv_silu(k_ref, kh_ref, kd), 1.0)
    vo_ref[0] = conv_silu(v_ref, vh_ref, 2 * kd).astype(vo_ref.dtype)

    ba = ba_ref[0]
    beta = _sigmoid(ba)
    zz = ba + dt_ref[...]
    softplus = jnp.maximum(zz, 0.0) + jnp.log1p(jnp.exp(-jnp.abs(zz)))
    g = -jnp.exp(alog_ref[...]) * softplus
    r = lax.broadcasted_iota(jnp.int32, (ts, ts), 0)
    c = lax.broadcasted_iota(jnp.int32, (ts, ts), 1)
    in_chunk_tril = ((r // DN_CHUNK == c // DN_CHUNK) & (c <= r)).astype(BF16)
    g1 = g.astype(BF16)
    g2 = (g - g1.astype(F32)).astype(BF16)
    g3 = (g - g1.astype(F32) - g2.astype(F32)).astype(BF16)
    parts = jnp.dot(in_chunk_tril, jnp.concatenate([g1, g2, g3], axis=1), preferred_element_type=F32)
    gcum = parts[:, :LANES] + parts[:, LANES:2 * LANES] + parts[:, 2 * LANES:]
    lane = lax.broadcasted_iota(jnp.int32, ba.shape, 1)
    g_ref[0] = jnp.where(lane < DN_V_HEADS, beta, gcum)


def _dn_prep(proj, ba, conv_w, a_log, dt_bias):
    bsz, seq, _ = proj.shape
    kd = DN_QK_HEADS * DN_HEAD_DIM
    vd = DN_V_HEADS * DN_HEAD_DIM
    ts = min(PREP_TS, seq)
    halo = BF16_SUBLANES
    hb = ts // halo
    alog_row = jnp.zeros((1, LANES), F32).at[0, DN_V_HEADS:2 * DN_V_HEADS].set(a_log)
    dt_row = jnp.zeros((1, LANES), F32).at[0, DN_V_HEADS:2 * DN_V_HEADS].set(dt_bias)

    def halo_map(col):
        return lambda b, i: (b, jnp.maximum(i * hb - 1, 0), col)

    return pl.pallas_call(
        _dn_prep_kernel,
        grid=(bsz, seq // ts),
        in_specs=[
            pl.BlockSpec((1, ts, kd), lambda b, i: (b, i, 0)),
            pl.BlockSpec((1, ts, kd), lambda b, i: (b, i, 1)),
            pl.BlockSpec((1, ts, vd), lambda b, i: (b, i, 1)),
            pl.BlockSpec((1, halo, kd), halo_map(0)),
            pl.BlockSpec((1, halo, kd), halo_map(1)),
            pl.BlockSpec((1, halo, vd), halo_map(1)),
            pl.BlockSpec((DN_CONV, 2 * kd + vd), lambda b, i: (0, 0)),
            pl.BlockSpec((1, ts, LANES), lambda b, i: (b, i, 0)),
            pl.BlockSpec((1, LANES), lambda b, i: (0, 0)),
            pl.BlockSpec((1, LANES), lambda b, i: (0, 0)),
        ],
        out_specs=[
            pl.BlockSpec((1, ts, kd), lambda b, i: (b, i, 0)),
            pl.BlockSpec((1, ts, kd), lambda b, i: (b, i, 0)),
            pl.BlockSpec((1, ts, vd), lambda b, i: (b, i, 0)),
            pl.BlockSpec((1, ts, LANES), lambda b, i: (b, i, 0)),
        ],
        out_shape=[
            jax.ShapeDtypeStruct((bsz, seq, kd), BF16),
            jax.ShapeDtypeStruct((bsz, seq, kd), BF16),
            jax.ShapeDtypeStruct((bsz, seq, vd), BF16),
            jax.ShapeDtypeStruct((bsz, seq, LANES), F32),
        ],
        scratch_shapes=[pltpu.VMEM((ts + halo, vd), F32)],
        compiler_params=_cparams(("parallel", "arbitrary")),
        name="dn_prep",
    )(proj, proj, proj, proj, proj, proj, conv_w, ba, alog_row, dt_row)


def _bmm(a, b):
    return lax.dot_general(a.astype(BF16), b.astype(BF16), (((2,), (1,)), ((0,), (0,))),
                           preferred_element_type=F32)


def _bmm_nt(a, b):
    return lax.dot_general(a.astype(BF16), b.astype(BF16), (((2,), (2,)), ((0,), (0,))),
                           preferred_element_type=F32)


def _bmm_tn(a, b):
    return lax.dot_general(a.astype(BF16), b.astype(BF16), (((1,), (1,)), ((0,), (0,))),
                           preferred_element_type=F32)


def _unit_lower_inverse_wide(a_twice, upper, eye_upper):
    c = a_twice.shape[1]
    x = jnp.where(upper, eye_upper, -a_twice)
    span = 1
    while span < c:
        x = _bmm(x[:, :, :c], x) + jnp.where(upper, x, 0.0)
        span *= 2
    return x


def _dn_chunk_kernel(q_ref, k_ref, v_ref, z_ref, g_ref, gt_ref, ow_ref, o_ref, s_ref):
    c_len = DN_CHUNK
    dh = DN_HEAD_DIM
    rep = DN_V_HEADS // DN_QK_HEADS

    @pl.when(pl.program_id(1) == 0)
    def _():
        s_ref[...] = jnp.zeros(s_ref.shape, F32)

    heads = range(DN_V_HEADS)
    n_heads = DN_V_HEADS
    ri = lax.broadcasted_iota(jnp.int32, (1, c_len, 2 * c_len), 1)
    lane = lax.broadcasted_iota(jnp.int32, (1, c_len, 2 * c_len), 2)
    upper = lane >= c_len
    ci = jnp.where(upper, lane - c_len, lane)
    causal = ci <= ri
    strict = ci < ri
    eye_upper = ((ci == ri) & upper).astype(F32)
    onorm = ow_ref[...]

    def head_cols(ref, rows, h):
        return ref[0, rows, h * dh:(h + 1) * dh]

    def per_v_head(t):
        return jnp.stack([t[h // rep] for h in heads])

    def state_free_part(c):
        r0 = pl.multiple_of(c * c_len, c_len)
        rows = pl.ds(r0, c_len)
        gcols = g_ref[0, rows, :]
        grows = gt_ref[0, c]
        wide = (n_heads, c_len, dh)
        beta_b = jnp.broadcast_to(jnp.stack([gcols[:, h:h + 1] for h in heads]), wide)
        g_b = jnp.broadcast_to(
            jnp.stack([gcols[:, DN_V_HEADS + h:DN_V_HEADS + h + 1] for h in heads]), wide)
        g_r = jnp.stack([grows[DN_V_HEADS + h:DN_V_HEADS + h + 1, :] for h in heads])
        g_r = jnp.concatenate([g_r, g_r], axis=2)
        g_last = g_b[:, c_len - 1:c_len, :]

        qn = jnp.stack([head_cols(q_ref, rows, hq) for hq in range(DN_QK_HEADS)])
        kn = jnp.stack([head_cols(k_ref, rows, hq) for hq in range(DN_QK_HEADS)])
        qk_kk = _bmm_nt(jnp.concatenate([qn, kn], axis=1), jnp.concatenate([kn, kn], axis=1))
        qk = per_v_head(qk_kk[:, :c_len])
        kk = per_v_head(qk_kk[:, c_len:])
        qf = per_v_head(qn).astype(F32)
        kf = per_v_head(kn).astype(F32)
        vf = jnp.stack([head_cols(v_ref, rows, h) for h in heads]).astype(F32)

        decay = jnp.where(causal, jnp.exp(g_b - g_r), 0.0)
        a_twice = jnp.where(strict, beta_b * kk * decay, 0.0)
        x_inv = _unit_lower_inverse_wide(a_twice, upper, eye_upper)
        eg = jnp.exp(g_b)
        rhs = jnp.concatenate([vf * beta_b, kf * (beta_b * eg)], axis=2)
        sol = _bmm(x_inv, jnp.concatenate([jnp.zeros_like(rhs), rhs], axis=1))
        u = sol[:, :, :dh]
        w = sol[:, :, dh:]
        qkm = jnp.where(causal, qk * decay, 0.0)[:, :, :c_len]
        qg = qf * eg
        kdec = kf * jnp.exp(g_last - g_b)
        return rows, jnp.concatenate([w, qg], axis=1), u, qkm, kdec, jnp.exp(g_last)

    def state_part(rows, w_qg, u, qkm, kdec, chunk_decay):
        state = s_ref[...]
        ws = _bmm(w_qg, state)
        v_new = u - ws[:, :c_len]
        o = ws[:, c_len:] + _bmm(qkm, v_new)
        s_ref[...] = state * chunk_decay + _bmm_tn(kdec, v_new)

        o = o * lax.rsqrt(jnp.mean(o * o, -1, keepdims=True) + RMS_EPS) * onorm
        for h in heads:
            zf = head_cols(z_ref, rows, h).astype(F32)
            o_ref[0, rows, h * dh:(h + 1) * dh] = (o[h] * (zf * _sigmoid(zf))).astype(o_ref.dtype)

    def chunk_group(i, carry):
        prepared = [state_free_part(i * DN_CHUNK_UNROLL + j) for j in range(DN_CHUNK_UNROLL)]
        for args in prepared:
            state_part(*args)
        return carry

    lax.fori_loop(0, q_ref.shape[1] // (c_len * DN_CHUNK_UNROLL), chunk_group, 0)


def _dn_chunk(qn, kn, vv, proj, gates, onorm_w):
    bsz, seq, kd = qn.shape
    vd = vv.shape[2]
    n_chunks = seq // DN_CHUNK
    cb = min(CHUNKS_PER_STEP, n_chunks)
    rows = cb * DN_CHUNK
    gates_t = jnp.swapaxes(gates[:, :, :2 * DN_V_HEADS].reshape(bsz, n_chunks, DN_CHUNK, 2 * DN_V_HEADS), 2, 3)
    z_col = (2 * kd + vd) // vd
    return pl.pallas_call(
        _dn_chunk_kernel,
        grid=(bsz, n_chunks // cb),
        in_specs=[
            pl.BlockSpec((1, rows, kd), lambda b, n: (b, n, 0)),
            pl.BlockSpec((1, rows, kd), lambda b, n: (b, n, 0)),
            pl.BlockSpec((1, rows, vd), lambda b, n: (b, n, 0)),
            pl.BlockSpec((1, rows, vd), lambda b, n: (b, n, z_col)),
            pl.BlockSpec((1, rows, LANES), lambda b, n: (b, n, 0)),
            pl.BlockSpec((1, cb, 2 * DN_V_HEADS, DN_CHUNK), lambda b, n: (b, n, 0, 0)),
            pl.BlockSpec((1, DN_HEAD_DIM), lambda b, n: (0, 0)),
        ],
        out_specs=pl.BlockSpec((1, rows, vd), lambda b, n: (b, n, 0)),
        out_shape=jax.ShapeDtypeStruct((bsz, seq, vd), BF16),
        scratch_shapes=[pltpu.VMEM((DN_V_HEADS, DN_HEAD_DIM, DN_HEAD_DIM), F32)],
        compiler_params=_cparams(("parallel", "arbitrary")),
        name="dn_chunk",
    )(qn, kn, vv, proj, gates, gates_t, onorm_w.reshape(1, DN_HEAD_DIM))


def _residual_ln(x, y, gate_row, ln_g, ln_b):
    return _layer_norm(DEEPNORM_ALPHA * x + (1.0 + gate_row) * y, ln_g, ln_b)


def _route_store(h, rw_ref, rb_ref, idx_ref, gate_ref):
    logits = _narrow_dot(h, rw_ref) + rb_ref[...]
    lane = lax.broadcasted_iota(jnp.int32, logits.shape, 1).astype(F32)
    work = logits
    idx_out = jnp.zeros(logits.shape, F32)
    val_out = jnp.full(logits.shape, NEG_BIG, F32)
    for k in range(TOP_K):
        m = jnp.max(work, -1, keepdims=True)
        am = jnp.min(jnp.where(work == m, lane, float(LANES)), -1, keepdims=True)
        idx_out = jnp.where(lane == k, am, idx_out)
        val_out = jnp.where(lane == k, m, val_out)
        work = jnp.where(lane == am, NEG_BIG * 2.0, work)
    top = jnp.max(val_out, -1, keepdims=True)
    e = jnp.where(lane < TOP_K, jnp.exp(val_out - top), 0.0)
    idx_ref[...] = idx_out.astype(jnp.int32)
    gate_ref[...] = e / jnp.sum(e, -1, keepdims=True)


def _post_mixer_tail(x, y, mod_ref, lng_ref, lnb_ref, rw_ref, rb_ref, x1_ref, h3_ref, idx_ref, gate_ref):
    x1 = _residual_ln(x, y, mod_ref[0, 2:3, :], lng_ref[...], lnb_ref[...])
    x1_ref[...] = x1
    h2 = x1 * (1.0 + mod_ref[0, 4:5, :]) + mod_ref[0, 3:4, :]
    _rows_to_tiles(h3_ref, h2)
    _route_store(h2, rw_ref, rb_ref, idx_ref, gate_ref)


def _router_operands(router_w, router_b):
    rb = jnp.full((1, LANES), NEG_BIG, F32).at[0, :N_EXPERTS].set(router_b)
    return _split_weight(router_w), rb


def _post_out_specs(tm, d):
    return [
        pl.BlockSpec((tm, d), lambda i: (i, 0)),
        pl.BlockSpec((tm * ROW_TILE, LANES), lambda i: (i, 0)),
        pl.BlockSpec((tm, LANES), lambda i: (i, 0)),
        pl.BlockSpec((tm, LANES), lambda i: (i, 0)),
    ]


def _post_out_shapes(t, d):
    assert d == ROW_TILE * LANES
    return [
        jax.ShapeDtypeStruct((t, d), F32),
        jax.ShapeDtypeStruct((t * ROW_TILE, LANES), F32),
        jax.ShapeDtypeStruct((t, LANES), jnp.int32),
        jax.ShapeDtypeStruct((t, LANES), F32),
    ]


def _dn_out_kernel(o_ref, w_ref, x_ref, mod_ref, lng_ref, lnb_ref, rw_ref, rb_ref,
                   x1_ref, h3_ref, idx_ref, gate_ref):
    y = jnp.dot(o_ref[...], w_ref[...], preferred_element_type=F32)
    _post_mixer_tail(x_ref[...], y, mod_ref, lng_ref, lnb_ref, rw_ref, rb_ref,
                     x1_ref, h3_ref, idx_ref, gate_ref)


def _dn_out(o, out_w, x, mod, ln_g, ln_b, router_w, router_b):
    bsz, seq, d = x.shape
    t = bsz * seq
    vd = o.shape[2]
    tm = min(POST_TM, seq)
    rw, rb = _router_operands(router_w, router_b)
    per_b = seq // tm
    return pl.pallas_call(
        _dn_out_kernel,
        grid=(t // tm,),
        in_specs=[
            pl.BlockSpec((tm, vd), lambda i: (i, 0)),
            pl.BlockSpec((vd, d), lambda i: (0, 0)),
            pl.BlockSpec((tm, d), lambda i: (i, 0)),
            pl.BlockSpec((1, 6, d), lambda i: (i // per_b, 0, 0)),
            pl.BlockSpec((1, d), lambda i: (0, 0)),
            pl.BlockSpec((1, d), lambda i: (0, 0)),
            pl.BlockSpec((d, 2 * LANES), lambda i: (0, 0)),
            pl.BlockSpec((1, LANES), lambda i: (0, 0)),
        ],
        out_specs=_post_out_specs(tm, d),
        out_shape=_post_out_shapes(t, d),
        compiler_params=_cparams(("parallel",)),
        name="dn_out_ln_route",
    )(o.reshape(t, vd), out_w.astype(BF16), x.reshape(t, d), mod, ln_g.reshape(1, d), ln_b.reshape(1, d), rw, rb)


def _pw1_glu_store(x, mod_ref, wa_ref, wb_ref, ba_ref, bb_ref, u_ref):
    h = (x * (1.0 + mod_ref[0, 1:2, :]) + mod_ref[0, 0:1, :]).astype(BF16)
    pa = jnp.dot(h, wa_ref[...], preferred_element_type=F32) + ba_ref[...]
    pb = jnp.dot(h, wb_ref[...], preferred_element_type=F32) + bb_ref[...]
    _rows_to_tiles(u_ref, pa * _sigmoid(pb))


def _cf_tail_kernel(u_ref, uh_ref, dw_ref, dwb_ref, cg_ref, cb_ref, w2_ref, b2_ref,
                    x_ref, mod_ref, lng_ref, lnb_ref, rw_ref, rb_ref,
                    x1_ref, h3_ref, idx_ref, gate_ref, scr, conv_scr):
    tm = u_ref.shape[0] // ROW_TILE
    halo = uh_ref.shape[0] // ROW_TILE
    keep = (pl.program_id(1) > 0).astype(F32)
    scr[0:halo * ROW_TILE, :] = uh_ref[...] * keep
    scr[halo * ROW_TILE:(halo + tm) * ROW_TILE, :] = u_ref[...]
    first = halo - (CF_KERNEL - 1)
    blk = CF_CONV_ROWS * ROW_TILE

    def conv_tokens(tb, carry):
        r0 = pl.multiple_of(tb * blk, blk)
        acc = jnp.concatenate([dwb_ref[...]] * CF_CONV_ROWS, axis=0)
        for j in range(CF_KERNEL):
            w_tile = dw_ref[j * ROW_TILE:(j + 1) * ROW_TILE, :]
            w_blk = jnp.concatenate([w_tile] * CF_CONV_ROWS, axis=0)
            acc = acc + scr[pl.ds(r0 + (first + j) * ROW_TILE, blk), :] * w_blk
        conv_scr[pl.ds(r0, blk), :] = acc
        return carry

    lax.fori_loop(0, tm // CF_CONV_ROWS, conv_tokens, 0)
    conv = _tiles_to_rows(conv_scr)
    normed = _layer_norm(conv, cg_ref[...], cb_ref[...])
    act = normed * _sigmoid(normed)
    y = jnp.dot(act.astype(BF16), w2_ref[...], preferred_element_type=F32) + b2_ref[...]
    _post_mixer_tail(x_ref[0], y, mod_ref, lng_ref, lnb_ref, rw_ref, rb_ref,
                     x1_ref, h3_ref, idx_ref, gate_ref)


def _cf_tail(u, dw_w, dw_b, cf_ln_g, cf_ln_b, pw2_w, pw2_b, x, mod, ln_g, ln_b, router_w, router_b):
    bsz, seq, d = x.shape
    inner = dw_w.shape[1]
    t = bsz * seq
    tm = min(CF_TM, seq)
    per_b = seq // tm
    hb = tm // CF_HALO
    halos_per_b = seq // CF_HALO
    rw, rb = _router_operands(router_w, router_b)
    row = lambda v: v.reshape(1, -1)
    tiles = lambda v: v.reshape(-1, LANES)
    const = lambda b, i: (0, 0)
    flat = lambda b, i: (b * per_b + i, 0)
    return pl.pallas_call(
        _cf_tail_kernel,
        grid=(bsz, per_b),
        in_specs=[
            pl.BlockSpec((tm * ROW_TILE, LANES), flat),
            pl.BlockSpec((CF_HALO * ROW_TILE, LANES),
                         lambda b, i: (b * halos_per_b + jnp.maximum(i * hb - 1, 0), 0)),
            pl.BlockSpec((CF_KERNEL * ROW_TILE, LANES), const),
            pl.BlockSpec((ROW_TILE, LANES), const),
            pl.BlockSpec((1, inner), const),
            pl.BlockSpec((1, inner), const),
            pl.BlockSpec((inner, d), const),
            pl.BlockSpec((1, d), const),
            pl.BlockSpec((1, tm, d), lambda b, i: (b, i, 0)),
            pl.BlockSpec((1, 6, d), lambda b, i: (b, 0, 0)),
            pl.BlockSpec((1, d), const),
            pl.BlockSpec((1, d), const),
            pl.BlockSpec((d, 2 * LANES), const),
            pl.BlockSpec((1, LANES), const),
        ],
        out_specs=[
            pl.BlockSpec((tm, d), flat),
            pl.BlockSpec((tm * ROW_TILE, LANES), flat),
            pl.BlockSpec((tm, LANES), flat),
            pl.BlockSpec((tm, LANES), flat),
        ],
        out_shape=_post_out_shapes(t, d),
        scratch_shapes=[pltpu.VMEM(((tm + CF_HALO) * ROW_TILE, LANES), F32),
                        pltpu.VMEM((tm * ROW_TILE, LANES), F32)],
        compiler_params=_cparams(("parallel", "arbitrary")),
        name="cf_conv_ln_pw2_route",
    )(u, u, tiles(dw_w), tiles(dw_b), row(cf_ln_g), row(cf_ln_b), pw2_w.astype(BF16), row(pw2_b),
      x, mod, row(ln_g), row(ln_b), rw, rb)


def _rank_kernel(idx_ref, rank_t_ref, idx_t_ref, cnt_ref, carry):
    tt = idx_ref.shape[0]

    @pl.when(pl.program_id(0) == 0)
    def _():
        carry[...] = jnp.zeros(carry.shape, F32)

    idx = idx_ref[...]
    lane = lax.broadcasted_iota(jnp.int32, idx.shape, 1)
    sel = [lane == idx[:, k:k + 1] for k in range(TOP_K)]
    multi_hot = sel[0]
    for k in range(1, TOP_K):
        multi_hot = multi_hot | sel[k]
    mh = multi_hot.astype(BF16)
    r = lax.broadcasted_iota(jnp.int32, (tt, tt), 0)
    c = lax.broadcasted_iota(jnp.int32, (tt, tt), 1)
    before = jnp.dot((c < r).astype(BF16), mh, preferred_element_type=F32) + carry[...]
    rank = jnp.zeros(idx.shape, F32)
    for k in range(TOP_K):
        rk = jnp.sum(jnp.where(sel[k], before, 0.0), -1, keepdims=True)
        rank = jnp.where(lane == k, rk, rank)
    rank_t_ref[...] = jnp.transpose(rank)[:SUBLANES].astype(jnp.int32)
    idx_t_ref[...] = jnp.transpose(idx.astype(F32))[:SUBLANES].astype(jnp.int32)
    total = carry[...] + jnp.sum(mh.astype(F32), 0, keepdims=True)
    carry[...] = total
    cnt_ref[...] = total.astype(jnp.int32)


def _rank_and_count(idx):
    t = idx.shape[0]
    tt = min(RANK_TT, t)
    return pl.pallas_call(
        _rank_kernel,
        grid=(t // tt,),
        in_specs=[pl.BlockSpec((tt, LANES), lambda i: (i, 0))],
        out_specs=[
            pl.BlockSpec((SUBLANES, tt), lambda i: (0, i)),
            pl.BlockSpec((SUBLANES, tt), lambda i: (0, i)),
            pl.BlockSpec((1, LANES), lambda i: (0, 0)),
        ],
        out_shape=[
            jax.ShapeDtypeStruct((SUBLANES, t), jnp.int32),
            jax.ShapeDtypeStruct((SUBLANES, t), jnp.int32),
            jax.ShapeDtypeStruct((1, LANES), jnp.int32),
        ],
        scratch_shapes=[pltpu.VMEM((1, LANES), F32)],
        compiler_params=_cparams(("arbitrary",)),
        name="moe_rank_count",
    )(idx)


def _dest_kernel(pstart_ref, idx_t_ref, rank_t_ref, dest_t_ref):
    idx = idx_t_ref[...]
    base = jnp.zeros(idx.shape, jnp.int32)
    for e in range(N_EXPERTS):
        base = jnp.where(idx == e, pstart_ref[e], base)
    dest_t_ref[...] = base + rank_t_ref[...]


def _dest_rows(pstarts, idx_t, rank_t):
    t = idx_t.shape[1]
    dest_t = pl.pallas_call(
        _dest_kernel,
        grid_spec=pltpu.PrefetchScalarGridSpec(
            num_scalar_prefetch=1,
            grid=(1,),
            in_specs=[pl.BlockSpec((SUBLANES, t), lambda i, ps: (0, 0)),
                      pl.BlockSpec((SUBLANES, t), lambda i, ps: (0, 0))],
            out_specs=pl.BlockSpec((SUBLANES, t), lambda i, ps: (0, 0)),
        ),
        out_shape=jax.ShapeDtypeStruct((SUBLANES, t), jnp.int32),
        compiler_params=_cparams(("arbitrary",)),
        name="moe_dest_rows",
    )(pstarts, idx_t, rank_t)
    return jnp.pad(dest_t[:TOP_K], ((0, 0), (0, DEST_SLOT_PAD))).reshape(TOP_K * (t + DEST_SLOT_PAD))


def _num_expert_blocks(n_assign):
    bound = n_assign + N_EXPERTS * (EXPERT_TM - 1)
    return -(-bound // EXPERT_TM)


def _routing_tables(n_tokens, counts):
    nb = _num_expert_blocks(n_tokens * TOP_K)
    counts = counts[0, :N_EXPERTS]
    padded = (counts + EXPERT_TM - 1) // EXPERT_TM * EXPERT_TM
    pends = jnp.cumsum(padded)
    pstarts = (pends - padded).astype(jnp.int32)
    block_row0 = jnp.arange(nb, dtype=jnp.int32) * EXPERT_TM
    segments_done = jnp.sum((pends[None, :] <= block_row0[:, None]).astype(jnp.int32), axis=1)
    block_e = jnp.minimum(segments_done, N_EXPERTS - 1)
    r0 = block_row0[:, None]
    in_segment = (pstarts[None, :] <= r0) & (r0 < pends[None, :])
    live_if = jnp.clip(pstarts[None, :] + counts[None, :] - r0, 0, EXPERT_TM)
    live = jnp.sum(jnp.where(in_segment, live_if, 0), axis=1).astype(jnp.int32)
    return pstarts, block_e, live


def _expert_run_tables(block_e):
    nb = block_e.shape[0]
    pos = jnp.arange(nb, dtype=jnp.int32)
    first = jnp.concatenate([jnp.ones((1,), jnp.int32), (block_e[1:] != block_e[:-1]).astype(jnp.int32)])
    slot = (jnp.cumsum(first) - 1) % 2
    start_at_or_after = lax.cummin(jnp.where(first == 1, pos, nb), reverse=True)
    next_start = jnp.concatenate([start_at_or_after[1:], jnp.full((1,), nb, jnp.int32)])
    picks = next_start[:, None] == pos[None, :]
    next_e = jnp.where(next_start < nb, jnp.sum(jnp.where(picks, block_e[None, :], 0), axis=1), -1)
    return first, slot.astype(jnp.int32), next_e.astype(jnp.int32)


def _pad_fill_copies(live_ref, i, zeros, xb_hbm, sem):
    lv = live_ref[i]
    pad = EXPERT_TM - lv
    row0 = i * EXPERT_TM + lv
    copies = []
    piece = EXPERT_TM
    while piece >= 1:
        offset = pad & ~(2 * piece - 1)
        start = pl.multiple_of((row0 + offset) * ROW_TILE, ROW_TILE)
        copy = pltpu.make_async_copy(zeros.at[pl.ds(0, piece * ROW_TILE), :],
                                     xb_hbm.at[pl.ds(start, piece * ROW_TILE), :], sem)
        copies.append(((pad & piece) != 0, copy))
        piece //= 2
    return copies


def _dispatch_kernel(dest_ref, live_ref, h_ref, xb_hbm, zeros, sem, fill_sem):
    tt = h_ref.shape[0] // ROW_TILE
    base = pl.program_id(0) * tt
    slot_stride = pl.num_programs(0) * tt + DEST_SLOT_PAD
    n_blocks = xb_hbm.shape[0] // (EXPERT_TM * ROW_TILE)

    @pl.when(pl.program_id(0) == 0)
    def _():
        zeros[...] = jnp.zeros(zeros.shape, F32)

        def start_fill(i, carry):
            for live_bit, copy in _pad_fill_copies(live_ref, i, zeros, xb_hbm, fill_sem):
                pl.when(live_bit)(copy.start)
            return carry

        lax.fori_loop(0, n_blocks, start_fill, 0)

    def issue(t, carry):
        for k in range(TOP_K):
            pltpu.make_async_copy(_token_tile(h_ref, t),
                                  _token_tile(xb_hbm, dest_ref[k * slot_stride + base + t]), sem).start()
        return carry

    lax.fori_loop(0, tt, issue, 0)
    for _ in range(TOP_K):
        pltpu.make_async_copy(h_ref, xb_hbm.at[pl.ds(0, tt * ROW_TILE), :], sem).wait()

    @pl.when(pl.program_id(0) == 0)
    def _():
        def wait_fill(i, carry):
            for live_bit, copy in _pad_fill_copies(live_ref, i, zeros, xb_hbm, fill_sem):
                pl.when(live_bit)(copy.wait)
            return carry

        lax.fori_loop(0, n_blocks, wait_fill, 0)


def _dispatch(dest, live, h3, n_rows):
    t = h3.shape[0] // ROW_TILE
    return pl.pallas_call(
        _dispatch_kernel,
        grid_spec=pltpu.PrefetchScalarGridSpec(
            num_scalar_prefetch=2,
            grid=(t // DISPATCH_TT,),
            in_specs=[pl.BlockSpec((DISPATCH_TT * ROW_TILE, LANES), lambda i, dst, lv: (i, 0))],
            out_specs=pl.BlockSpec(memory_space=pl.ANY),
            scratch_shapes=[pltpu.VMEM((EXPERT_TM * ROW_TILE, LANES), F32), pltpu.SemaphoreType.DMA,
                            pltpu.SemaphoreType.DMA],
        ),
        out_shape=jax.ShapeDtypeStruct((n_rows * ROW_TILE, LANES), F32),
        compiler_params=_cparams(("arbitrary",)),
        name="moe_dispatch",
    )(dest, live, h3)


def _expert_kernel(be_ref, live_ref, first_ref, slot_ref, next_ref, x_ref, w1_hbm, b1_ref, w2_hbm, b2_ref,
                   y_ref, w1f, w2f, w1s, w2s, sem, *, layer):
    i = pl.program_id(0)
    live = live_ref[i]

    def weight_copies(e, slot):
        return (pltpu.make_async_copy(w1_hbm.at[layer, e], w1f.at[slot], sem.at[0, slot]),
                pltpu.make_async_copy(w2_hbm.at[layer, e], w2f.at[slot], sem.at[1, slot]))

    @pl.when(i == 0)
    def _():
        for copy in weight_copies(be_ref[0], 0):
            copy.start()

    @pl.when(first_ref[i] == 1)
    def _():
        slot = slot_ref[i]
        for copy in weight_copies(be_ref[i], slot):
            copy.wait()
        nxt = next_ref[i]

        @pl.when(nxt >= 0)
        def _():
            for copy in weight_copies(nxt, 1 - slot):
                copy.start()

        w1s[...] = w1f[slot].astype(BF16)
        w2s[...] = w2f[slot].astype(BF16)

    @pl.when(live > 0)
    def _():
        dff = w2s.shape[0]
        x = _tiles_to_rows(x_ref)
        row = lax.broadcasted_iota(jnp.int32, (x.shape[0], 1), 0)
        x = jnp.where(row < live, x, 0.0).astype(BF16)
        hh = jnp.dot(x, w1s[...], preferred_element_type=F32) + b1_ref[0, 0]
        x_glu = jnp.minimum(hh[:, :dff], SWIGLU_LIMIT)
        x_lin = jnp.clip(hh[:, dff:], -SWIGLU_LIMIT, SWIGLU_LIMIT)
        act = x_glu * _sigmoid(SWIGLU_ALPHA * x_glu) * (x_lin + 1.0)
        y = jnp.dot(act.astype(BF16), w2s[...], preferred_element_type=F32) + b2_ref[0, 0]
        _rows_to_tiles(y_ref, y)

    @pl.when(live == 0)
    def _():
        y_ref[...] = jnp.zeros(y_ref.shape, F32)


def _experts(block_e, live, xb, layer, w1, b1, w2, b2):
    depth, n_e, d, two_f = w1.shape
    dff = two_f // 2
    block_rows = EXPERT_TM * ROW_TILE
    nb = xb.shape[0] // block_rows
    first, slot, next_e = _expert_run_tables(block_e)
    bias_map = lambda i, be, *_: (layer, be[i], 0, 0)
    return pl.pallas_call(
        functools.partial(_expert_kernel, layer=layer),
        grid_spec=pltpu.PrefetchScalarGridSpec(
            num_scalar_prefetch=5,
            grid=(nb,),
            in_specs=[
                pl.BlockSpec((block_rows, LANES), lambda i, *_: (i, 0)),
                pl.BlockSpec(memory_space=pl.ANY),
                pl.BlockSpec((1, 1, 1, two_f), bias_map),
                pl.BlockSpec(memory_space=pl.ANY),
                pl.BlockSpec((1, 1, 1, d), bias_map),
            ],
            out_specs=pl.BlockSpec((block_rows, LANES), lambda i, *_: (i, 0)),
            scratch_shapes=[pltpu.VMEM((2, d, two_f), F32), pltpu.VMEM((2, dff, d), F32),
                            pltpu.VMEM((d, two_f), BF16), pltpu.VMEM((dff, d), BF16),
                            pltpu.SemaphoreType.DMA((2, 2))],
        ),
        out_shape=jax.ShapeDtypeStruct(xb.shape, F32),
        compiler_params=_cparams(("arbitrary",)),
        name="moe_experts",
    )(block_e, live, first, slot, next_e, xb, w1, b1.reshape(depth, n_e, 1, two_f), w2,
      b2.reshape(depth, n_e, 1, d))


def _combine_kernel(dest_ref, yb_hbm, gate_ref, x_ref, mod_ref, lng_ref, lnb_ref, *rest, with_pw1):
    if with_pw1:
        modn_ref, wa_ref, wb_ref, ba_ref, bb_ref, o_ref, u_ref, buf, sem = rest
    else:
        o_ref, buf, sem = rest
    tt = x_ref.shape[0]
    step = pl.program_id(0)
    n_steps = pl.num_programs(0)
    slot_stride = n_steps * tt + DEST_SLOT_PAD

    def start_gather(tile, slot):
        def issue(t, carry):
            for k in range(TOP_K):
                src = _token_tile(yb_hbm, dest_ref[k * slot_stride + tile * tt + t])
                pltpu.make_async_copy(src, _token_tile(buf.at[slot, k], t), sem.at[slot]).start()
            return carry

        lax.fori_loop(0, tt, issue, 0)

    slot = step % 2

    @pl.when(step == 0)
    def _():
        start_gather(step, slot)

    @pl.when(step + 1 < n_steps)
    def _():
        start_gather(step + 1, 1 - slot)

    for k in range(TOP_K):
        pltpu.make_async_copy(yb_hbm.at[pl.ds(0, tt * ROW_TILE), :], buf.at[slot, k], sem.at[slot]).wait()

    gates = gate_ref[...]
    y = None
    for k in range(TOP_K):
        term = gates[:, k:k + 1] * _tiles_to_rows(buf.at[slot, k])
        y = term if y is None else y + term
    x_new = _residual_ln(x_ref[...], y, mod_ref[0, 5:6, :], lng_ref[...], lnb_ref[...])
    o_ref[...] = x_new
    if with_pw1:
        _pw1_glu_store(x_new, modn_ref, wa_ref, wb_ref, ba_ref, bb_ref, u_ref)


def _combine(dest, yb, gates, x1, mod, ln_g, ln_b, seq, next_pw1=None):
    t, d = x1.shape
    tt = min(COMBINE_TT, seq)
    per_b = seq // tt
    const = lambda i, dst: (0, 0)
    in_specs = [
        pl.BlockSpec(memory_space=pl.ANY),
        pl.BlockSpec((tt, LANES), lambda i, dst: (i, 0)),
        pl.BlockSpec((tt, d), lambda i, dst: (i, 0)),
        pl.BlockSpec((1, 6, d), lambda i, dst: (i // per_b, 0, 0)),
        pl.BlockSpec((1, d), const),
        pl.BlockSpec((1, d), const),
    ]
    operands = [dest, yb, gates, x1, mod, ln_g.reshape(1, d), ln_b.reshape(1, d)]
    out_specs = [pl.BlockSpec((tt, d), lambda i, dst: (i, 0))]
    out_shape = [jax.ShapeDtypeStruct((t, d), F32)]
    if next_pw1 is not None:
        mod_next, pw1_w, pw1_b = next_pw1
        inner = pw1_w.shape[1] // 2
        assert inner == ROW_TILE * LANES
        w = pw1_w.astype(BF16)
        b = pw1_b.reshape(1, 2 * inner)
        in_specs += [
            pl.BlockSpec((1, 6, d), lambda i, dst: (i // per_b, 0, 0)),
            pl.BlockSpec((d, inner), const),
            pl.BlockSpec((d, inner), lambda i, dst: (0, 1)),
            pl.BlockSpec((1, inner), const),
            pl.BlockSpec((1, inner), lambda i, dst: (0, 1)),
        ]
        operands += [mod_next, w, w, b, b]
        out_specs.append(pl.BlockSpec((tt * ROW_TILE, LANES), lambda i, dst: (i, 0)))
        out_shape.append(jax.ShapeDtypeStruct((t * ROW_TILE, LANES), F32))
    return pl.pallas_call(
        functools.partial(_combine_kernel, with_pw1=next_pw1 is not None),
        grid_spec=pltpu.PrefetchScalarGridSpec(
            num_scalar_prefetch=1,
            grid=(t // tt,),
            in_specs=in_specs,
            out_specs=out_specs,
            scratch_shapes=[pltpu.VMEM((2, TOP_K, tt * ROW_TILE, LANES), F32), pltpu.SemaphoreType.DMA((2,))],
        ),
        out_shape=out_shape,
        compiler_params=_cparams(("arbitrary",)),
        name="moe_combine_ln",
    )(*operands)


def _moe_block(x1, h3, idx, gates, mod, layer, w1, b1, w2, b2, ln_g, ln_b, seq, next_pw1=None):
    t = x1.shape[0]
    rank_t, idx_t, counts = _rank_and_count(idx)
    pstarts, block_e, live = _routing_tables(t, counts)
    dest = _dest_rows(pstarts, idx_t, rank_t)
    n_rows = _num_expert_blocks(t * TOP_K) * EXPERT_TM
    xb = _dispatch(dest, live, h3, n_rows)
    yb = _experts(block_e, live, xb, layer, w1, b1, w2, b2)
    return _combine(dest, yb, gates, x1, mod, ln_g, ln_b, seq, next_pw1)


def kernel(x, c, ada_w, ada_b, dn_in_w, dn_conv_w, dn_A_log, dn_dt_bias, dn_onorm_w, dn_out_w,
           cf_pw1_w, cf_pw1_b, cf_dw_w, cf_dw_b, cf_ln_g, cf_ln_b, cf_pw2_w, cf_pw2_b,
           ln1_g, ln1_b, router_w, router_b, e_w1, e_b1, e_w2, e_b2, ln2_g, ln2_b):
    bsz, seq, d = x.shape
    mods = _ada_ln(c, ada_w, ada_b)

    proj, ba = _dn_in_proj(x, mods[0], dn_in_w[0])
    qn, kn, vv, gates_dn = _dn_prep(proj, ba, dn_conv_w[0], dn_A_log[0], dn_dt_bias[0])
    o = _dn_chunk(qn, kn, vv, proj, gates_dn, dn_onorm_w[0])
    x1, h3, idx, gates = _dn_out(o, dn_out_w[0], x, mods[0], ln1_g[0], ln1_b[0], router_w[0], router_b[0])
    x2, u = _moe_block(x1, h3, idx, gates, mods[0], 0, e_w1, e_b1, e_w2, e_b2, ln2_g[0], ln2_b[0], seq,
                       next_pw1=(mods[1], cf_pw1_w[0], cf_pw1_b[0]))

    x2 = x2.reshape(bsz, seq, d)
    x3, h3, idx, gates = _cf_tail(u, cf_dw_w[0], cf_dw_b[0], cf_ln_g[0], cf_ln_b[0], cf_pw2_w[0], cf_pw2_b[0],
                                  x2, mods[1], ln1_g[1], ln1_b[1], router_w[1], router_b[1])
    (x4,) = _moe_block(x3, h3, idx, gates, mods[1], 1, e_w1, e_b1, e_w2, e_b2, ln2_g[1], ln2_b[1], seq)
    return x4.reshape(bsz, seq, d)
```

```python
import functools

import jax
import jax.numpy as jnp
from jax import lax
from jax.experimental import pallas as pl
from jax.experimental.pallas import tpu as pltpu

F32 = jnp.float32
BF16 = jnp.bfloat16
HIGHEST = lax.Precision.HIGHEST

DEPTH = 2
DN_QK_HEADS = 8
DN_V_HEADS = 16
DN_HEAD_DIM = 128
DN_CONV = 4
DN_CHUNK = 64
CF_KERNEL = 31
N_EXPERTS = 32
TOP_K = 4
SWIGLU_LIMIT = 7.0
SWIGLU_ALPHA = 1.702
LN_EPS = 1e-5
RMS_EPS = 1e-6
L2_EPS = 1e-6
DEEPNORM_ALPHA = (2 * DEPTH) ** 0.25

LANES = 128
SUBLANES = 8
BF16_SUBLANES = 16
VMEM_LIMIT = 56 * 1024 * 1024

ADA_TN = 1536
PROJ_TM = 1024
PROJ_TN = 2048
PREP_TS = 512
CHUNKS_PER_STEP = 4
DN_CHUNK_UNROLL = 2
POST_TM = 512
ROUTE_TT = 256
DISPATCH_TT = 512
EXPERT_TM = 512
CF_TM = 512
CF_HALO = 32
CF_CONV_ROWS = 32

DEST_SLOT_PAD = 1
NEG_BIG = -1e30


def _cparams(sem):
    return pltpu.CompilerParams(dimension_semantics=sem, vmem_limit_bytes=VMEM_LIMIT)


def _sigmoid(x):
    return jax.nn.sigmoid(x)


def _layer_norm(v, g, b):
    mu = jnp.mean(v, -1, keepdims=True)
    d = v - mu
    var = jnp.mean(d * d, -1, keepdims=True)
    return d * lax.rsqrt(var + LN_EPS) * g + b


ROW_TILE = SUBLANES


def _rows_to_tiles(o_ref, val):
    rows = val.shape[0]
    for s in range(ROW_TILE):
        o_ref[pl.ds(s, rows, stride=ROW_TILE), :] = val[:, s * LANES:(s + 1) * LANES]


def _tiles_to_rows(x_ref):
    rows = x_ref.shape[0] // ROW_TILE
    return jnp.concatenate([x_ref[pl.ds(s, rows, stride=ROW_TILE), :] for s in range(ROW_TILE)], axis=1)


def _split_weight(w):
    k, n = w.shape
    hi = w.astype(BF16)
    lo = (w - hi.astype(F32)).astype(BF16)
    out = jnp.zeros((k, 2 * LANES), BF16)
    return out.at[:, :n].set(hi).at[:, LANES:LANES + n].set(lo)


def _narrow_dot(h, w_split_ref):
    m = h.shape[0]
    hi = h.astype(BF16)
    lo = (h - hi.astype(F32)).astype(BF16)
    out = jnp.dot(jnp.concatenate([hi, lo], axis=0), w_split_ref[...], preferred_element_type=F32)
    return out[:m, :LANES] + out[:m, LANES:] + out[m:, :LANES]


def _token_tile(ref, tok):
    return ref.at[pl.ds(pl.multiple_of(tok * ROW_TILE, ROW_TILE), ROW_TILE), :]


def _ada_kernel(c_ref, w_ref, b_ref, o_ref):
    c = c_ref[...]
    cond = c * _sigmoid(c)
    o_ref[0] = jnp.dot(cond, w_ref[0], precision=HIGHEST, preferred_element_type=F32) + b_ref[0]


def _ada_ln(c, ada_w, ada_b):
    depth, d, n = ada_w.shape
    bsz = c.shape[0]
    c_pad = jnp.zeros((SUBLANES, d), F32).at[:bsz].set(c)
    out = pl.pallas_call(
        _ada_kernel,
        grid=(depth, n // ADA_TN),
        in_specs=[
            pl.BlockSpec((SUBLANES, d), lambda i, j: (0, 0)),
            pl.BlockSpec((1, d, ADA_TN), lambda i, j: (i, 0, j)),
            pl.BlockSpec((1, 1, ADA_TN), lambda i, j: (i, 0, j)),
        ],
        out_specs=pl.BlockSpec((1, SUBLANES, ADA_TN), lambda i, j: (i, 0, j)),
        out_shape=jax.ShapeDtypeStruct((depth, SUBLANES, n), F32),
        compiler_params=_cparams(("parallel", "parallel")),
        name="ada_ln",
    )(c_pad, ada_w, ada_b.reshape(depth, 1, n))
    return out[:, :bsz].reshape(depth, bsz, 6, d)


def _inproj_kernel(x_ref, mod_ref, w_ref, wba_ref, proj_ref, ba_ref, h_scr):
    @pl.when(pl.program_id(2) == 0)
    def _():
        h = x_ref[0] * (1.0 + mod_ref[0, 1:2, :]) + mod_ref[0, 0:1, :]
        h_scr[...] = h.astype(BF16)
        ba_ref[0] = _narrow_dot(h, wba_ref)

    proj_ref[0] = jnp.dot(h_scr[...], w_ref[...], preferred_element_type=F32).astype(BF16)


def _dn_in_proj(x, mod, in_w):
    bsz, seq, d = x.shape
    n_main = in_w.shape[1] - 2 * DN_V_HEADS
    w_main = in_w[:, :n_main].astype(BF16)
    w_ba = _split_weight(in_w[:, n_main:])
    tm = min(PROJ_TM, seq)
    return pl.pallas_call(
        _inproj_kernel,
        grid=(bsz, seq // tm, n_main // PROJ_TN),
        in_specs=[
            pl.BlockSpec((1, tm, d), lambda b, i, j: (b, i, 0)),
            pl.BlockSpec((1, 6, d), lambda b, i, j: (b, 0, 0)),
            pl.BlockSpec((d, PROJ_TN), lambda b, i, j: (0, j)),
            pl.BlockSpec((d, 2 * LANES), lambda b, i, j: (0, 0)),
        ],
        out_specs=[
            pl.BlockSpec((1, tm, PROJ_TN), lambda b, i, j: (b, i, j)),
            pl.BlockSpec((1, tm, LANES), lambda b, i, j: (b, i, 0)),
        ],
        out_shape=[
            jax.ShapeDtypeStruct((bsz, seq, n_main), BF16),
            jax.ShapeDtypeStruct((bsz, seq, LANES), F32),
        ],
        scratch_shapes=[pltpu.VMEM((tm, d), BF16)],
        compiler_params=_cparams(("parallel", "parallel", "arbitrary")),
        name="dn_in_proj",
    )(x, mod, w_main, w_ba)


def _dn_prep_kernel(q_ref, k_ref, v_ref, qh_ref, kh_ref, vh_ref, cw_ref, ba_ref, alog_ref, dt_ref,
                    qo_ref, ko_ref, vo_ref, g_ref, scr):
    ts = q_ref.shape[1]
    halo = qh_ref.shape[1]
    keep = (pl.program_id(1) > 0).astype(F32)

    def conv_silu(x_ref, h_ref, c0):
        width = x_ref.shape[2]
        scr[0:halo, 0:width] = h_ref[0].astype(F32) * keep
        scr[halo:halo + ts, 0:width] = x_ref[0].astype(F32)
        acc = None
        for j in range(DN_CONV):
            off = halo - (DN_CONV - 1) + j
            term = scr[off:off + ts, 0:width] * cw_ref[j:j + 1, c0:c0 + width]
            acc = term if acc is None else acc + term
        return acc * _sigmoid(acc)

    def l2norm_store(o_ref, x, scale):
        for h in range(x.shape[1] // DN_HEAD_DIM):
            xh = x[:, h * DN_HEAD_DIM:(h + 1) * DN_HEAD_DIM]
            ss = jnp.sum(xh * xh, -1, keepdims=True)
            o_ref[0, :, h * DN_HEAD_DIM:(h + 1) * DN_HEAD_DIM] = (
                xh * lax.rsqrt(ss + L2_EPS) * scale).astype(o_ref.dtype)

    kd = q_ref.shape[2]
    l2norm_store(qo_ref, conv_silu(q_ref, qh_ref, 0), DN_HEAD_DIM ** -0.5)
    l2norm_store(ko_ref, conv_silu(k_ref, kh_ref, kd), 1.0)
    vo_ref[0] = conv_silu(v_ref, vh_ref, 2 * kd).astype(vo_ref.dtype)

    ba = ba_ref[0]
    beta = _sigmoid(ba)
    zz = ba + dt_ref[...]
    softplus = jnp.maximum(zz, 0.0) + jnp.log1p(jnp.exp(-jnp.abs(zz)))
    g = -jnp.exp(alog_ref[...]) * softplus
    r = lax.broadcasted_iota(jnp.int32, (ts, ts), 0)
    c = lax.broadcasted_iota(jnp.int32, (ts, ts), 1)
    in_chunk_tril = ((r // DN_CHUNK == c // DN_CHUNK) & (c <= r)).astype(BF16)
    g1 = g.astype(BF16)
    g2 = (g - g1.astype(F32)).astype(BF16)
    g3 = (g - g1.astype(F32) - g2.astype(F32)).astype(BF16)
    parts = jnp.dot(in_chunk_tril, jnp.concatenate([g1, g2, g3], axis=1), preferred_element_type=F32)
    gcum = parts[:, :LANES] + parts[:, LANES:2 * LANES] + parts[:, 2 * LANES:]
    lane = lax.broadcasted_iota(jnp.int32, ba.shape, 1)
    g_ref[0] = jnp.where(lane < DN_V_HEADS, beta, gcum)


def _dn_prep(proj, ba, conv_w, a_log, dt_bias):
    bsz, seq, _ = proj.shape
    kd = DN_QK_HEADS * DN_HEAD_DIM
    vd = DN_V_HEADS * DN_HEAD_DIM
    ts = min(PREP_TS, seq)
    halo = BF16_SUBLANES
    hb = ts // halo
    alog_row = jnp.zeros((1, LANES), F32).at[0, DN_V_HEADS:2 * DN_V_HEADS].set(a_log)
    dt_row = jnp.zeros((1, LANES), F32).at[0, DN_V_HEADS:2 * DN_V_HEADS].set(dt_bias)

    def halo_map(col):
        return lambda b, i: (b, jnp.maximum(i * hb - 1, 0), col)

    return pl.pallas_call(
        _dn_prep_kernel,
        grid=(bsz, seq // ts),
        in_specs=[
            pl.BlockSpec((1, ts, kd), lambda b, i: (b, i, 0)),
            pl.BlockSpec((1, ts, kd), lambda b, i: (b, i, 1)),
            pl.BlockSpec((1, ts, vd), lambda b, i: (b, i, 1)),
            pl.BlockSpec((1, halo, kd), halo_map(0)),
            pl.BlockSpec((1, halo, kd), halo_map(1)),
            pl.BlockSpec((1, halo, vd), halo_map(1)),
            pl.BlockSpec((DN_CONV, 2 * kd + vd), lambda b, i: (0, 0)),
            pl.BlockSpec((1, ts, LANES), lambda b, i: (b, i, 0)),
            pl.BlockSpec((1, LANES), lambda b, i: (0, 0)),
            pl.BlockSpec((1, LANES), lambda b, i: (0, 0)),
        ],
        out_specs=[
            pl.BlockSpec((1, ts, kd), lambda b, i: (b, i, 0)),
            pl.BlockSpec((1, ts, kd), lambda b, i: (b, i, 0)),
            pl.BlockSpec((1, ts, vd), lambda b, i: (b, i, 0)),
            pl.BlockSpec((1, ts, LANES), lambda b, i: (b, i, 0)),
        ],
        out_shape=[
            jax.ShapeDtypeStruct((bsz, seq, kd), BF16),
            jax.ShapeDtypeStruct((bsz, seq, kd), BF16),
            jax.ShapeDtypeStruct((bsz, seq, vd), BF16),
            jax.ShapeDtypeStruct((bsz, seq, LANES), F32),
        ],
        scratch_shapes=[pltpu.VMEM((ts + halo, vd), F32)],
        compiler_params=_cparams(("parallel", "arbitrary")),
        name="dn_prep",
    )(proj, proj, proj, proj, proj, proj, conv_w, ba, alog_row, dt_row)


def _bmm(a, b):
    return lax.dot_general(a.astype(BF16), b.astype(BF16), (((2,), (1,)), ((0,), (0,))),
                           preferred_element_type=F32)


def _bmm_nt(a, b):
    return lax.dot_general(a.astype(BF16), b.astype(BF16), (((2,), (2,)), ((0,), (0,))),
                           preferred_element_type=F32)


def _bmm_tn(a, b):
    return lax.dot_general(a.astype(BF16), b.astype(BF16), (((1,), (1,)), ((0,), (0,))),
                           preferred_element_type=F32)


def _unit_lower_inverse_wide(a_twice, upper, eye_upper):
    c = a_twice.shape[1]
    x = jnp.where(upper, eye_upper, -a_twice)
    span = 1
    while span < c:
        x = _bmm(x[:, :, :c], x) + jnp.where(upper, x, 0.0)
        span *= 2
    return x


def _dn_chunk_kernel(q_ref, k_ref, v_ref, z_ref, g_ref, gt_ref, ow_ref, o_ref, s_ref):
    c_len = DN_CHUNK
    dh = DN_HEAD_DIM
    rep = DN_V_HEADS // DN_QK_HEADS

    @pl.when(pl.program_id(1) == 0)
    def _():
        s_ref[...] = jnp.zeros(s_ref.shape, F32)

    heads = range(DN_V_HEADS)
    n_heads = DN_V_HEADS
    ri = lax.broadcasted_iota(jnp.int32, (1, c_len, 2 * c_len), 1)
    lane = lax.broadcasted_iota(jnp.int32, (1, c_len, 2 * c_len), 2)
    upper = lane >= c_len
    ci = jnp.where(upper, lane - c_len, lane)
    causal = ci <= ri
    strict = ci < ri
    eye_upper = ((ci == ri) & upper).astype(F32)
    onorm = ow_ref[...]

    def head_cols(ref, rows, h):
        return ref[0, rows, h * dh:(h + 1) * dh]

    def per_v_head(t):
        return jnp.stack([t[h // rep] for h in heads])

    def state_free_part(c):
        r0 = pl.multiple_of(c * c_len, c_len)
        rows = pl.ds(r0, c_len)
        gcols = g_ref[0, rows, :]
        grows = gt_ref[0, c]
        wide = (n_heads, c_len, dh)
        beta_b = jnp.broadcast_to(jnp.stack([gcols[:, h:h + 1] for h in heads]), wide)
        g_b = jnp.broadcast_to(
            jnp.stack([gcols[:, DN_V_HEADS + h:DN_V_HEADS + h + 1] for h in heads]), wide)
        g_r = jnp.stack([grows[DN_V_HEADS + h:DN_V_HEADS + h + 1, :] for h in heads])
        g_r = jnp.concatenate([g_r, g_r], axis=2)
        g_last = g_b[:, c_len - 1:c_len, :]

        qn = jnp.stack([head_cols(q_ref, rows, hq) for hq in range(DN_QK_HEADS)])
        kn = jnp.stack([head_cols(k_ref, rows, hq) for hq in range(DN_QK_HEADS)])
        qk_kk = _bmm_nt(jnp.concatenate([qn, kn], axis=1), jnp.concatenate([kn, kn], axis=1))
        qk = per_v_head(qk_kk[:, :c_len])
        kk = per_v_head(qk_kk[:, c_len:])
        qf = per_v_head(qn).astype(F32)
        kf = per_v_head(kn).astype(F32)
        vf = jnp.stack([head_cols(v_ref, rows, h) for h in heads]).astype(F32)

        decay = jnp.where(causal, jnp.exp(g_b - g_r), 0.0)
        a_twice = jnp.where(strict, beta_b * kk * decay, 0.0)
        x_inv = _unit_lower_inverse_wide(a_twice, upper, eye_upper)
        eg = jnp.exp(g_b)
        rhs = jnp.concatenate([vf * beta_b, kf * (beta_b * eg)], axis=2)
        sol = _bmm(x_inv, jnp.concatenate([jnp.zeros_like(rhs), rhs], axis=1))
        u = sol[:, :, :dh]
        w = sol[:, :, dh:]
        qkm = jnp.where(causal, qk * decay, 0.0)[:, :, :c_len]
        qg = qf * eg
        kdec = kf * jnp.exp(g_last - g_b)
        return rows, jnp.concatenate([w, qg], axis=1), u, qkm, kdec, jnp.exp(g_last)

    def state_part(rows, w_qg, u, qkm, kdec, chunk_decay):
        state = s_ref[...]
        ws = _bmm(w_qg, state)
        v_new = u - ws[:, :c_len]
        o = ws[:, c_len:] + _bmm(qkm, v_new)
        s_ref[...] = state * chunk_decay + _bmm_tn(kdec, v_new)

        o = o * lax.rsqrt(jnp.mean(o * o, -1, keepdims=True) + RMS_EPS) * onorm
        for h in heads:
            zf = head_cols(z_ref, rows, h).astype(F32)
            o_ref[0, rows, h * dh:(h + 1) * dh] = (o[h] * (zf * _sigmoid(zf))).astype(o_ref.dtype)

    def chunk_group(i, carry):
        prepared = [state_free_part(i * DN_CHUNK_UNROLL + j) for j in range(DN_CHUNK_UNROLL)]
        for args in prepared:
            state_part(*args)
        return carry

    lax.fori_loop(0, q_ref.shape[1] // (c_len * DN_CHUNK_UNROLL), chunk_group, 0)


def _dn_chunk(qn, kn, vv, proj, gates, onorm_w):
    bsz, seq, kd = qn.shape
    vd = vv.shape[2]
    n_chunks = seq // DN_CHUNK
    cb = min(CHUNKS_PER_STEP, n_chunks)
    rows = cb * DN_CHUNK
    gates_t = jnp.swapaxes(gates[:, :, :2 * DN_V_HEADS].reshape(bsz, n_chunks, DN_CHUNK, 2 * DN_V_HEADS), 2, 3)
    z_col = (2 * kd + vd) // vd
    return pl.pallas_call(
        _dn_chunk_kernel,
        grid=(bsz, n_chunks // cb),
        in_specs=[
            pl.BlockSpec((1, rows, kd), lambda b, n: (b, n, 0)),
            pl.BlockSpec((1, rows, kd), lambda b, n: (b, n, 0)),
            pl.BlockSpec((1, rows, vd), lambda b, n: (b, n, 0)),
            pl.BlockSpec((1, rows, vd), lambda b, n: (b, n, z_col)),
            pl.BlockSpec((1, rows, LANES), lambda b, n: (b, n, 0)),
            pl.BlockSpec((1, cb, 2 * DN_V_HEADS, DN_CHUNK), lambda b, n: (b, n, 0, 0)),
            pl.BlockSpec((1, DN_HEAD_DIM), lambda b, n: (0, 0)),
        ],
        out_specs=pl.BlockSpec((1, rows, vd), lambda b, n: (b, n, 0)),
        out_shape=jax.ShapeDtypeStruct((bsz, seq, vd), BF16),
        scratch_shapes=[pltpu.VMEM((DN_V_HEADS, DN_HEAD_DIM, DN_HEAD_DIM), F32)],
        compiler_params=_cparams(("parallel", "arbitrary")),
        name="dn_chunk",
    )(qn, kn, vv, proj, gates, gates_t, onorm_w.reshape(1, DN_HEAD_DIM))


def _residual_ln(x, y, gate_row, ln_g, ln_b):
    return _layer_norm(DEEPNORM_ALPHA * x + (1.0 + gate_row) * y, ln_g, ln_b)


def _route_store(h, rw_ref, rb_ref, idx_ref, gate_ref):
    logits = _narrow_dot(h, rw_ref) + rb_ref[...]
    lane = lax.broadcasted_iota(jnp.int32, logits.shape, 1).astype(F32)
    work = logits
    idx_out = jnp.zeros(logits.shape, F32)
    val_out = jnp.full(logits.shape, NEG_BIG, F32)
    for k in range(TOP_K):
        m = jnp.max(work, -1, keepdims=True)
        am = jnp.min(jnp.where(work == m, lane, float(LANES)), -1, keepdims=True)
        idx_out = jnp.where(lane == k, am, idx_out)
        val_out = jnp.where(lane == k, m, val_out)
        work = jnp.where(lane == am, NEG_BIG * 2.0, work)
    top = jnp.max(val_out, -1, keepdims=True)
    e = jnp.where(lane < TOP_K, jnp.exp(val_out - top), 0.0)
    idx_ref[...] = idx_out.astype(jnp.int32)
    gate_ref[...] = e / jnp.sum(e, -1, keepdims=True)


def _post_mixer_tail(x, y, mod_ref, lng_ref, lnb_ref, rw_ref, rb_ref, x1_ref, h3_ref, idx_ref, gate_ref):
    x1 = _residual_ln(x, y, mod_ref[0, 2:3, :], lng_ref[...], lnb_ref[...])
    x1_ref[...] = x1
    h2 = x1 * (1.0 + mod_ref[0, 4:5, :]) + mod_ref[0, 3:4, :]
    _rows_to_tiles(h3_ref, h2)
    _route_store(h2, rw_ref, rb_ref, idx_ref, gate_ref)


def _router_operands(router_w, router_b):
    rb = jnp.full((1, LANES), NEG_BIG, F32).at[0, :N_EXPERTS].set(router_b)
    return _split_weight(router_w), rb


def _post_out_specs(tm, d):
    return [
        pl.BlockSpec((tm, d), lambda i: (i, 0)),
        pl.BlockSpec((tm * ROW_TILE, LANES), lambda i: (i, 0)),
        pl.BlockSpec((tm, LANES), lambda i: (i, 0)),
        pl.BlockSpec((tm, LANES), lambda i: (i, 0)),
    ]


def _post_out_shapes(t, d):
    assert d == ROW_TILE * LANES
    return [
        jax.ShapeDtypeStruct((t, d), F32),
        jax.ShapeDtypeStruct((t * ROW_TILE, LANES), F32),
        jax.ShapeDtypeStruct((t, LANES), jnp.int32),
        jax.ShapeDtypeStruct((t, LANES), F32),
    ]


def _dn_out_kernel(o_ref, w_ref, x_ref, mod_ref, lng_ref, lnb_ref, rw_ref, rb_ref,
                   x1_ref, h3_ref, idx_ref, gate_ref):
    y = jnp.dot(o_ref[...], w_ref[...], preferred_element_type=F32)
    _post_mixer_tail(x_ref[...], y, mod_ref, lng_ref, lnb_ref, rw_ref, rb_ref,
                     x1_ref, h3_ref, idx_ref, gate_ref)


def _dn_out(o, out_w, x, mod, ln_g, ln_b, router_w, router_b):
    bsz, seq, d = x.shape
    t = bsz * seq
    vd = o.shape[2]
    tm = min(POST_TM, seq)
    rw, rb = _router_operands(router_w, router_b)
    per_b = seq // tm
    return pl.pallas_call(
        _dn_out_kernel,
        grid=(t // tm,),
        in_specs=[
            pl.BlockSpec((tm, vd), lambda i: (i, 0)),
            pl.BlockSpec((vd, d), lambda i: (0, 0)),
            pl.BlockSpec((tm, d), lambda i: (i, 0)),
            pl.BlockSpec((1, 6, d), lambda i: (i // per_b, 0, 0)),
            pl.BlockSpec((1, d), lambda i: (0, 0)),
            pl.BlockSpec((1, d), lambda i: (0, 0)),
            pl.BlockSpec((d, 2 * LANES), lambda i: (0, 0)),
            pl.BlockSpec((1, LANES), lambda i: (0, 0)),
        ],
        out_specs=_post_out_specs(tm, d),
        out_shape=_post_out_shapes(t, d),
        compiler_params=_cparams(("parallel",)),
        name="dn_out_ln_route",
    )(o.reshape(t, vd), out_w.astype(BF16), x.reshape(t, d), mod, ln_g.reshape(1, d), ln_b.reshape(1, d), rw, rb)


def _pw1_glu_store(x, mod_ref, wa_ref, wb_ref, ba_ref, bb_ref, u_ref):
    h = (x * (1.0 + mod_ref[0, 1:2, :]) + mod_ref[0, 0:1, :]).astype(BF16)
    pa = jnp.dot(h, wa_ref[...], preferred_element_type=F32) + ba_ref[...]
    pb = jnp.dot(h, wb_ref[...], preferred_element_type=F32) + bb_ref[...]
    _rows_to_tiles(u_ref, pa * _sigmoid(pb))


def _cf_tail_kernel(u_ref, uh_ref, dw_ref, dwb_ref, cg_ref, cb_ref, w2_ref, b2_ref,
                    x_ref, mod_ref, lng_ref, lnb_ref, rw_ref, rb_ref,
                    x1_ref, h3_ref, idx_ref, gate_ref, scr, conv_scr):
    tm = u_ref.shape[0] // ROW_TILE
    halo = uh_ref.shape[0] // ROW_TILE
    keep = (pl.program_id(1) > 0).astype(F32)
    scr[0:halo * ROW_TILE, :] = uh_ref[...] * keep
    scr[halo * ROW_TILE:(halo + tm) * ROW_TILE, :] = u_ref[...]
    first = halo - (CF_KERNEL - 1)
    blk = CF_CONV_ROWS * ROW_TILE

    def conv_tokens(tb, carry):
        r0 = pl.multiple_of(tb * blk, blk)
        acc = jnp.concatenate([dwb_ref[...]] * CF_CONV_ROWS, axis=0)
        for j in range(CF_KERNEL):
            w_tile = dw_ref[j * ROW_TILE:(j + 1) * ROW_TILE, :]
            w_blk = jnp.concatenate([w_tile] * CF_CONV_ROWS, axis=0)
            acc = acc + scr[pl.ds(r0 + (first + j) * ROW_TILE, blk), :] * w_blk
        conv_scr[pl.ds(r0, blk), :] = acc
        return carry

    lax.fori_loop(0, tm // CF_CONV_ROWS, conv_tokens, 0)
    conv = _tiles_to_rows(conv_scr)
    normed = _layer_norm(conv, cg_ref[...], cb_ref[...])
    act = normed * _sigmoid(normed)
    y = jnp.dot(act.astype(BF16), w2_ref[...], preferred_element_type=F32) + b2_ref[...]
    _post_mixer_tail(x_ref[0], y, mod_ref, lng_ref, lnb_ref, rw_ref, rb_ref,
                     x1_ref, h3_ref, idx_ref, gate_ref)


def _cf_tail(u, dw_w, dw_b, cf_ln_g, cf_ln_b, pw2_w, pw2_b, x, mod, ln_g, ln_b, router_w, router_b):
    bsz, seq, d = x.shape
    inner = dw_w.shape[1]
    t = bsz * seq
    tm = min(CF_TM, seq)
    per_b = seq // tm
    hb = tm // CF_HALO
    halos_per_b = seq // CF_HALO
    rw, rb = _router_operands(router_w, router_b)
    row = lambda v: v.reshape(1, -1)
    tiles = lambda v: v.reshape(-1, LANES)
    const = lambda b, i: (0, 0)
    flat = lambda b, i: (b * per_b + i, 0)
    return pl.pallas_call(
        _cf_tail_kernel,
        grid=(bsz, per_b),
        in_specs=[
            pl.BlockSpec((tm * ROW_TILE, LANES), flat),
            pl.BlockSpec((CF_HALO * ROW_TILE, LANES),
                         lambda b, i: (b * halos_per_b + jnp.maximum(i * hb - 1, 0), 0)),
            pl.BlockSpec((CF_KERNEL * ROW_TILE, LANES), const),
            pl.BlockSpec((ROW_TILE, LANES), const),
            pl.BlockSpec((1, inner), const),
            pl.BlockSpec((1, inner), const),
            pl.BlockSpec((inner, d), const),
            pl.BlockSpec((1, d), const),
            pl.BlockSpec((1, tm, d), lambda b, i: (b, i, 0)),
            pl.BlockSpec((1, 6, d), lambda b, i: (b, 0, 0)),
            pl.BlockSpec((1, d), const),
            pl.BlockSpec((1, d), const),
            pl.BlockSpec((d, 2 * LANES), const),
            pl.BlockSpec((1, LANES), const),
        ],
        out_specs=[
            pl.BlockSpec((tm, d), flat),
            pl.BlockSpec((tm * ROW_TILE, LANES), flat),
            pl.BlockSpec((tm, LANES), flat),
            pl.BlockSpec((tm, LANES), flat),
        ],
        out_shape=_post_out_shapes(t, d),
        scratch_shapes=[pltpu.VMEM(((tm + CF_HALO) * ROW_TILE, LANES), F32),
                        pltpu.VMEM((tm * ROW_TILE, LANES), F32)],
        compiler_params=_cparams(("parallel", "arbitrary")),
        name="cf_conv_ln_pw2_route",
    )(u, u, tiles(dw_w), tiles(dw_b), row(cf_ln_g), row(cf_ln_b), pw2_w.astype(BF16), row(pw2_b),
      x, mod, row(ln_g), row(ln_b), rw, rb)


def _rank_kernel(idx_ref, rank_t_ref, idx_t_ref, loc_ref, tile_ref, cnt_ref, carry):
    tt = idx_ref.shape[0]

    @pl.when(pl.program_id(0) == 0)
    def _():
        carry[...] = jnp.zeros(carry.shape, F32)

    idx = idx_ref[...]
    lane = lax.broadcasted_iota(jnp.int32, idx.shape, 1)
    sel = [lane == idx[:, k:k + 1] for k in range(TOP_K)]
    multi_hot = sel[0]
    for k in range(1, TOP_K):
        multi_hot = multi_hot | sel[k]
    mh = multi_hot.astype(BF16)
    r = lax.broadcasted_iota(jnp.int32, (tt, tt), 0)
    c = lax.broadcasted_iota(jnp.int32, (tt, tt), 1)
    before_in_tile = jnp.dot((c < r).astype(BF16), mh, preferred_element_type=F32)
    before_tile = carry[...]
    run_len = jnp.sum(mh.astype(F32), 0, keepdims=True)
    e_from = lax.broadcasted_iota(jnp.int32, (LANES, LANES), 0)
    e_to = lax.broadcasted_iota(jnp.int32, (LANES, LANES), 1)
    run_start = jnp.dot(jnp.broadcast_to(run_len, (SUBLANES, LANES)).astype(BF16),
                        (e_from < e_to).astype(BF16), preferred_element_type=F32)[:1]
    rank = jnp.zeros(idx.shape, F32)
    loc = jnp.zeros(idx.shape, F32)
    for k in range(TOP_K):
        in_tile = jnp.sum(jnp.where(sel[k], before_in_tile, 0.0), -1, keepdims=True)
        rk = in_tile + jnp.sum(jnp.where(sel[k], before_tile, 0.0), -1, keepdims=True)
        lk = in_tile + jnp.sum(jnp.where(sel[k], run_start, 0.0), -1, keepdims=True)
        rank = jnp.where(lane == k, rk, rank)
        loc = jnp.where(lane == k, lk, loc)
    rank_t_ref[...] = jnp.transpose(rank)[:SUBLANES].astype(jnp.int32)
    idx_t_ref[...] = jnp.transpose(idx.astype(F32))[:SUBLANES].astype(jnp.int32)
    loc_ref[...] = loc.astype(jnp.int32)
    row = lax.broadcasted_iota(jnp.int32, (SUBLANES, LANES), 0)
    tile_ref[0] = jnp.where(row == 0, before_tile, jnp.where(row == 1, run_len, run_start)).astype(jnp.int32)
    total = before_tile + run_len
    carry[...] = total
    cnt_ref[...] = total.astype(jnp.int32)


def _rank_and_count(idx):
    t = idx.shape[0]
    tt = ROUTE_TT
    assert t % tt == 0 and tt <= 256
    return pl.pallas_call(
        _rank_kernel,
        grid=(t // tt,),
        in_specs=[pl.BlockSpec((tt, LANES), lambda i: (i, 0))],
        out_specs=[
            pl.BlockSpec((SUBLANES, tt), lambda i: (0, i)),
            pl.BlockSpec((SUBLANES, tt), lambda i: (0, i)),
            pl.BlockSpec((tt, LANES), lambda i: (i, 0)),
            pl.BlockSpec((1, SUBLANES, LANES), lambda i: (i, 0, 0)),
            pl.BlockSpec((1, LANES), lambda i: (0, 0)),
        ],
        out_shape=[
            jax.ShapeDtypeStruct((SUBLANES, t), jnp.int32),
            jax.ShapeDtypeStruct((SUBLANES, t), jnp.int32),
            jax.ShapeDtypeStruct((t, LANES), jnp.int32),
            jax.ShapeDtypeStruct((t // tt, SUBLANES, LANES), jnp.int32),
            jax.ShapeDtypeStruct((1, LANES), jnp.int32),
        ],
        scratch_shapes=[pltpu.VMEM((1, LANES), F32)],
        compiler_params=_cparams(("arbitrary",)),
        name="moe_rank_count",
    )(idx)


def _dest_kernel(pstart_ref, idx_t_ref, rank_t_ref, dest_t_ref):
    idx = idx_t_ref[...]
    base = jnp.zeros(idx.shape, jnp.int32)
    for e in range(N_EXPERTS):
        base = jnp.where(idx == e, pstart_ref[e], base)
    dest_t_ref[...] = base + rank_t_ref[...]


def _dest_rows(pstarts, idx_t, rank_t):
    t = idx_t.shape[1]
    dest_t = pl.pallas_call(
        _dest_kernel,
        grid_spec=pltpu.PrefetchScalarGridSpec(
            num_scalar_prefetch=1,
            grid=(1,),
            in_specs=[pl.BlockSpec((SUBLANES, t), lambda i, ps: (0, 0)),
                      pl.BlockSpec((SUBLANES, t), lambda i, ps: (0, 0))],
            out_specs=pl.BlockSpec((SUBLANES, t), lambda i, ps: (0, 0)),
        ),
        out_shape=jax.ShapeDtypeStruct((SUBLANES, t), jnp.int32),
        compiler_params=_cparams(("arbitrary",)),
        name="moe_dest_rows",
    )(pstarts, idx_t, rank_t)
    return jnp.pad(dest_t[:TOP_K], ((0, 0), (0, DEST_SLOT_PAD))).reshape(TOP_K * (t + DEST_SLOT_PAD))


def _num_expert_blocks(n_assign):
    bound = n_assign + N_EXPERTS * (EXPERT_TM - 1)
    return -(-bound // EXPERT_TM)


def _routing_tables(n_tokens, counts):
    nb = _num_expert_blocks(n_tokens * TOP_K)
    counts = counts[0, :N_EXPERTS]
    padded = (counts + EXPERT_TM - 1) // EXPERT_TM * EXPERT_TM
    pends = jnp.cumsum(padded)
    pstarts = (pends - padded).astype(jnp.int32)
    block_row0 = jnp.arange(nb, dtype=jnp.int32) * EXPERT_TM
    segments_done = jnp.sum((pends[None, :] <= block_row0[:, None]).astype(jnp.int32), axis=1)
    block_e = jnp.minimum(segments_done, N_EXPERTS - 1)
    r0 = block_row0[:, None]
    in_segment = (pstarts[None, :] <= r0) & (r0 < pends[None, :])
    live_if = jnp.clip(pstarts[None, :] + counts[None, :] - r0, 0, EXPERT_TM)
    live = jnp.sum(jnp.where(in_segment, live_if, 0), axis=1).astype(jnp.int32)
    return pstarts, block_e, live


def _expert_run_tables(block_e):
    nb = block_e.shape[0]
    pos = jnp.arange(nb, dtype=jnp.int32)
    first = jnp.concatenate([jnp.ones((1,), jnp.int32), (block_e[1:] != block_e[:-1]).astype(jnp.int32)])
    slot = (jnp.cumsum(first) - 1) % 2
    start_at_or_after = lax.cummin(jnp.where(first == 1, pos, nb), reverse=True)
    next_start = jnp.concatenate([start_at_or_after[1:], jnp.full((1,), nb, jnp.int32)])
    picks = next_start[:, None] == pos[None, :]
    next_e = jnp.where(next_start < nb, jnp.sum(jnp.where(picks, block_e[None, :], 0), axis=1), -1)
    return first, slot.astype(jnp.int32), next_e.astype(jnp.int32)


def _pad_fill_copies(live_ref, i, zeros, xb_hbm, sem):
    lv = live_ref[i]
    pad = EXPERT_TM - lv
    row0 = i * EXPERT_TM + lv
    copies = []
    piece = EXPERT_TM
    while piece >= 1:
        offset = pad & ~(2 * piece - 1)
        start = pl.multiple_of((row0 + offset) * ROW_TILE, ROW_TILE)
        copy = pltpu.make_async_copy(zeros.at[pl.ds(0, piece * ROW_TILE), :],
                                     xb_hbm.at[pl.ds(start, piece * ROW_TILE), :], sem)
        copies.append(((pad & piece) != 0, copy))
        piece //= 2
    return copies


def _dispatch_kernel(dest_ref, live_ref, h_ref, xb_hbm, zeros, sem, fill_sem):
    tt = h_ref.shape[0] // ROW_TILE
    base = pl.program_id(0) * tt
    slot_stride = pl.num_programs(0) * tt + DEST_SLOT_PAD
    n_blocks = xb_hbm.shape[0] // (EXPERT_TM * ROW_TILE)

    @pl.when(pl.program_id(0) == 0)
    def _():
        zeros[...] = jnp.zeros(zeros.shape, F32)

        def start_fill(i, carry):
            for live_bit, copy in _pad_fill_copies(live_ref, i, zeros, xb_hbm, fill_sem):
                pl.when(live_bit)(copy.start)
            return carry

        lax.fori_loop(0, n_blocks, start_fill, 0)

    def issue(t, carry):
        for k in range(TOP_K):
            pltpu.make_async_copy(_token_tile(h_ref, t),
                                  _token_tile(xb_hbm, dest_ref[k * slot_stride + base + t]), sem).start()
        return carry

    lax.fori_loop(0, tt, issue, 0)
    for _ in range(TOP_K):
        pltpu.make_async_copy(h_ref, xb_hbm.at[pl.ds(0, tt * ROW_TILE), :], sem).wait()

    @pl.when(pl.program_id(0) == 0)
    def _():
        def wait_fill(i, carry):
            for live_bit, copy in _pad_fill_copies(live_ref, i, zeros, xb_hbm, fill_sem):
                pl.when(live_bit)(copy.wait)
            return carry

        lax.fori_loop(0, n_blocks, wait_fill, 0)


def _dispatch(dest, live, h3, n_rows):
    t = h3.shape[0] // ROW_TILE
    return pl.pallas_call(
        _dispatch_kernel,
        grid_spec=pltpu.PrefetchScalarGridSpec(
            num_scalar_prefetch=2,
            grid=(t // DISPATCH_TT,),
            in_specs=[pl.BlockSpec((DISPATCH_TT * ROW_TILE, LANES), lambda i, dst, lv: (i, 0))],
            out_specs=pl.BlockSpec(memory_space=pl.ANY),
            scratch_shapes=[pltpu.VMEM((EXPERT_TM * ROW_TILE, LANES), F32), pltpu.SemaphoreType.DMA,
                            pltpu.SemaphoreType.DMA],
        ),
        out_shape=jax.ShapeDtypeStruct((n_rows * ROW_TILE, LANES), F32),
        compiler_params=_cparams(("arbitrary",)),
        name="moe_dispatch",
    )(dest, live, h3)


def _expert_kernel(be_ref, live_ref, first_ref, slot_ref, next_ref, x_ref, w1_hbm, b1_ref, w2_hbm, b2_ref,
                   y_ref, w1f, w2f, w1s, w2s, sem, *, layer):
    i = pl.program_id(0)
    live = live_ref[i]

    def weight_copies(e, slot):
        return (pltpu.make_async_copy(w1_hbm.at[layer, e], w1f.at[slot], sem.at[0, slot]),
                pltpu.make_async_copy(w2_hbm.at[layer, e], w2f.at[slot], sem.at[1, slot]))

    @pl.when(i == 0)
    def _():
        for copy in weight_copies(be_ref[0], 0):
            copy.start()

    @pl.when(first_ref[i] == 1)
    def _():
        slot = slot_ref[i]
        for copy in weight_copies(be_ref[i], slot):
            copy.wait()
        nxt = next_ref[i]

        @pl.when(nxt >= 0)
        def _():
            for copy in weight_copies(nxt, 1 - slot):
                copy.start()

        w1s[...] = w1f[slot].astype(BF16)
        w2s[...] = w2f[slot].astype(BF16)

    @pl.when(live > 0)
    def _():
        dff = w2s.shape[0]
        x = _tiles_to_rows(x_ref)
        row = lax.broadcasted_iota(jnp.int32, (x.shape[0], 1), 0)
        x = jnp.where(row < live, x, 0.0).astype(BF16)
        hh = jnp.dot(x, w1s[...], preferred_element_type=F32) + b1_ref[0, 0]
        x_glu = jnp.minimum(hh[:, :dff], SWIGLU_LIMIT)
        x_lin = jnp.clip(hh[:, dff:], -SWIGLU_LIMIT, SWIGLU_LIMIT)
        act = x_glu * _sigmoid(SWIGLU_ALPHA * x_glu) * (x_lin + 1.0)
        y = jnp.dot(act.astype(BF16), w2s[...], preferred_element_type=F32) + b2_ref[0, 0]
        _rows_to_tiles(y_ref, y)

    @pl.when(live == 0)
    def _():
        y_ref[...] = jnp.zeros(y_ref.shape, F32)


def _experts(block_e, live, xb, layer, w1, b1, w2, b2):
    depth, n_e, d, two_f = w1.shape
    dff = two_f // 2
    block_rows = EXPERT_TM * ROW_TILE
    nb = xb.shape[0] // block_rows
    first, slot, next_e = _expert_run_tables(block_e)
    bias_map = lambda i, be, *_: (layer, be[i], 0, 0)
    return pl.pallas_call(
        functools.partial(_expert_kernel, layer=layer),
        grid_spec=pltpu.PrefetchScalarGridSpec(
            num_scalar_prefetch=5,
            grid=(nb,),
            in_specs=[
                pl.BlockSpec((block_rows, LANES), lambda i, *_: (i, 0)),
                pl.BlockSpec(memory_space=pl.ANY),
                pl.BlockSpec((1, 1, 1, two_f), bias_map),
                pl.BlockSpec(memory_space=pl.ANY),
                pl.BlockSpec((1, 1, 1, d), bias_map),
            ],
            out_specs=pl.BlockSpec((block_rows, LANES), lambda i, *_: (i, 0)),
            scratch_shapes=[pltpu.VMEM((2, d, two_f), F32), pltpu.VMEM((2, dff, d), F32),
                            pltpu.VMEM((d, two_f), BF16), pltpu.VMEM((dff, d), BF16),
                            pltpu.SemaphoreType.DMA((2, 2))],
        ),
        out_shape=jax.ShapeDtypeStruct(xb.shape, F32),
        compiler_params=_cparams(("arbitrary",)),
        name="moe_experts",
    )(block_e, live, first, slot, next_e, xb, w1, b1.reshape(depth, n_e, 1, two_f), w2,
      b2.reshape(depth, n_e, 1, d))


def _run_piece_copies(src_ref, len_ref, dst_ref, run, yb_hbm, stage, sem):
    length = len_ref[run]
    src0 = src_ref[run]
    dst0 = dst_ref[run]
    copies = []
    piece = ROUTE_TT
    while piece >= 1:
        offset = length & ~(2 * piece - 1)
        src = pl.ds(pl.multiple_of((src0 + offset) * ROW_TILE, ROW_TILE), piece * ROW_TILE)
        dst = pl.ds(pl.multiple_of((dst0 + offset) * ROW_TILE, ROW_TILE), piece * ROW_TILE)
        copies.append(((length & piece) != 0, pltpu.make_async_copy(yb_hbm.at[src, :], stage.at[dst, :], sem)))
        piece //= 2
    return copies


def _combine_kernel(src_ref, len_ref, dst_ref, yb_hbm, loc_ref, gate_ref, x_ref, mod_ref, lng_ref, lnb_ref,
                    *rest, with_pw1):
    if with_pw1:
        modn_ref, wa_ref, wb_ref, ba_ref, bb_ref, o_ref, u_ref, stage, sem = rest
    else:
        o_ref, stage, sem = rest
    tt = x_ref.shape[0]
    step = pl.program_id(0)
    n_steps = pl.num_programs(0)

    def start_gather(tile, slot):
        def per_expert(e, carry):
            copies = _run_piece_copies(src_ref, len_ref, dst_ref, tile * N_EXPERTS + e, yb_hbm,
                                       stage.at[slot], sem.at[slot])
            for wanted, copy in copies:
                pl.when(wanted)(copy.start)
            return carry

        lax.fori_loop(0, N_EXPERTS, per_expert, 0)

    slot = step % 2

    @pl.when(step == 0)
    def _():
        start_gather(step, slot)

    @pl.when(step + 1 < n_steps)
    def _():
        start_gather(step + 1, 1 - slot)

    pltpu.make_async_copy(yb_hbm.at[pl.ds(0, TOP_K * tt * ROW_TILE), :], stage.at[slot], sem.at[slot]).wait()

    staged = _tiles_to_rows(stage.at[slot]).astype(BF16)
    loc = loc_ref[...]
    gates = gate_ref[...]
    pos = lax.broadcasted_iota(jnp.int32, (tt, TOP_K * tt), 1)
    pick = jnp.zeros((tt, TOP_K * tt), F32)
    for k in range(TOP_K):
        pick = jnp.where(pos == loc[:, k:k + 1], gates[:, k:k + 1], pick)
    y = jnp.dot(pick.astype(BF16), staged, preferred_element_type=F32)
    x_new = _residual_ln(x_ref[...], y, mod_ref[0, 5:6, :], lng_ref[...], lnb_ref[...])
    o_ref[...] = x_new
    if with_pw1:
        _pw1_glu_store(x_new, modn_ref, wa_ref, wb_ref, ba_ref, bb_ref, u_ref)


def _combine(pstarts, tile_runs, yb, loc, gates, x1, mod, ln_g, ln_b, seq, next_pw1=None):
    t, d = x1.shape
    tt = ROUTE_TT
    per_b = seq // tt
    run_src = (pstarts[None, :] + tile_runs[:, 0, :N_EXPERTS]).reshape(-1)
    run_len = tile_runs[:, 1, :N_EXPERTS].reshape(-1)
    run_dst = tile_runs[:, 2, :N_EXPERTS].reshape(-1)
    const = lambda i, *_: (0, 0)
    tile_map = lambda i, *_: (i, 0)
    in_specs = [
        pl.BlockSpec(memory_space=pl.ANY),
        pl.BlockSpec((tt, LANES), tile_map),
        pl.BlockSpec((tt, LANES), tile_map),
        pl.BlockSpec((tt, d), tile_map),
        pl.BlockSpec((1, 6, d), lambda i, *_: (i // per_b, 0, 0)),
        pl.BlockSpec((1, d), const),
        pl.BlockSpec((1, d), const),
    ]
    operands = [run_src, run_len, run_dst, yb, loc, gates, x1, mod, ln_g.reshape(1, d), ln_b.reshape(1, d)]
    out_specs = [pl.BlockSpec((tt, d), tile_map)]
    out_shape = [jax.ShapeDtypeStruct((t, d), F32)]
    if next_pw1 is not None:
        mod_next, pw1_w, pw1_b = next_pw1
        inner = pw1_w.shape[1] // 2
        assert inner == ROW_TILE * LANES
        w = pw1_w.astype(BF16)
        b = pw1_b.reshape(1, 2 * inner)
        in_specs += [
            pl.BlockSpec((1, 6, d), lambda i, *_: (i // per_b, 0, 0)),
            pl.BlockSpec((d, inner), const),
            pl.BlockSpec((d, inner), lambda i, *_: (0, 1)),
            pl.BlockSpec((1, inner), const),
            pl.BlockSpec((1, inner), lambda i, *_: (0, 1)),
        ]
        operands += [mod_next, w, w, b, b]
        out_specs.append(pl.BlockSpec((tt * ROW_TILE, LANES), tile_map))
        out_shape.append(jax.ShapeDtypeStruct((t * ROW_TILE, LANES), F32))
    return pl.pallas_call(
        functools.partial(_combine_kernel, with_pw1=next_pw1 is not None),
        grid_spec=pltpu.PrefetchScalarGridSpec(
            num_scalar_prefetch=3,
            grid=(t // tt,),
            in_specs=in_specs,
            out_specs=out_specs,
            scratch_shapes=[pltpu.VMEM((2, TOP_K * tt * ROW_TILE, LANES), F32), pltpu.SemaphoreType.DMA((2,))],
        ),
        out_shape=out_shape,
        compiler_params=_cparams(("arbitrary",)),
        name="moe_combine_ln",
    )(*operands)


def _moe_block(x1, h3, idx, gates, mod, layer, w1, b1, w2, b2, ln_g, ln_b, seq, next_pw1=None):
    t = x1.shape[0]
    rank_t, idx_t, loc, tile_runs, counts = _rank_and_count(idx)
    pstarts, block_e, live = _routing_tables(t, counts)
    dest = _dest_rows(pstarts, idx_t, rank_t)
    n_rows = _num_expert_blocks(t * TOP_K) * EXPERT_TM
    xb = _dispatch(dest, live, h3, n_rows)
    yb = _experts(block_e, live, xb, layer, w1, b1, w2, b2)
    return _combine(pstarts, tile_runs, yb, loc, gates, x1, mod, ln_g, ln_b, seq, next_pw1)


def kernel(x, c, ada_w, ada_b, dn_in_w, dn_conv_w, dn_A_log, dn_dt_bias, dn_onorm_w, dn_out_w,
           cf_pw1_w, cf_pw1_b, cf_dw_w, cf_dw_b, cf_ln_g, cf_ln_b, cf_pw2_w, cf_pw2_b,
           ln1_g, ln1_b, router_w, router_b, e_w1, e_b1, e_w2, e_b2, ln2_g, ln2_b):
    bsz, seq, d = x.shape
    mods = _ada_ln(c, ada_w, ada_b)

    proj, ba = _dn_in_proj(x, mods[0], dn_in_w[0])
    qn, kn, vv, gates_dn = _dn_prep(proj, ba, dn_conv_w[0], dn_A_log[0], dn_dt_bias[0])
    o = _dn_chunk(qn, kn, vv, proj, gates_dn, dn_onorm_w[0])
    x1, h3, idx, gates = _dn_out(o, dn_out_w[0], x, mods[0], ln1_g[0], ln1_b[0], router_w[0], router_b[0])
    x2, u = _moe_block(x1, h3, idx, gates, mods[0], 0, e_w1, e_b1, e_w2, e_b2, ln2_g[0], ln2_b[0], seq,
                       next_pw1=(mods[1], cf_pw1_w[0], cf_pw1_b[0]))

    x2 = x2.reshape(bsz, seq, d)
    x3, h3, idx, gates = _cf_tail(u, cf_dw_w[0], cf_dw_b[0], cf_ln_g[0], cf_ln_b[0], cf_pw2_w[0], cf_pw2_b[0],
                                  x2, mods[1], ln1_g[1], ln1_b[1], router_w[1], router_b[1])
    (x4,) = _moe_block(x3, h3, idx, gates, mods[1], 1, e_w1, e_b1, e_w2, e_b2, ln2_g[1], ln2_b[1], seq)
    return x4.reshape(bsz, seq, d)
```

```python
import functools

import jax
import jax.numpy as jnp
from jax import lax
from jax.experimental import pallas as pl
from jax.experimental.pallas import tpu as pltpu

F32 = jnp.float32
BF16 = jnp.bfloat16
HIGHEST = lax.Precision.HIGHEST

DEPTH = 2
DN_QK_HEADS = 8
DN_V_HEADS = 16
DN_HEAD_DIM = 128
DN_CONV = 4
DN_CHUNK = 64
CF_KERNEL = 31
N_EXPERTS = 32
TOP_K = 4
SWIGLU_LIMIT = 7.0
SWIGLU_ALPHA = 1.702
LN_EPS = 1e-5
RMS_EPS = 1e-6
L2_EPS = 1e-6
DEEPNORM_ALPHA = (2 * DEPTH) ** 0.25

LANES = 128
SUBLANES = 8
BF16_SUBLANES = 16
VMEM_LIMIT = 56 * 1024 * 1024

ADA_TN = 1536
PROJ_TM = 1024
PROJ_TN = 2048
PREP_TS = 512
CHUNKS_PER_STEP = 4
DN_CHUNK_UNROLL = 2
POST_TM = 512
ROUTE_TT = 256
EXPERT_TM = 512
CF_TM = 512
CF_HALO = 32
CF_CONV_ROWS = 32

NEG_BIG = -1e30


def _cparams(sem):
    return pltpu.CompilerParams(dimension_semantics=sem, vmem_limit_bytes=VMEM_LIMIT)


def _sigmoid(x):
    return jax.nn.sigmoid(x)


def _layer_norm(v, g, b):
    mu = jnp.mean(v, -1, keepdims=True)
    d = v - mu
    var = jnp.mean(d * d, -1, keepdims=True)
    return d * lax.rsqrt(var + LN_EPS) * g + b


ROW_TILE = SUBLANES


def _rows_to_tiles(o_ref, val):
    rows = val.shape[0]
    for s in range(ROW_TILE):
        o_ref[pl.ds(s, rows, stride=ROW_TILE), :] = val[:, s * LANES:(s + 1) * LANES]


def _tiles_to_rows(x_ref):
    rows = x_ref.shape[0] // ROW_TILE
    return jnp.concatenate([x_ref[pl.ds(s, rows, stride=ROW_TILE), :] for s in range(ROW_TILE)], axis=1)


def _split_weight(w):
    k, n = w.shape
    hi = w.astype(BF16)
    lo = (w - hi.astype(F32)).astype(BF16)
    out = jnp.zeros((k, 2 * LANES), BF16)
    return out.at[:, :n].set(hi).at[:, LANES:LANES + n].set(lo)


def _narrow_dot(h, w_split_ref):
    m = h.shape[0]
    hi = h.astype(BF16)
    lo = (h - hi.astype(F32)).astype(BF16)
    out = jnp.dot(jnp.concatenate([hi, lo], axis=0), w_split_ref[...], preferred_element_type=F32)
    return out[:m, :LANES] + out[:m, LANES:] + out[m:, :LANES]


def _ada_kernel(c_ref, w_ref, b_ref, o_ref):
    c = c_ref[...]
    cond = c * _sigmoid(c)
    o_ref[0] = jnp.dot(cond, w_ref[0], precision=HIGHEST, preferred_element_type=F32) + b_ref[0]


def _ada_ln(c, ada_w, ada_b):
    depth, d, n = ada_w.shape
    bsz = c.shape[0]
    c_pad = jnp.zeros((SUBLANES, d), F32).at[:bsz].set(c)
    out = pl.pallas_call(
        _ada_kernel,
        grid=(depth, n // ADA_TN),
        in_specs=[
            pl.BlockSpec((SUBLANES, d), lambda i, j: (0, 0)),
            pl.BlockSpec((1, d, ADA_TN), lambda i, j: (i, 0, j)),
            pl.BlockSpec((1, 1, ADA_TN), lambda i, j: (i, 0, j)),
        ],
        out_specs=pl.BlockSpec((1, SUBLANES, ADA_TN), lambda i, j: (i, 0, j)),
        out_shape=jax.ShapeDtypeStruct((depth, SUBLANES, n), F32),
        compiler_params=_cparams(("parallel", "parallel")),
        name="ada_ln",
    )(c_pad, ada_w, ada_b.reshape(depth, 1, n))
    return out[:, :bsz].reshape(depth, bsz, 6, d)


def _inproj_kernel(x_ref, mod_ref, w_ref, wba_ref, proj_ref, ba_ref, h_scr):
    @pl.when(pl.program_id(2) == 0)
    def _():
        h = x_ref[0] * (1.0 + mod_ref[0, 1:2, :]) + mod_ref[0, 0:1, :]
        h_scr[...] = h.astype(BF16)
        ba_ref[0] = _narrow_dot(h, wba_ref)

    proj_ref[0] = jnp.dot(h_scr[...], w_ref[...], preferred_element_type=F32).astype(BF16)


def _dn_in_proj(x, mod, in_w):
    bsz, seq, d = x.shape
    n_main = in_w.shape[1] - 2 * DN_V_HEADS
    w_main = in_w[:, :n_main].astype(BF16)
    w_ba = _split_weight(in_w[:, n_main:])
    tm = min(PROJ_TM, seq)
    return pl.pallas_call(
        _inproj_kernel,
        grid=(bsz, seq // tm, n_main // PROJ_TN),
        in_specs=[
            pl.BlockSpec((1, tm, d), lambda b, i, j: (b, i, 0)),
            pl.BlockSpec((1, 6, d), lambda b, i, j: (b, 0, 0)),
            pl.BlockSpec((d, PROJ_TN), lambda b, i, j: (0, j)),
            pl.BlockSpec((d, 2 * LANES), lambda b, i, j: (0, 0)),
        ],
        out_specs=[
            pl.BlockSpec((1, tm, PROJ_TN), lambda b, i, j: (b, i, j)),
            pl.BlockSpec((1, tm, LANES), lambda b, i, j: (b, i, 0)),
        ],
        out_shape=[
            jax.ShapeDtypeStruct((bsz, seq, n_main), BF16),
            jax.ShapeDtypeStruct((bsz, seq, LANES), F32),
        ],
        scratch_shapes=[pltpu.VMEM((tm, d), BF16)],
        compiler_params=_cparams(("parallel", "parallel", "arbitrary")),
        name="dn_in_proj",
    )(x, mod, w_main, w_ba)


def _dn_prep_kernel(q_ref, k_ref, v_ref, qh_ref, kh_ref, vh_ref, cw_ref, ba_ref, alog_ref, dt_ref,
                    qo_ref, ko_ref, vo_ref, g_ref, scr):
    ts = q_ref.shape[1]
    halo = qh_ref.shape[1]
    keep = (pl.program_id(1) > 0).astype(F32)

    def conv_silu(x_ref, h_ref, c0):
        width = x_ref.shape[2]
        scr[0:halo, 0:width] = h_ref[0].astype(F32) * keep
        scr[halo:halo + ts, 0:width] = x_ref[0].astype(F32)
        acc = None
        for j in range(DN_CONV):
            off = halo - (DN_CONV - 1) + j
            term = scr[off:off + ts, 0:width] * cw_ref[j:j + 1, c0:c0 + width]
            acc = term if acc is None else acc + term
        return acc * _sigmoid(acc)

    def l2norm_store(o_ref, x, scale):
        for h in range(x.shape[1] // DN_HEAD_DIM):
            xh = x[:, h * DN_HEAD_DIM:(h + 1) * DN_HEAD_DIM]
            ss = jnp.sum(xh * xh, -1, keepdims=True)
            o_ref[0, :, h * DN_HEAD_DIM:(h + 1) * DN_HEAD_DIM] = (
                xh * lax.rsqrt(ss + L2_EPS) * scale).astype(o_ref.dtype)

    kd = q_ref.shape[2]
    l2norm_store(qo_ref, conv_silu(q_ref, qh_ref, 0), DN_HEAD_DIM ** -0.5)
    l2norm_store(ko_ref, conv_silu(k_ref, kh_ref, kd), 1.0)
    vo_ref[0] = conv_silu(v_ref, vh_ref, 2 * kd).astype(vo_ref.dtype)

    ba = ba_ref[0]
    beta = _sigmoid(ba)
    zz = ba + dt_ref[...]
    softplus = jnp.maximum(zz, 0.0) + jnp.log1p(jnp.exp(-jnp.abs(zz)))
    g = -jnp.exp(alog_ref[...]) * softplus
    r = lax.broadcasted_iota(jnp.int32, (ts, ts), 0)
    c = lax.broadcasted_iota(jnp.int32, (ts, ts), 1)
    in_chunk_tril = ((r // DN_CHUNK == c // DN_CHUNK) & (c <= r)).astype(BF16)
    g1 = g.astype(BF16)
    g2 = (g - g1.astype(F32)).astype(BF16)
    g3 = (g - g1.astype(F32) - g2.astype(F32)).astype(BF16)
    parts = jnp.dot(in_chunk_tril, jnp.concatenate([g1, g2, g3], axis=1), preferred_element_type=F32)
    gcum = parts[:, :LANES] + parts[:, LANES:2 * LANES] + parts[:, 2 * LANES:]
    lane = lax.broadcasted_iota(jnp.int32, ba.shape, 1)
    g_ref[0] = jnp.where(lane < DN_V_HEADS, beta, gcum)


def _dn_prep(proj, ba, conv_w, a_log, dt_bias):
    bsz, seq, _ = proj.shape
    kd = DN_QK_HEADS * DN_HEAD_DIM
    vd = DN_V_HEADS * DN_HEAD_DIM
    ts = min(PREP_TS, seq)
    halo = BF16_SUBLANES
    hb = ts // halo
    alog_row = jnp.zeros((1, LANES), F32).at[0, DN_V_HEADS:2 * DN_V_HEADS].set(a_log)
    dt_row = jnp.zeros((1, LANES), F32).at[0, DN_V_HEADS:2 * DN_V_HEADS].set(dt_bias)

    def halo_map(col):
        return lambda b, i: (b, jnp.maximum(i * hb - 1, 0), col)

    return pl.pallas_call(
        _dn_prep_kernel,
        grid=(bsz, seq // ts),
        in_specs=[
            pl.BlockSpec((1, ts, kd), lambda b, i: (b, i, 0)),
            pl.BlockSpec((1, ts, kd), lambda b, i: (b, i, 1)),
            pl.BlockSpec((1, ts, vd), lambda b, i: (b, i, 1)),
            pl.BlockSpec((1, halo, kd), halo_map(0)),
            pl.BlockSpec((1, halo, kd), halo_map(1)),
            pl.BlockSpec((1, halo, vd), halo_map(1)),
            pl.BlockSpec((DN_CONV, 2 * kd + vd), lambda b, i: (0, 0)),
            pl.BlockSpec((1, ts, LANES), lambda b, i: (b, i, 0)),
            pl.BlockSpec((1, LANES), lambda b, i: (0, 0)),
            pl.BlockSpec((1, LANES), lambda b, i: (0, 0)),
        ],
        out_specs=[
            pl.BlockSpec((1, ts, kd), lambda b, i: (b, i, 0)),
            pl.BlockSpec((1, ts, kd), lambda b, i: (b, i, 0)),
            pl.BlockSpec((1, ts, vd), lambda b, i: (b, i, 0)),
            pl.BlockSpec((1, ts, LANES), lambda b, i: (b, i, 0)),
        ],
        out_shape=[
            jax.ShapeDtypeStruct((bsz, seq, kd), BF16),
            jax.ShapeDtypeStruct((bsz, seq, kd), BF16),
            jax.ShapeDtypeStruct((bsz, seq, vd), BF16),
            jax.ShapeDtypeStruct((bsz, seq, LANES), F32),
        ],
        scratch_shapes=[pltpu.VMEM((ts + halo, vd), F32)],
        compiler_params=_cparams(("parallel", "arbitrary")),
        name="dn_prep",
    )(proj, proj, proj, proj, proj, proj, conv_w, ba, alog_row, dt_row)


def _bmm(a, b):
    return lax.dot_general(a.astype(BF16), b.astype(BF16), (((2,), (1,)), ((0,), (0,))),
                           preferred_element_type=F32)


def _bmm_nt(a, b):
    return lax.dot_general(a.astype(BF16), b.astype(BF16), (((2,), (2,)), ((0,), (0,))),
                           preferred_element_type=F32)


def _bmm_tn(a, b):
    return lax.dot_general(a.astype(BF16), b.astype(BF16), (((1,), (1,)), ((0,), (0,))),
                           preferred_element_type=F32)


def _unit_lower_inverse_wide(a_twice, upper, eye_upper):
    c = a_twice.shape[1]
    x = jnp.where(upper, eye_upper, -a_twice)
    span = 1
    while span < c:
        x = _bmm(x[:, :, :c], x) + jnp.where(upper, x, 0.0)
        span *= 2
    return x


def _dn_chunk_kernel(q_ref, k_ref, v_ref, z_ref, g_ref, gt_ref, ow_ref, o_ref, s_ref):
    c_len = DN_CHUNK
    dh = DN_HEAD_DIM
    rep = DN_V_HEADS // DN_QK_HEADS

    @pl.when(pl.program_id(1) == 0)
    def _():
        s_ref[...] = jnp.zeros(s_ref.shape, F32)

    heads = range(DN_V_HEADS)
    n_heads = DN_V_HEADS
    ri = lax.broadcasted_iota(jnp.int32, (1, c_len, 2 * c_len), 1)
    lane = lax.broadcasted_iota(jnp.int32, (1, c_len, 2 * c_len), 2)
    upper = lane >= c_len
    ci = jnp.where(upper, lane - c_len, lane)
    causal = ci <= ri
    strict = ci < ri
    eye_upper = ((ci == ri) & upper).astype(F32)
    onorm = ow_ref[...]

    def head_cols(ref, rows, h):
        return ref[0, rows, h * dh:(h + 1) * dh]

    def per_v_head(t):
        return jnp.stack([t[h // rep] for h in heads])

    def state_free_part(c):
        r0 = pl.multiple_of(c * c_len, c_len)
        rows = pl.ds(r0, c_len)
        gcols = g_ref[0, rows, :]
        grows = gt_ref[0, c]
        wide = (n_heads, c_len, dh)
        beta_b = jnp.broadcast_to(jnp.stack([gcols[:, h:h + 1] for h in heads]), wide)
        g_b = jnp.broadcast_to(
            jnp.stack([gcols[:, DN_V_HEADS + h:DN_V_HEADS + h + 1] for h in heads]), wide)
        g_r = jnp.stack([grows[DN_V_HEADS + h:DN_V_HEADS + h + 1, :] for h in heads])
        g_r = jnp.concatenate([g_r, g_r], axis=2)
        g_last = g_b[:, c_len - 1:c_len, :]

        qn = jnp.stack([head_cols(q_ref, rows, hq) for hq in range(DN_QK_HEADS)])
        kn = jnp.stack([head_cols(k_ref, rows, hq) for hq in range(DN_QK_HEADS)])
        qk_kk = _bmm_nt(jnp.concatenate([qn, kn], axis=1), jnp.concatenate([kn, kn], axis=1))
        qk = per_v_head(qk_kk[:, :c_len])
        kk = per_v_head(qk_kk[:, c_len:])
        qf = per_v_head(qn).astype(F32)
        kf = per_v_head(kn).astype(F32)
        vf = jnp.stack([head_cols(v_ref, rows, h) for h in heads]).astype(F32)

        decay = jnp.where(causal, jnp.exp(g_b - g_r), 0.0)
        a_twice = jnp.where(strict, beta_b * kk * decay, 0.0)
        x_inv = _unit_lower_inverse_wide(a_twice, upper, eye_upper)
        eg = jnp.exp(g_b)
        rhs = jnp.concatenate([vf * beta_b, kf * (beta_b * eg)], axis=2)
        sol = _bmm(x_inv, jnp.concatenate([jnp.zeros_like(rhs), rhs], axis=1))
        u = sol[:, :, :dh]
        w = sol[:, :, dh:]
        qkm = jnp.where(causal, qk * decay, 0.0)[:, :, :c_len]
        qg = qf * eg
        kdec = kf * jnp.exp(g_last - g_b)
        return rows, jnp.concatenate([w, qg], axis=1), u, qkm, kdec, jnp.exp(g_last)

    def state_part(rows, w_qg, u, qkm, kdec, chunk_decay):
        state = s_ref[...]
        ws = _bmm(w_qg, state)
        v_new = u - ws[:, :c_len]
        o = ws[:, c_len:] + _bmm(qkm, v_new)
        s_ref[...] = state * chunk_decay + _bmm_tn(kdec, v_new)

        o = o * lax.rsqrt(jnp.mean(o * o, -1, keepdims=True) + RMS_EPS) * onorm
        for h in heads:
            zf = head_cols(z_ref, rows, h).astype(F32)
            o_ref[0, rows, h * dh:(h + 1) * dh] = (o[h] * (zf * _sigmoid(zf))).astype(o_ref.dtype)

    def chunk_group(i, carry):
        prepared = [state_free_part(i * DN_CHUNK_UNROLL + j) for j in range(DN_CHUNK_UNROLL)]
        for args in prepared:
            state_part(*args)
        return carry

    lax.fori_loop(0, q_ref.shape[1] // (c_len * DN_CHUNK_UNROLL), chunk_group, 0)


def _dn_chunk(qn, kn, vv, proj, gates, onorm_w):
    bsz, seq, kd = qn.shape
    vd = vv.shape[2]
    n_chunks = seq // DN_CHUNK
    cb = min(CHUNKS_PER_STEP, n_chunks)
    rows = cb * DN_CHUNK
    gates_t = jnp.swapaxes(gates[:, :, :2 * DN_V_HEADS].reshape(bsz, n_chunks, DN_CHUNK, 2 * DN_V_HEADS), 2, 3)
    z_col = (2 * kd + vd) // vd
    return pl.pallas_call(
        _dn_chunk_kernel,
        grid=(bsz, n_chunks // cb),
        in_specs=[
            pl.BlockSpec((1, rows, kd), lambda b, n: (b, n, 0)),
            pl.BlockSpec((1, rows, kd), lambda b, n: (b, n, 0)),
            pl.BlockSpec((1, rows, vd), lambda b, n: (b, n, 0)),
            pl.BlockSpec((1, rows, vd), lambda b, n: (b, n, z_col)),
            pl.BlockSpec((1, rows, LANES), lambda b, n: (b, n, 0)),
            pl.BlockSpec((1, cb, 2 * DN_V_HEADS, DN_CHUNK), lambda b, n: (b, n, 0, 0)),
            pl.BlockSpec((1, DN_HEAD_DIM), lambda b, n: (0, 0)),
        ],
        out_specs=pl.BlockSpec((1, rows, vd), lambda b, n: (b, n, 0)),
        out_shape=jax.ShapeDtypeStruct((bsz, seq, vd), BF16),
        scratch_shapes=[pltpu.VMEM((DN_V_HEADS, DN_HEAD_DIM, DN_HEAD_DIM), F32)],
        compiler_params=_cparams(("parallel", "arbitrary")),
        name="dn_chunk",
    )(qn, kn, vv, proj, gates, gates_t, onorm_w.reshape(1, DN_HEAD_DIM))


def _residual_ln(x, y, gate_row, ln_g, ln_b):
    return _layer_norm(DEEPNORM_ALPHA * x + (1.0 + gate_row) * y, ln_g, ln_b)


def _route_store(h, rw_ref, rb_ref, idx_ref, gate_ref):
    logits = _narrow_dot(h, rw_ref) + rb_ref[...]
    lane = lax.broadcasted_iota(jnp.int32, logits.shape, 1).astype(F32)
    work = logits
    idx_out = jnp.zeros(logits.shape, F32)
    val_out = jnp.full(logits.shape, NEG_BIG, F32)
    for k in range(TOP_K):
        m = jnp.max(work, -1, keepdims=True)
        am = jnp.min(jnp.where(work == m, lane, float(LANES)), -1, keepdims=True)
        idx_out = jnp.where(lane == k, am, idx_out)
        val_out = jnp.where(lane == k, m, val_out)
        work = jnp.where(lane == am, NEG_BIG * 2.0, work)
    top = jnp.max(val_out, -1, keepdims=True)
    e = jnp.where(lane < TOP_K, jnp.exp(val_out - top), 0.0)
    idx_ref[...] = idx_out.astype(jnp.int32)
    gate_ref[...] = e / jnp.sum(e, -1, keepdims=True)


def _post_mixer_tail(x, y, mod_ref, lng_ref, lnb_ref, rw_ref, rb_ref, x1_ref, h3_ref, idx_ref, gate_ref):
    x1 = _residual_ln(x, y, mod_ref[0, 2:3, :], lng_ref[...], lnb_ref[...])
    x1_ref[...] = x1
    h2 = x1 * (1.0 + mod_ref[0, 4:5, :]) + mod_ref[0, 3:4, :]
    _rows_to_tiles(h3_ref, h2)
    _route_store(h2, rw_ref, rb_ref, idx_ref, gate_ref)


def _router_operands(router_w, router_b):
    rb = jnp.full((1, LANES), NEG_BIG, F32).at[0, :N_EXPERTS].set(router_b)
    return _split_weight(router_w), rb


def _post_out_specs(tm, d):
    return [
        pl.BlockSpec((tm, d), lambda i: (i, 0)),
        pl.BlockSpec((tm * ROW_TILE, LANES), lambda i: (i, 0)),
        pl.BlockSpec((tm, LANES), lambda i: (i, 0)),
        pl.BlockSpec((tm, LANES), lambda i: (i, 0)),
    ]


def _post_out_shapes(t, d):
    assert d == ROW_TILE * LANES
    return [
        jax.ShapeDtypeStruct((t, d), F32),
        jax.ShapeDtypeStruct((t * ROW_TILE, LANES), F32),
        jax.ShapeDtypeStruct((t, LANES), jnp.int32),
        jax.ShapeDtypeStruct((t, LANES), F32),
    ]


def _dn_out_kernel(o_ref, w_ref, x_ref, mod_ref, lng_ref, lnb_ref, rw_ref, rb_ref,
                   x1_ref, h3_ref, idx_ref, gate_ref):
    y = jnp.dot(o_ref[...], w_ref[...], preferred_element_type=F32)
    _post_mixer_tail(x_ref[...], y, mod_ref, lng_ref, lnb_ref, rw_ref, rb_ref,
                     x1_ref, h3_ref, idx_ref, gate_ref)


def _dn_out(o, out_w, x, mod, ln_g, ln_b, router_w, router_b):
    bsz, seq, d = x.shape
    t = bsz * seq
    vd = o.shape[2]
    tm = min(POST_TM, seq)
    rw, rb = _router_operands(router_w, router_b)
    per_b = seq // tm
    return pl.pallas_call(
        _dn_out_kernel,
        grid=(t // tm,),
        in_specs=[
            pl.BlockSpec((tm, vd), lambda i: (i, 0)),
            pl.BlockSpec((vd, d), lambda i: (0, 0)),
            pl.BlockSpec((tm, d), lambda i: (i, 0)),
            pl.BlockSpec((1, 6, d), lambda i: (i // per_b, 0, 0)),
            pl.BlockSpec((1, d), lambda i: (0, 0)),
            pl.BlockSpec((1, d), lambda i: (0, 0)),
            pl.BlockSpec((d, 2 * LANES), lambda i: (0, 0)),
            pl.BlockSpec((1, LANES), lambda i: (0, 0)),
        ],
        out_specs=_post_out_specs(tm, d),
        out_shape=_post_out_shapes(t, d),
        compiler_params=_cparams(("parallel",)),
        name="dn_out_ln_route",
    )(o.reshape(t, vd), out_w.astype(BF16), x.reshape(t, d), mod, ln_g.reshape(1, d), ln_b.reshape(1, d), rw, rb)


def _pw1_glu_store(x, mod_ref, wa_ref, wb_ref, ba_ref, bb_ref, u_ref):
    h = (x * (1.0 + mod_ref[0, 1:2, :]) + mod_ref[0, 0:1, :]).astype(BF16)
    pa = jnp.dot(h, wa_ref[...], preferred_element_type=F32) + ba_ref[...]
    pb = jnp.dot(h, wb_ref[...], preferred_element_type=F32) + bb_ref[...]
    _rows_to_tiles(u_ref, pa * _sigmoid(pb))


def _cf_tail_kernel(u_ref, uh_ref, dw_ref, dwb_ref, cg_ref, cb_ref, w2_ref, b2_ref,
                    x_ref, mod_ref, lng_ref, lnb_ref, rw_ref, rb_ref,
                    x1_ref, h3_ref, idx_ref, gate_ref, scr, conv_scr):
    tm = u_ref.shape[0] // ROW_TILE
    halo = uh_ref.shape[0] // ROW_TILE
    keep = (pl.program_id(1) > 0).astype(F32)
    scr[0:halo * ROW_TILE, :] = uh_ref[...] * keep
    scr[halo * ROW_TILE:(halo + tm) * ROW_TILE, :] = u_ref[...]
    first = halo - (CF_KERNEL - 1)
    blk = CF_CONV_ROWS * ROW_TILE

    def conv_tokens(tb, carry):
        r0 = pl.multiple_of(tb * blk, blk)
        acc = jnp.concatenate([dwb_ref[...]] * CF_CONV_ROWS, axis=0)
        for j in range(CF_KERNEL):
            w_tile = dw_ref[j * ROW_TILE:(j + 1) * ROW_TILE, :]
            w_blk = jnp.concatenate([w_tile] * CF_CONV_ROWS, axis=0)
            acc = acc + scr[pl.ds(r0 + (first + j) * ROW_TILE, blk), :] * w_blk
        conv_scr[pl.ds(r0, blk), :] = acc
        return carry

    lax.fori_loop(0, tm // CF_CONV_ROWS, conv_tokens, 0)
    conv = _tiles_to_rows(conv_scr)
    normed = _layer_norm(conv, cg_ref[...], cb_ref[...])
    act = normed * _sigmoid(normed)
    y = jnp.dot(act.astype(BF16), w2_ref[...], preferred_element_type=F32) + b2_ref[...]
    _post_mixer_tail(x_ref[0], y, mod_ref, lng_ref, lnb_ref, rw_ref, rb_ref,
                     x1_ref, h3_ref, idx_ref, gate_ref)


def _cf_tail(u, dw_w, dw_b, cf_ln_g, cf_ln_b, pw2_w, pw2_b, x, mod, ln_g, ln_b, router_w, router_b):
    bsz, seq, d = x.shape
    inner = dw_w.shape[1]
    t = bsz * seq
    tm = min(CF_TM, seq)
    per_b = seq // tm
    hb = tm // CF_HALO
    halos_per_b = seq // CF_HALO
    rw, rb = _router_operands(router_w, router_b)
    row = lambda v: v.reshape(1, -1)
    tiles = lambda v: v.reshape(-1, LANES)
    const = lambda b, i: (0, 0)
    flat = lambda b, i: (b * per_b + i, 0)
    return pl.pallas_call(
        _cf_tail_kernel,
        grid=(bsz, per_b),
        in_specs=[
            pl.BlockSpec((tm * ROW_TILE, LANES), flat),
            pl.BlockSpec((CF_HALO * ROW_TILE, LANES),
                         lambda b, i: (b * halos_per_b + jnp.maximum(i * hb - 1, 0), 0)),
            pl.BlockSpec((CF_KERNEL * ROW_TILE, LANES), const),
            pl.BlockSpec((ROW_TILE, LANES), const),
            pl.BlockSpec((1, inner), const),
            pl.BlockSpec((1, inner), const),
            pl.BlockSpec((inner, d), const),
            pl.BlockSpec((1, d), const),
            pl.BlockSpec((1, tm, d), lambda b, i: (b, i, 0)),
            pl.BlockSpec((1, 6, d), lambda b, i: (b, 0, 0)),
            pl.BlockSpec((1, d), const),
            pl.BlockSpec((1, d), const),
            pl.BlockSpec((d, 2 * LANES), const),
            pl.BlockSpec((1, LANES), const),
        ],
        out_specs=[
            pl.BlockSpec((tm, d), flat),
            pl.BlockSpec((tm * ROW_TILE, LANES), flat),
            pl.BlockSpec((tm, LANES), flat),
            pl.BlockSpec((tm, LANES), flat),
        ],
        out_shape=_post_out_shapes(t, d),
        scratch_shapes=[pltpu.VMEM(((tm + CF_HALO) * ROW_TILE, LANES), F32),
                        pltpu.VMEM((tm * ROW_TILE, LANES), F32)],
        compiler_params=_cparams(("parallel", "arbitrary")),
        name="cf_conv_ln_pw2_route",
    )(u, u, tiles(dw_w), tiles(dw_b), row(cf_ln_g), row(cf_ln_b), pw2_w.astype(BF16), row(pw2_b),
      x, mod, row(ln_g), row(ln_b), rw, rb)


def _rank_kernel(idx_ref, loc_ref, tile_ref, cnt_ref, carry):
    tt = idx_ref.shape[0]

    @pl.when(pl.program_id(0) == 0)
    def _():
        carry[...] = jnp.zeros(carry.shape, F32)

    idx = idx_ref[...]
    lane = lax.broadcasted_iota(jnp.int32, idx.shape, 1)
    sel = [lane == idx[:, k:k + 1] for k in range(TOP_K)]
    multi_hot = sel[0]
    for k in range(1, TOP_K):
        multi_hot = multi_hot | sel[k]
    mh = multi_hot.astype(BF16)
    r = lax.broadcasted_iota(jnp.int32, (tt, tt), 0)
    c = lax.broadcasted_iota(jnp.int32, (tt, tt), 1)
    before_in_tile = jnp.dot((c < r).astype(BF16), mh, preferred_element_type=F32)
    before_tile = carry[...]
    run_len = jnp.sum(mh.astype(F32), 0, keepdims=True)
    e_from = lax.broadcasted_iota(jnp.int32, (LANES, LANES), 0)
    e_to = lax.broadcasted_iota(jnp.int32, (LANES, LANES), 1)
    run_start = jnp.dot(jnp.broadcast_to(run_len, (SUBLANES, LANES)).astype(BF16),
                        (e_from < e_to).astype(BF16), preferred_element_type=F32)[:1]
    loc = jnp.zeros(idx.shape, F32)
    for k in range(TOP_K):
        lk = jnp.sum(jnp.where(sel[k], before_in_tile + run_start, 0.0), -1, keepdims=True)
        loc = jnp.where(lane == k, lk, loc)
    loc_ref[...] = loc.astype(jnp.int32)
    row = lax.broadcasted_iota(jnp.int32, (SUBLANES, LANES), 0)
    tile_ref[0] = jnp.where(row == 0, before_tile, jnp.where(row == 1, run_len, run_start)).astype(jnp.int32)
    total = before_tile + run_len
    carry[...] = total
    cnt_ref[...] = total.astype(jnp.int32)


def _rank_and_count(idx):
    t = idx.shape[0]
    tt = ROUTE_TT
    assert t % tt == 0 and tt <= 256
    return pl.pallas_call(
        _rank_kernel,
        grid=(t // tt,),
        in_specs=[pl.BlockSpec((tt, LANES), lambda i: (i, 0))],
        out_specs=[
            pl.BlockSpec((tt, LANES), lambda i: (i, 0)),
            pl.BlockSpec((1, SUBLANES, LANES), lambda i: (i, 0, 0)),
            pl.BlockSpec((1, LANES), lambda i: (0, 0)),
        ],
        out_shape=[
            jax.ShapeDtypeStruct((t, LANES), jnp.int32),
            jax.ShapeDtypeStruct((t // tt, SUBLANES, LANES), jnp.int32),
            jax.ShapeDtypeStruct((1, LANES), jnp.int32),
        ],
        scratch_shapes=[pltpu.VMEM((1, LANES), F32)],
        compiler_params=_cparams(("arbitrary",)),
        name="moe_rank_count",
    )(idx)


def _num_expert_blocks(n_assign):
    bound = n_assign + N_EXPERTS * (EXPERT_TM - 1)
    return -(-bound // EXPERT_TM)


def _routing_tables(n_tokens, counts):
    nb = _num_expert_blocks(n_tokens * TOP_K)
    counts = counts[0, :N_EXPERTS]
    padded = (counts + EXPERT_TM - 1) // EXPERT_TM * EXPERT_TM
    pends = jnp.cumsum(padded)
    pstarts = (pends - padded).astype(jnp.int32)
    block_row0 = jnp.arange(nb, dtype=jnp.int32) * EXPERT_TM
    segments_done = jnp.sum((pends[None, :] <= block_row0[:, None]).astype(jnp.int32), axis=1)
    block_e = jnp.minimum(segments_done, N_EXPERTS - 1)
    r0 = block_row0[:, None]
    in_segment = (pstarts[None, :] <= r0) & (r0 < pends[None, :])
    live_if = jnp.clip(pstarts[None, :] + counts[None, :] - r0, 0, EXPERT_TM)
    live = jnp.sum(jnp.where(in_segment, live_if, 0), axis=1).astype(jnp.int32)
    return pstarts, block_e, live


def _expert_run_tables(block_e):
    nb = block_e.shape[0]
    pos = jnp.arange(nb, dtype=jnp.int32)
    first = jnp.concatenate([jnp.ones((1,), jnp.int32), (block_e[1:] != block_e[:-1]).astype(jnp.int32)])
    slot = (jnp.cumsum(first) - 1) % 2
    start_at_or_after = lax.cummin(jnp.where(first == 1, pos, nb), reverse=True)
    next_start = jnp.concatenate([start_at_or_after[1:], jnp.full((1,), nb, jnp.int32)])
    picks = next_start[:, None] == pos[None, :]
    next_e = jnp.where(next_start < nb, jnp.sum(jnp.where(picks, block_e[None, :], 0), axis=1), -1)
    return first, slot.astype(jnp.int32), next_e.astype(jnp.int32)


def _pad_fill_copies(live_ref, i, zeros, xb_hbm, sem):
    lv = live_ref[i]
    pad = EXPERT_TM - lv
    row0 = i * EXPERT_TM + lv
    copies = []
    piece = EXPERT_TM
    while piece >= 1:
        offset = pad & ~(2 * piece - 1)
        start = pl.multiple_of((row0 + offset) * ROW_TILE, ROW_TILE)
        copy = pltpu.make_async_copy(zeros.at[pl.ds(0, piece * ROW_TILE), :],
                                     xb_hbm.at[pl.ds(start, piece * ROW_TILE), :], sem)
        copies.append(((pad & piece) != 0, copy))
        piece //= 2
    return copies


def _run_piece_copies(src_ref, len_ref, dst_ref, run, sorted_hbm, stage, sem, to_stage):
    length = len_ref[run]
    src0 = src_ref[run]
    dst0 = dst_ref[run]
    copies = []
    piece = ROUTE_TT
    while piece >= 1:
        offset = length & ~(2 * piece - 1)
        hbm_rows = pl.ds(pl.multiple_of((src0 + offset) * ROW_TILE, ROW_TILE), piece * ROW_TILE)
        stage_rows = pl.ds(pl.multiple_of((dst0 + offset) * ROW_TILE, ROW_TILE), piece * ROW_TILE)
        ends = (sorted_hbm.at[hbm_rows, :], stage.at[stage_rows, :])
        copy = pltpu.make_async_copy(*(ends if to_stage else ends[::-1]), sem)
        copies.append(((length & piece) != 0, copy))
        piece //= 2
    return copies


def _dispatch_kernel(src_ref, len_ref, dst_ref, live_ref, h_ref, loc_ref, xb_hbm, stage, zeros, sem, fill_sem):
    tt = loc_ref.shape[0]
    step = pl.program_id(0)
    n_steps = pl.num_programs(0)
    slot = step % 2
    n_blocks = xb_hbm.shape[0] // (EXPERT_TM * ROW_TILE)
    whole_tile = pl.ds(0, TOP_K * tt * ROW_TILE)

    @pl.when(step == 0)
    def _():
        zeros[...] = jnp.zeros(zeros.shape, F32)

        def start_fill(i, carry):
            for live_bit, copy in _pad_fill_copies(live_ref, i, zeros, xb_hbm, fill_sem):
                pl.when(live_bit)(copy.start)
            return carry

        lax.fori_loop(0, n_blocks, start_fill, 0)

    def wait_tile_writes(s):
        pltpu.make_async_copy(stage.at[s], xb_hbm.at[whole_tile, :], sem.at[s]).wait()

    @pl.when(step >= 2)
    def _():
        wait_tile_writes(slot)

    h = _tiles_to_rows(h_ref).astype(BF16)
    loc = loc_ref[...]
    pos = lax.broadcasted_iota(jnp.int32, (tt, TOP_K * tt), 1)
    one_hot = pos == loc[:, 0:1]
    for k in range(1, TOP_K):
        one_hot = one_hot | (pos == loc[:, k:k + 1])
    packed = lax.dot_general(one_hot.astype(BF16), h, (((0,), (0,)), ((), ())),
                             preferred_element_type=F32)
    _rows_to_tiles(stage.at[slot], packed)

    def per_expert(e, carry):
        copies = _run_piece_copies(src_ref, len_ref, dst_ref, step * N_EXPERTS + e, xb_hbm,
                                   stage.at[slot], sem.at[slot], to_stage=False)
        for wanted, copy in copies:
            pl.when(wanted)(copy.start)
        return carry

    lax.fori_loop(0, N_EXPERTS, per_expert, 0)

    @pl.when(step == n_steps - 1)
    def _():
        wait_tile_writes(slot)

        @pl.when(n_steps >= 2)
        def _():
            wait_tile_writes(1 - slot)

        def wait_fill(i, carry):
            for live_bit, copy in _pad_fill_copies(live_ref, i, zeros, xb_hbm, fill_sem):
                pl.when(live_bit)(copy.wait)
            return carry

        lax.fori_loop(0, n_blocks, wait_fill, 0)


def _run_tables(pstarts, tile_runs):
    run_src = (pstarts[None, :] + tile_runs[:, 0, :N_EXPERTS]).reshape(-1)
    run_len = tile_runs[:, 1, :N_EXPERTS].reshape(-1)
    run_dst = tile_runs[:, 2, :N_EXPERTS].reshape(-1)
    return run_src, run_len, run_dst


def _dispatch(run_tables, live, h3, loc, n_rows):
    t = h3.shape[0] // ROW_TILE
    tt = ROUTE_TT
    tile_map = lambda i, *_: (i, 0)
    return pl.pallas_call(
        _dispatch_kernel,
        grid_spec=pltpu.PrefetchScalarGridSpec(
            num_scalar_prefetch=4,
            grid=(t // tt,),
            in_specs=[pl.BlockSpec((tt * ROW_TILE, LANES), tile_map),
                      pl.BlockSpec((tt, LANES), tile_map)],
            out_specs=pl.BlockSpec(memory_space=pl.ANY),
            scratch_shapes=[pltpu.VMEM((2, TOP_K * tt * ROW_TILE, LANES), F32),
                            pltpu.VMEM((EXPERT_TM * ROW_TILE, LANES), F32),
                            pltpu.SemaphoreType.DMA((2,)), pltpu.SemaphoreType.DMA],
        ),
        out_shape=jax.ShapeDtypeStruct((n_rows * ROW_TILE, LANES), F32),
        compiler_params=_cparams(("arbitrary",)),
        name="moe_dispatch",
    )(*run_tables, live, h3, loc)


def _expert_kernel(be_ref, live_ref, first_ref, slot_ref, next_ref, x_ref, w1_hbm, b1_ref, w2_hbm, b2_ref,
                   y_ref, w1f, w2f, w1s, w2s, sem, *, layer):
    i = pl.program_id(0)
    live = live_ref[i]

    def weight_copies(e, slot):
        return (pltpu.make_async_copy(w1_hbm.at[layer, e], w1f.at[slot], sem.at[0, slot]),
                pltpu.make_async_copy(w2_hbm.at[layer, e], w2f.at[slot], sem.at[1, slot]))

    @pl.when(i == 0)
    def _():
        for copy in weight_copies(be_ref[0], 0):
            copy.start()

    @pl.when(first_ref[i] == 1)
    def _():
        slot = slot_ref[i]
        for copy in weight_copies(be_ref[i], slot):
            copy.wait()
        nxt = next_ref[i]

        @pl.when(nxt >= 0)
        def _():
            for copy in weight_copies(nxt, 1 - slot):
                copy.start()

        w1s[...] = w1f[slot].astype(BF16)
        w2s[...] = w2f[slot].astype(BF16)

    @pl.when(live > 0)
    def _():
        dff = w2s.shape[0]
        x = _tiles_to_rows(x_ref)
        row = lax.broadcasted_iota(jnp.int32, (x.shape[0], 1), 0)
        x = jnp.where(row < live, x, 0.0).astype(BF16)
        hh = jnp.dot(x, w1s[...], preferred_element_type=F32) + b1_ref[0, 0]
        x_glu = jnp.minimum(hh[:, :dff], SWIGLU_LIMIT)
        x_lin = jnp.clip(hh[:, dff:], -SWIGLU_LIMIT, SWIGLU_LIMIT)
        act = x_glu * _sigmoid(SWIGLU_ALPHA * x_glu) * (x_lin + 1.0)
        y = jnp.dot(act.astype(BF16), w2s[...], preferred_element_type=F32) + b2_ref[0, 0]
        _rows_to_tiles(y_ref, y)

    @pl.when(live == 0)
    def _():
        y_ref[...] = jnp.zeros(y_ref.shape, F32)


def _experts(block_e, live, xb, layer, w1, b1, w2, b2):
    depth, n_e, d, two_f = w1.shape
    dff = two_f // 2
    block_rows = EXPERT_TM * ROW_TILE
    nb = xb.shape[0] // block_rows
    first, slot, next_e = _expert_run_tables(block_e)
    bias_map = lambda i, be, *_: (layer, be[i], 0, 0)
    return pl.pallas_call(
        functools.partial(_expert_kernel, layer=layer),
        grid_spec=pltpu.PrefetchScalarGridSpec(
            num_scalar_prefetch=5,
            grid=(nb,),
            in_specs=[
                pl.BlockSpec((block_rows, LANES), lambda i, *_: (i, 0)),
                pl.BlockSpec(memory_space=pl.ANY),
                pl.BlockSpec((1, 1, 1, two_f), bias_map),
                pl.BlockSpec(memory_space=pl.ANY),
                pl.BlockSpec((1, 1, 1, d), bias_map),
            ],
            out_specs=pl.BlockSpec((block_rows, LANES), lambda i, *_: (i, 0)),
            scratch_shapes=[pltpu.VMEM((2, d, two_f), F32), pltpu.VMEM((2, dff, d), F32),
                            pltpu.VMEM((d, two_f), BF16), pltpu.VMEM((dff, d), BF16),
                            pltpu.SemaphoreType.DMA((2, 2))],
        ),
        out_shape=jax.ShapeDtypeStruct(xb.shape, F32),
        compiler_params=_cparams(("arbitrary",)),
        name="moe_experts",
    )(block_e, live, first, slot, next_e, xb, w1, b1.reshape(depth, n_e, 1, two_f), w2,
      b2.reshape(depth, n_e, 1, d))


def _combine_kernel(src_ref, len_ref, dst_ref, yb_hbm, loc_ref, gate_ref, x_ref, mod_ref, lng_ref, lnb_ref,
                    *rest, with_pw1):
    if with_pw1:
        modn_ref, wa_ref, wb_ref, ba_ref, bb_ref, o_ref, u_ref, stage, sem = rest
    else:
        o_ref, stage, sem = rest
    tt = x_ref.shape[0]
    step = pl.program_id(0)
    n_steps = pl.num_programs(0)

    def start_gather(tile, slot):
        def per_expert(e, carry):
            copies = _run_piece_copies(src_ref, len_ref, dst_ref, tile * N_EXPERTS + e, yb_hbm,
                                       stage.at[slot], sem.at[slot], to_stage=True)
            for wanted, copy in copies:
                pl.when(wanted)(copy.start)
            return carry

        lax.fori_loop(0, N_EXPERTS, per_expert, 0)

    slot = step % 2

    @pl.when(step == 0)
    def _():
        start_gather(step, slot)

    @pl.when(step + 1 < n_steps)
    def _():
        start_gather(step + 1, 1 - slot)

    pltpu.make_async_copy(yb_hbm.at[pl.ds(0, TOP_K * tt * ROW_TILE), :], stage.at[slot], sem.at[slot]).wait()

    staged = _tiles_to_rows(stage.at[slot]).astype(BF16)
    loc = loc_ref[...]
    gates = gate_ref[...]
    pos = lax.broadcasted_iota(jnp.int32, (tt, TOP_K * tt), 1)
    pick = jnp.zeros((tt, TOP_K * tt), F32)
    for k in range(TOP_K):
        pick = jnp.where(pos == loc[:, k:k + 1], gates[:, k:k + 1], pick)
    y = jnp.dot(pick.astype(BF16), staged, preferred_element_type=F32)
    x_new = _residual_ln(x_ref[...], y, mod_ref[0, 5:6, :], lng_ref[...], lnb_ref[...])
    o_ref[...] = x_new
    if with_pw1:
        _pw1_glu_store(x_new, modn_ref, wa_ref, wb_ref, ba_ref, bb_ref, u_ref)


def _combine(run_tables, yb, loc, gates, x1, mod, ln_g, ln_b, seq, next_pw1=None):
    t, d = x1.shape
    tt = ROUTE_TT
    per_b = seq // tt
    const = lambda i, *_: (0, 0)
    tile_map = lambda i, *_: (i, 0)
    in_specs = [
        pl.BlockSpec(memory_space=pl.ANY),
        pl.BlockSpec((tt, LANES), tile_map),
        pl.BlockSpec((tt, LANES), tile_map),
        pl.BlockSpec((tt, d), tile_map),
        pl.BlockSpec((1, 6, d), lambda i, *_: (i // per_b, 0, 0)),
        pl.BlockSpec((1, d), const),
        pl.BlockSpec((1, d), const),
    ]
    operands = [*run_tables, yb, loc, gates, x1, mod, ln_g.reshape(1, d), ln_b.reshape(1, d)]
    out_specs = [pl.BlockSpec((tt, d), tile_map)]
    out_shape = [jax.ShapeDtypeStruct((t, d), F32)]
    if next_pw1 is not None:
        mod_next, pw1_w, pw1_b = next_pw1
        inner = pw1_w.shape[1] // 2
        assert inner == ROW_TILE * LANES
        w = pw1_w.astype(BF16)
        b = pw1_b.reshape(1, 2 * inner)
        in_specs += [
            pl.BlockSpec((1, 6, d), lambda i, *_: (i // per_b, 0, 0)),
            pl.BlockSpec((d, inner), const),
            pl.BlockSpec((d, inner), lambda i, *_: (0, 1)),
            pl.BlockSpec((1, inner), const),
            pl.BlockSpec((1, inner), lambda i, *_: (0, 1)),
        ]
        operands += [mod_next, w, w, b, b]
        out_specs.append(pl.BlockSpec((tt * ROW_TILE, LANES), tile_map))
        out_shape.append(jax.ShapeDtypeStruct((t * ROW_TILE, LANES), F32))
    return pl.pallas_call(
        functools.partial(_combine_kernel, with_pw1=next_pw1 is not None),
        grid_spec=pltpu.PrefetchScalarGridSpec(
            num_scalar_prefetch=3,
            grid=(t // tt,),
            in_specs=in_specs,
            out_specs=out_specs,
            scratch_shapes=[pltpu.VMEM((2, TOP_K * tt * ROW_TILE, LANES), F32), pltpu.SemaphoreType.DMA((2,))],
        ),
        out_shape=out_shape,
        compiler_params=_cparams(("arbitrary",)),
        name="moe_combine_ln",
    )(*operands)


def _moe_block(x1, h3, idx, gates, mod, layer, w1, b1, w2, b2, ln_g, ln_b, seq, next_pw1=None):
    t = x1.shape[0]
    loc, tile_runs, counts = _rank_and_count(idx)
    pstarts, block_e, live = _routing_tables(t, counts)
    run_tables = _run_tables(pstarts, tile_runs)
    n_rows = _num_expert_blocks(t * TOP_K) * EXPERT_TM
    xb = _dispatch(run_tables, live, h3, loc, n_rows)
    yb = _experts(block_e, live, xb, layer, w1, b1, w2, b2)
    return _combine(run_tables, yb, loc, gates, x1, mod, ln_g, ln_b, seq, next_pw1)


def kernel(x, c, ada_w, ada_b, dn_in_w, dn_conv_w, dn_A_log, dn_dt_bias, dn_onorm_w, dn_out_w,
           cf_pw1_w, cf_pw1_b, cf_dw_w, cf_dw_b, cf_ln_g, cf_ln_b, cf_pw2_w, cf_pw2_b,
           ln1_g, ln1_b, router_w, router_b, e_w1, e_b1, e_w2, e_b2, ln2_g, ln2_b):
    bsz, seq, d = x.shape
    mods = _ada_ln(c, ada_w, ada_b)

    proj, ba = _dn_in_proj(x, mods[0], dn_in_w[0])
    qn, kn, vv, gates_dn = _dn_prep(proj, ba, dn_conv_w[0], dn_A_log[0], dn_dt_bias[0])
    o = _dn_chunk(qn, kn, vv, proj, gates_dn, dn_onorm_w[0])
    x1, h3, idx, gates = _dn_out(o, dn_out_w[0], x, mods[0], ln1_g[0], ln1_b[0], router_w[0], router_b[0])
    x2, u = _moe_block(x1, h3, idx, gates, mods[0], 0, e_w1, e_b1, e_w2, e_b2, ln2_g[0], ln2_b[0], seq,
                       next_pw1=(mods[1], cf_pw1_w[0], cf_pw1_b[0]))

    x2 = x2.reshape(bsz, seq, d)
    x3, h3, idx, gates = _cf_tail(u, cf_dw_w[0], cf_dw_b[0], cf_ln_g[0], cf_ln_b[0], cf_pw2_w[0], cf_pw2_b[0],
                                  x2, mods[1], ln1_g[1], ln1_b[1], router_w[1], router_b[1])
    (x4,) = _moe_block(x3, h3, idx, gates, mods[1], 1, e_w1, e_b1, e_w2, e_b2, ln2_g[1], ln2_b[1], seq)
    return x4.reshape(bsz, seq, d)
```

```python
import functools

import jax
import jax.numpy as jnp
from jax import lax
from jax.experimental import pallas as pl
from jax.experimental.pallas import tpu as pltpu

F32 = jnp.float32
BF16 = jnp.bfloat16
HIGHEST = lax.Precision.HIGHEST

DEPTH = 2
DN_QK_HEADS = 8
DN_V_HEADS = 16
DN_HEAD_DIM = 128
DN_CONV = 4
DN_CHUNK = 64
CF_KERNEL = 31
N_EXPERTS = 32
TOP_K = 4
SWIGLU_LIMIT = 7.0
SWIGLU_ALPHA = 1.702
LN_EPS = 1e-5
RMS_EPS = 1e-6
L2_EPS = 1e-6
DEEPNORM_ALPHA = (2 * DEPTH) ** 0.25

LANES = 128
SUBLANES = 8
BF16_SUBLANES = 16
VMEM_LIMIT = 56 * 1024 * 1024

ADA_TN = 1536
PROJ_TM = 1024
PROJ_TN = 2048
PREP_TS = 512
CHUNKS_PER_STEP = 8
DN_CHUNK_UNROLL = 4
POST_TM = 512
ROUTE_TT = 256
RANK_TILES_PER_STEP = 4
EXPERT_TM = 512
CF_TM = 512
CF_HALO = 32
CF_CONV_ROWS = 32

NEG_BIG = -1e30


def _cparams(sem):
    return pltpu.CompilerParams(dimension_semantics=sem, vmem_limit_bytes=VMEM_LIMIT)


def _sigmoid(x):
    return jax.nn.sigmoid(x)


def _layer_norm(v, g, b):
    mu = jnp.mean(v, -1, keepdims=True)
    d = v - mu
    var = jnp.mean(d * d, -1, keepdims=True)
    return d * lax.rsqrt(var + LN_EPS) * g + b


ROW_TILE = SUBLANES


def _rows_to_tiles(o_ref, val):
    rows = val.shape[0]
    for s in range(ROW_TILE):
        o_ref[pl.ds(s, rows, stride=ROW_TILE), :] = val[:, s * LANES:(s + 1) * LANES]


def _tiles_to_rows(x_ref):
    rows = x_ref.shape[0] // ROW_TILE
    return jnp.concatenate([x_ref[pl.ds(s, rows, stride=ROW_TILE), :] for s in range(ROW_TILE)], axis=1)


def _split_weight(w):
    k, n = w.shape
    hi = w.astype(BF16)
    lo = (w - hi.astype(F32)).astype(BF16)
    out = jnp.zeros((k, 2 * LANES), BF16)
    return out.at[:, :n].set(hi).at[:, LANES:LANES + n].set(lo)


def _narrow_dot(h, w_split_ref):
    m = h.shape[0]
    hi = h.astype(BF16)
    lo = (h - hi.astype(F32)).astype(BF16)
    out = jnp.dot(jnp.concatenate([hi, lo], axis=0), w_split_ref[...], preferred_element_type=F32)
    return out[:m, :LANES] + out[:m, LANES:] + out[m:, :LANES]


def _ada_kernel(c_ref, w_ref, b_ref, o_ref):
    c = c_ref[...]
    cond = c * _sigmoid(c)
    o_ref[0] = jnp.dot(cond, w_ref[0], precision=HIGHEST, preferred_element_type=F32) + b_ref[0]


def _ada_ln(c, ada_w, ada_b):
    depth, d, n = ada_w.shape
    bsz = c.shape[0]
    c_pad = jnp.zeros((SUBLANES, d), F32).at[:bsz].set(c)
    out = pl.pallas_call(
        _ada_kernel,
        grid=(depth, n // ADA_TN),
        in_specs=[
            pl.BlockSpec((SUBLANES, d), lambda i, j: (0, 0)),
            pl.BlockSpec((1, d, ADA_TN), lambda i, j: (i, 0, j)),
            pl.BlockSpec((1, 1, ADA_TN), lambda i, j: (i, 0, j)),
        ],
        out_specs=pl.BlockSpec((1, SUBLANES, ADA_TN), lambda i, j: (i, 0, j)),
        out_shape=jax.ShapeDtypeStruct((depth, SUBLANES, n), F32),
        compiler_params=_cparams(("parallel", "parallel")),
        name="ada_ln",
    )(c_pad, ada_w, ada_b.reshape(depth, 1, n))
    return out[:, :bsz].reshape(depth, bsz, 6, d)


def _inproj_kernel(x_ref, mod_ref, w_ref, wba_ref, proj_ref, ba_ref, h_scr):
    @pl.when(pl.program_id(2) == 0)
    def _():
        h = x_ref[0] * (1.0 + mod_ref[0, 1:2, :]) + mod_ref[0, 0:1, :]
        h_scr[...] = h.astype(BF16)
        ba_ref[0] = _narrow_dot(h, wba_ref)

    proj_ref[0] = jnp.dot(h_scr[...], w_ref[...], preferred_element_type=F32).astype(BF16)


def _dn_in_proj(x, mod, in_w):
    bsz, seq, d = x.shape
    n_main = in_w.shape[1] - 2 * DN_V_HEADS
    w_main = in_w[:, :n_main].astype(BF16)
    w_ba = _split_weight(in_w[:, n_main:])
    tm = min(PROJ_TM, seq)
    return pl.pallas_call(
        _inproj_kernel,
        grid=(bsz, seq // tm, n_main // PROJ_TN),
        in_specs=[
            pl.BlockSpec((1, tm, d), lambda b, i, j: (b, i, 0)),
            pl.BlockSpec((1, 6, d), lambda b, i, j: (b, 0, 0)),
            pl.BlockSpec((d, PROJ_TN), lambda b, i, j: (0, j)),
            pl.BlockSpec((d, 2 * LANES), lambda b, i, j: (0, 0)),
        ],
        out_specs=[
            pl.BlockSpec((1, tm, PROJ_TN), lambda b, i, j: (b, i, j)),
            pl.BlockSpec((1, tm, LANES), lambda b, i, j: (b, i, 0)),
        ],
        out_shape=[
            jax.ShapeDtypeStruct((bsz, seq, n_main), BF16),
            jax.ShapeDtypeStruct((bsz, seq, LANES), F32),
        ],
        scratch_shapes=[pltpu.VMEM((tm, d), BF16)],
        compiler_params=_cparams(("parallel", "parallel", "arbitrary")),
        name="dn_in_proj",
    )(x, mod, w_main, w_ba)


def _dn_prep_kernel(q_ref, k_ref, v_ref, qh_ref, kh_ref, vh_ref, cw_ref, ba_ref, alog_ref, dt_ref,
                    qo_ref, ko_ref, vo_ref, g_ref, scr):
    ts = q_ref.shape[1]
    halo = qh_ref.shape[1]
    keep = (pl.program_id(1) > 0).astype(F32)

    def conv_silu(x_ref, h_ref, c0):
        width = x_ref.shape[2]
        scr[0:halo, 0:width] = h_ref[0].astype(F32) * keep
        scr[halo:halo + ts, 0:width] = x_ref[0].astype(F32)
        acc = None
        for j in range(DN_CONV):
            off = halo - (DN_CONV - 1) + j
            term = scr[off:off + ts, 0:width] * cw_ref[j:j + 1, c0:c0 + width]
            acc = term if acc is None else acc + term
        return acc * _sigmoid(acc)

    def l2norm_store(o_ref, x, scale):
        for h in range(x.shape[1] // DN_HEAD_DIM):
            xh = x[:, h * DN_HEAD_DIM:(h + 1) * DN_HEAD_DIM]
            ss = jnp.sum(xh * xh, -1, keepdims=True)
            o_ref[0, :, h * DN_HEAD_DIM:(h + 1) * DN_HEAD_DIM] = (
                xh * lax.rsqrt(ss + L2_EPS) * scale).astype(o_ref.dtype)

    kd = q_ref.shape[2]
    l2norm_store(qo_ref, conv_silu(q_ref, qh_ref, 0), DN_HEAD_DIM ** -0.5)
    l2norm_store(ko_ref, conv_silu(k_ref, kh_ref, kd), 1.0)
    vo_ref[0] = conv_silu(v_ref, vh_ref, 2 * kd).astype(vo_ref.dtype)

    ba = ba_ref[0]
    beta = _sigmoid(ba)
    zz = ba + dt_ref[...]
    softplus = jnp.maximum(zz, 0.0) + jnp.log1p(jnp.exp(-jnp.abs(zz)))
    g = -jnp.exp(alog_ref[...]) * softplus
    r = lax.broadcasted_iota(jnp.int32, (ts, ts), 0)
    c = lax.broadcasted_iota(jnp.int32, (ts, ts), 1)
    in_chunk_tril = ((r // DN_CHUNK == c // DN_CHUNK) & (c <= r)).astype(BF16)
    g1 = g.astype(BF16)
    g2 = (g - g1.astype(F32)).astype(BF16)
    g3 = (g - g1.astype(F32) - g2.astype(F32)).astype(BF16)
    parts = jnp.dot(in_chunk_tril, jnp.concatenate([g1, g2, g3], axis=1), preferred_element_type=F32)
    gcum = parts[:, :LANES] + parts[:, LANES:2 * LANES] + parts[:, 2 * LANES:]
    lane = lax.broadcasted_iota(jnp.int32, ba.shape, 1)
    g_ref[0] = jnp.where(lane < DN_V_HEADS, beta, gcum)


def _dn_prep(proj, ba, conv_w, a_log, dt_bias):
    bsz, seq, _ = proj.shape
    kd = DN_QK_HEADS * DN_HEAD_DIM
    vd = DN_V_HEADS * DN_HEAD_DIM
    ts = min(PREP_TS, seq)
    halo = BF16_SUBLANES
    hb = ts // halo
    alog_row = jnp.zeros((1, LANES), F32).at[0, DN_V_HEADS:2 * DN_V_HEADS].set(a_log)
    dt_row = jnp.zeros((1, LANES), F32).at[0, DN_V_HEADS:2 * DN_V_HEADS].set(dt_bias)

    def halo_map(col):
        return lambda b, i: (b, jnp.maximum(i * hb - 1, 0), col)

    return pl.pallas_call(
        _dn_prep_kernel,
        grid=(bsz, seq // ts),
        in_specs=[
            pl.BlockSpec((1, ts, kd), lambda b, i: (b, i, 0)),
            pl.BlockSpec((1, ts, kd), lambda b, i: (b, i, 1)),
            pl.BlockSpec((1, ts, vd), lambda b, i: (b, i, 1)),
            pl.BlockSpec((1, halo, kd), halo_map(0)),
            pl.BlockSpec((1, halo, kd), halo_map(1)),
            pl.BlockSpec((1, halo, vd), halo_map(1)),
            pl.BlockSpec((DN_CONV, 2 * kd + vd), lambda b, i: (0, 0)),
            pl.BlockSpec((1, ts, LANES), lambda b, i: (b, i, 0)),
            pl.BlockSpec((1, LANES), lambda b, i: (0, 0)),
            pl.BlockSpec((1, LANES), lambda b, i: (0, 0)),
        ],
        out_specs=[
            pl.BlockSpec((1, ts, kd), lambda b, i: (b, i, 0)),
            pl.BlockSpec((1, ts, kd), lambda b, i: (b, i, 0)),
            pl.BlockSpec((1, ts, vd), lambda b, i: (b, i, 0)),
            pl.BlockSpec((1, ts, LANES), lambda b, i: (b, i, 0)),
        ],
        out_shape=[
            jax.ShapeDtypeStruct((bsz, seq, kd), BF16),
            jax.ShapeDtypeStruct((bsz, seq, kd), BF16),
            jax.ShapeDtypeStruct((bsz, seq, vd), BF16),
            jax.ShapeDtypeStruct((bsz, seq, LANES), F32),
        ],
        scratch_shapes=[pltpu.VMEM((ts + halo, vd), F32)],
        compiler_params=_cparams(("parallel", "arbitrary")),
        name="dn_prep",
    )(proj, proj, proj, proj, proj, proj, conv_w, ba, alog_row, dt_row)


def _bmm(a, b):
    return lax.dot_general(a.astype(BF16), b.astype(BF16), (((2,), (1,)), ((0,), (0,))),
                           preferred_element_type=F32)


def _bmm_nt(a, b):
    return lax.dot_general(a.astype(BF16), b.astype(BF16), (((2,), (2,)), ((0,), (0,))),
                           preferred_element_type=F32)


def _bmm_tn(a, b):
    return lax.dot_general(a.astype(BF16), b.astype(BF16), (((1,), (1,)), ((0,), (0,))),
                           preferred_element_type=F32)


def _unit_lower_inverse_wide(a_twice, upper, eye_upper):
    c = a_twice.shape[1]
    x = jnp.where(upper, eye_upper, -a_twice)
    span = 1
    while span < c:
        x = _bmm(x[:, :, :c], x) + jnp.where(upper, x, 0.0)
        span *= 2
    return x


def _dn_chunk_kernel(q_ref, k_ref, v_ref, z_ref, g_ref, gt_ref, ow_ref, o_ref, s_ref):
    c_len = DN_CHUNK
    dh = DN_HEAD_DIM
    rep = DN_V_HEADS // DN_QK_HEADS

    @pl.when(pl.program_id(1) == 0)
    def _():
        s_ref[...] = jnp.zeros(s_ref.shape, F32)

    heads = range(DN_V_HEADS)
    n_heads = DN_V_HEADS
    ri = lax.broadcasted_iota(jnp.int32, (1, c_len, 2 * c_len), 1)
    lane = lax.broadcasted_iota(jnp.int32, (1, c_len, 2 * c_len), 2)
    upper = lane >= c_len
    ci = jnp.where(upper, lane - c_len, lane)
    causal = ci <= ri
    strict = ci < ri
    eye_upper = ((ci == ri) & upper).astype(F32)
    onorm = ow_ref[...]

    def head_cols(ref, rows, h):
        return ref[0, rows, h * dh:(h + 1) * dh]

    def per_v_head(t):
        return jnp.stack([t[h // rep] for h in heads])

    def state_free_part(c):
        r0 = pl.multiple_of(c * c_len, c_len)
        rows = pl.ds(r0, c_len)
        gcols = g_ref[0, rows, :]
        grows = gt_ref[0, c]
        wide = (n_heads, c_len, dh)
        beta_b = jnp.broadcast_to(jnp.stack([gcols[:, h:h + 1] for h in heads]), wide)
        g_b = jnp.broadcast_to(
            jnp.stack([gcols[:, DN_V_HEADS + h:DN_V_HEADS + h + 1] for h in heads]), wide)
        g_r = jnp.stack([grows[DN_V_HEADS + h:DN_V_HEADS + h + 1, :] for h in heads])
        g_r = jnp.concatenate([g_r, g_r], axis=2)
        g_last = g_b[:, c_len - 1:c_len, :]

        qn = jnp.stack([head_cols(q_ref, rows, hq) for hq in range(DN_QK_HEADS)])
        kn = jnp.stack([head_cols(k_ref, rows, hq) for hq in range(DN_QK_HEADS)])
        qk_kk = _bmm_nt(jnp.concatenate([qn, kn], axis=1), jnp.concatenate([kn, kn], axis=1))
        qk = per_v_head(qk_kk[:, :c_len])
        kk = per_v_head(qk_kk[:, c_len:])
        qf = per_v_head(qn).astype(F32)
        kf = per_v_head(kn).astype(F32)
        vf = jnp.stack([head_cols(v_ref, rows, h) for h in heads]).astype(F32)

        decay = jnp.where(causal, jnp.exp(g_b - g_r), 0.0)
        a_twice = jnp.where(strict, beta_b * kk * decay, 0.0)
        x_inv = _unit_lower_inverse_wide(a_twice, upper, eye_upper)
        eg = jnp.exp(g_b)
        rhs = jnp.concatenate([vf * beta_b, kf * (beta_b * eg)], axis=2)
        sol = _bmm(x_inv, jnp.concatenate([jnp.zeros_like(rhs), rhs], axis=1))
        u = sol[:, :, :dh]
        w = sol[:, :, dh:]
        qkm = jnp.where(causal, qk * decay, 0.0)[:, :, :c_len]
        qg = qf * eg
        kdec = kf * jnp.exp(g_last - g_b)
        return rows, jnp.concatenate([w, qg], axis=1), u, qkm, kdec, jnp.exp(g_last)

    def state_part(rows, w_qg, u, qkm, kdec, chunk_decay):
        state = s_ref[...]
        ws = _bmm(w_qg, state)
        v_new = u - ws[:, :c_len]
        o = ws[:, c_len:] + _bmm(qkm, v_new)
        s_ref[...] = state * chunk_decay + _bmm_tn(kdec, v_new)

        o = o * lax.rsqrt(jnp.mean(o * o, -1, keepdims=True) + RMS_EPS) * onorm
        for h in heads:
            zf = head_cols(z_ref, rows, h).astype(F32)
            o_ref[0, rows, h * dh:(h + 1) * dh] = (o[h] * (zf * _sigmoid(zf))).astype(o_ref.dtype)

    def chunk_group(i, carry):
        prepared = [state_free_part(i * DN_CHUNK_UNROLL + j) for j in range(DN_CHUNK_UNROLL)]
        for args in prepared:
            state_part(*args)
        return carry

    lax.fori_loop(0, q_ref.shape[1] // (c_len * DN_CHUNK_UNROLL), chunk_group, 0)


def _dn_chunk(qn, kn, vv, proj, gates, onorm_w):
    bsz, seq, kd = qn.shape
    vd = vv.shape[2]
    n_chunks = seq // DN_CHUNK
    cb = min(CHUNKS_PER_STEP, n_chunks)
    rows = cb * DN_CHUNK
    gates_t = jnp.swapaxes(gates[:, :, :2 * DN_V_HEADS].reshape(bsz, n_chunks, DN_CHUNK, 2 * DN_V_HEADS), 2, 3)
    z_col = (2 * kd + vd) // vd
    return pl.pallas_call(
        _dn_chunk_kernel,
        grid=(bsz, n_chunks // cb),
        in_specs=[
            pl.BlockSpec((1, rows, kd), lambda b, n: (b, n, 0)),
            pl.BlockSpec((1, rows, kd), lambda b, n: (b, n, 0)),
            pl.BlockSpec((1, rows, vd), lambda b, n: (b, n, 0)),
            pl.BlockSpec((1, rows, vd), lambda b, n: (b, n, z_col)),
            pl.BlockSpec((1, rows, LANES), lambda b, n: (b, n, 0)),
            pl.BlockSpec((1, cb, 2 * DN_V_HEADS, DN_CHUNK), lambda b, n: (b, n, 0, 0)),
            pl.BlockSpec((1, DN_HEAD_DIM), lambda b, n: (0, 0)),
        ],
        out_specs=pl.BlockSpec((1, rows, vd), lambda b, n: (b, n, 0)),
        out_shape=jax.ShapeDtypeStruct((bsz, seq, vd), BF16),
        scratch_shapes=[pltpu.VMEM((DN_V_HEADS, DN_HEAD_DIM, DN_HEAD_DIM), F32)],
        compiler_params=_cparams(("parallel", "arbitrary")),
        name="dn_chunk",
    )(qn, kn, vv, proj, gates, gates_t, onorm_w.reshape(1, DN_HEAD_DIM))


def _residual_ln(x, y, gate_row, ln_g, ln_b):
    return _layer_norm(DEEPNORM_ALPHA * x + (1.0 + gate_row) * y, ln_g, ln_b)


def _route_store(h, rw_ref, rb_ref, idx_ref, gate_ref):
    logits = _narrow_dot(h, rw_ref) + rb_ref[...]
    lane = lax.broadcasted_iota(jnp.int32, logits.shape, 1).astype(F32)
    work = logits
    idx_out = jnp.zeros(logits.shape, F32)
    val_out = jnp.full(logits.shape, NEG_BIG, F32)
    for k in range(TOP_K):
        m = jnp.max(work, -1, keepdims=True)
        am = jnp.min(jnp.where(work == m, lane, float(LANES)), -1, keepdims=True)
        idx_out = jnp.where(lane == k, am, idx_out)
        val_out = jnp.where(lane == k, m, val_out)
        work = jnp.where(lane == am, NEG_BIG * 2.0, work)
    top = jnp.max(val_out, -1, keepdims=True)
    e = jnp.where(lane < TOP_K, jnp.exp(val_out - top), 0.0)
    idx_ref[...] = idx_out.astype(jnp.int32)
    gate_ref[...] = e / jnp.sum(e, -1, keepdims=True)


def _post_mixer_tail(x, y, mod_ref, lng_ref, lnb_ref, rw_ref, rb_ref, x1_ref, h3_ref, idx_ref, gate_ref):
    x1 = _residual_ln(x, y, mod_ref[0, 2:3, :], lng_ref[...], lnb_ref[...])
    x1_ref[...] = x1
    h2 = x1 * (1.0 + mod_ref[0, 4:5, :]) + mod_ref[0, 3:4, :]
    _rows_to_tiles(h3_ref, h2)
    _route_store(h2, rw_ref, rb_ref, idx_ref, gate_ref)


def _router_operands(router_w, router_b):
    rb = jnp.full((1, LANES), NEG_BIG, F32).at[0, :N_EXPERTS].set(router_b)
    return _split_weight(router_w), rb


def _post_out_specs(tm, d):
    return [
        pl.BlockSpec((tm, d), lambda i: (i, 0)),
        pl.BlockSpec((tm * ROW_TILE, LANES), lambda i: (i, 0)),
        pl.BlockSpec((tm, LANES), lambda i: (i, 0)),
        pl.BlockSpec((tm, LANES), lambda i: (i, 0)),
    ]


def _post_out_shapes(t, d):
    assert d == ROW_TILE * LANES
    return [
        jax.ShapeDtypeStruct((t, d), F32),
        jax.ShapeDtypeStruct((t * ROW_TILE, LANES), F32),
        jax.ShapeDtypeStruct((t, LANES), jnp.int32),
        jax.ShapeDtypeStruct((t, LANES), F32),
    ]


def _dn_out_kernel(o_ref, w_ref, x_ref, mod_ref, lng_ref, lnb_ref, rw_ref, rb_ref,
                   x1_ref, h3_ref, idx_ref, gate_ref):
    y = jnp.dot(o_ref[...], w_ref[...], preferred_element_type=F32)
    _post_mixer_tail(x_ref[...], y, mod_ref, lng_ref, lnb_ref, rw_ref, rb_ref,
                     x1_ref, h3_ref, idx_ref, gate_ref)


def _dn_out(o, out_w, x, mod, ln_g, ln_b, router_w, router_b):
    bsz, seq, d = x.shape
    t = bsz * seq
    vd = o.shape[2]
    tm = min(POST_TM, seq)
    rw, rb = _router_operands(router_w, router_b)
    per_b = seq // tm
    return pl.pallas_call(
        _dn_out_kernel,
        grid=(t // tm,),
        in_specs=[
            pl.BlockSpec((tm, vd), lambda i: (i, 0)),
            pl.BlockSpec((vd, d), lambda i: (0, 0)),
            pl.BlockSpec((tm, d), lambda i: (i, 0)),
            pl.BlockSpec((1, 6, d), lambda i: (i // per_b, 0, 0)),
            pl.BlockSpec((1, d), lambda i: (0, 0)),
            pl.BlockSpec((1, d), lambda i: (0, 0)),
            pl.BlockSpec((d, 2 * LANES), lambda i: (0, 0)),
            pl.BlockSpec((1, LANES), lambda i: (0, 0)),
        ],
        out_specs=_post_out_specs(tm, d),
        out_shape=_post_out_shapes(t, d),
        compiler_params=_cparams(("parallel",)),
        name="dn_out_ln_route",
    )(o.reshape(t, vd), out_w.astype(BF16), x.reshape(t, d), mod, ln_g.reshape(1, d), ln_b.reshape(1, d), rw, rb)


def _pw1_glu_store(x, mod_ref, wa_ref, wb_ref, ba_ref, bb_ref, u_ref):
    h = (x * (1.0 + mod_ref[0, 1:2, :]) + mod_ref[0, 0:1, :]).astype(BF16)
    pa = jnp.dot(h, wa_ref[...], preferred_element_type=F32) + ba_ref[...]
    pb = jnp.dot(h, wb_ref[...], preferred_element_type=F32) + bb_ref[...]
    _rows_to_tiles(u_ref, pa * _sigmoid(pb))


def _cf_tail_kernel(u_ref, uh_ref, dw_ref, dwb_ref, cg_ref, cb_ref, w2_ref, b2_ref,
                    x_ref, mod_ref, lng_ref, lnb_ref, rw_ref, rb_ref,
                    x1_ref, h3_ref, idx_ref, gate_ref, scr, conv_scr):
    tm = u_ref.shape[0] // ROW_TILE
    halo = uh_ref.shape[0] // ROW_TILE
    keep = (pl.program_id(1) > 0).astype(F32)
    scr[0:halo * ROW_TILE, :] = uh_ref[...] * keep
    scr[halo * ROW_TILE:(halo + tm) * ROW_TILE, :] = u_ref[...]
    first = halo - (CF_KERNEL - 1)
    blk = CF_CONV_ROWS * ROW_TILE

    def conv_tokens(tb, carry):
        r0 = pl.multiple_of(tb * blk, blk)
        acc = jnp.concatenate([dwb_ref[...]] * CF_CONV_ROWS, axis=0)
        for j in range(CF_KERNEL):
            w_tile = dw_ref[j * ROW_TILE:(j + 1) * ROW_TILE, :]
            w_blk = jnp.concatenate([w_tile] * CF_CONV_ROWS, axis=0)
            acc = acc + scr[pl.ds(r0 + (first + j) * ROW_TILE, blk), :] * w_blk
        conv_scr[pl.ds(r0, blk), :] = acc
        return carry

    lax.fori_loop(0, tm // CF_CONV_ROWS, conv_tokens, 0)
    conv = _tiles_to_rows(conv_scr)
    normed = _layer_norm(conv, cg_ref[...], cb_ref[...])
    act = normed * _sigmoid(normed)
    y = jnp.dot(act.astype(BF16), w2_ref[...], preferred_element_type=F32) + b2_ref[...]
    _post_mixer_tail(x_ref[0], y, mod_ref, lng_ref, lnb_ref, rw_ref, rb_ref,
                     x1_ref, h3_ref, idx_ref, gate_ref)


def _cf_tail(u, dw_w, dw_b, cf_ln_g, cf_ln_b, pw2_w, pw2_b, x, mod, ln_g, ln_b, router_w, router_b):
    bsz, seq, d = x.shape
    inner = dw_w.shape[1]
    t = bsz * seq
    tm = min(CF_TM, seq)
    per_b = seq // tm
    hb = tm // CF_HALO
    halos_per_b = seq // CF_HALO
    rw, rb = _router_operands(router_w, router_b)
    row = lambda v: v.reshape(1, -1)
    tiles = lambda v: v.reshape(-1, LANES)
    const = lambda b, i: (0, 0)
    flat = lambda b, i: (b * per_b + i, 0)
    return pl.pallas_call(
        _cf_tail_kernel,
        grid=(bsz, per_b),
        in_specs=[
            pl.BlockSpec((tm * ROW_TILE, LANES), flat),
            pl.BlockSpec((CF_HALO * ROW_TILE, LANES),
                         lambda b, i: (b * halos_per_b + jnp.maximum(i * hb - 1, 0), 0)),
            pl.BlockSpec((CF_KERNEL * ROW_TILE, LANES), const),
            pl.BlockSpec((ROW_TILE, LANES), const),
            pl.BlockSpec((1, inner), const),
            pl.BlockSpec((1, inner), const),
            pl.BlockSpec((inner, d), const),
            pl.BlockSpec((1, d), const),
            pl.BlockSpec((1, tm, d), lambda b, i: (b, i, 0)),
            pl.BlockSpec((1, 6, d), lambda b, i: (b, 0, 0)),
            pl.BlockSpec((1, d), const),
            pl.BlockSpec((1, d), const),
            pl.BlockSpec((d, 2 * LANES), const),
            pl.BlockSpec((1, LANES), const),
        ],
        out_specs=[
            pl.BlockSpec((tm, d), flat),
            pl.BlockSpec((tm * ROW_TILE, LANES), flat),
            pl.BlockSpec((tm, LANES), flat),
            pl.BlockSpec((tm, LANES), flat),
        ],
        out_shape=_post_out_shapes(t, d),
        scratch_shapes=[pltpu.VMEM(((tm + CF_HALO) * ROW_TILE, LANES), F32),
                        pltpu.VMEM((tm * ROW_TILE, LANES), F32)],
        compiler_params=_cparams(("parallel", "arbitrary")),
        name="cf_conv_ln_pw2_route",
    )(u, u, tiles(dw_w), tiles(dw_b), row(cf_ln_g), row(cf_ln_b), pw2_w.astype(BF16), row(pw2_b),
      x, mod, row(ln_g), row(ln_b), rw, rb)


def _rank_kernel(idx_ref, loc_ref, tile_ref, cnt_ref, carry):
    tt = ROUTE_TT

    @pl.when(pl.program_id(0) == 0)
    def _():
        carry[...] = jnp.zeros(carry.shape, F32)

    lane = lax.broadcasted_iota(jnp.int32, (tt, LANES), 1)
    r = lax.broadcasted_iota(jnp.int32, (tt, tt), 0)
    c = lax.broadcasted_iota(jnp.int32, (tt, tt), 1)
    earlier = (c < r).astype(BF16)
    e_from = lax.broadcasted_iota(jnp.int32, (LANES, LANES), 0)
    e_to = lax.broadcasted_iota(jnp.int32, (LANES, LANES), 1)
    lower_experts = (e_from < e_to).astype(BF16)
    row = lax.broadcasted_iota(jnp.int32, (SUBLANES, LANES), 0)

    before_tile = carry[...]
    for sub in range(idx_ref.shape[0] // tt):
        rows = slice(sub * tt, (sub + 1) * tt)
        idx = idx_ref[rows, :]
        sel = [lane == idx[:, k:k + 1] for k in range(TOP_K)]
        multi_hot = sel[0]
        for k in range(1, TOP_K):
            multi_hot = multi_hot | sel[k]
        mh = multi_hot.astype(BF16)
        before_in_tile = jnp.dot(earlier, mh, preferred_element_type=F32)
        run_len = jnp.sum(mh.astype(F32), 0, keepdims=True)
        run_start = jnp.dot(jnp.broadcast_to(run_len, (SUBLANES, LANES)).astype(BF16), lower_experts,
                            preferred_element_type=F32)[:1]
        loc = jnp.zeros(idx.shape, F32)
        for k in range(TOP_K):
            lk = jnp.sum(jnp.where(sel[k], before_in_tile + run_start, 0.0), -1, keepdims=True)
            loc = jnp.where(lane == k, lk, loc)
        loc_ref[rows, :] = loc.astype(jnp.int32)
        tile_ref[sub] = jnp.where(row == 0, before_tile,
                                  jnp.where(row == 1, run_len, run_start)).astype(jnp.int32)
        before_tile = before_tile + run_len
    carry[...] = before_tile
    cnt_ref[...] = before_tile.astype(jnp.int32)


def _rank_and_count(idx):
    t = idx.shape[0]
    tt = ROUTE_TT
    assert tt <= 256
    per_step = RANK_TILES_PER_STEP
    rows = per_step * tt
    assert t % rows == 0
    return pl.pallas_call(
        _rank_kernel,
        grid=(t // rows,),
        in_specs=[pl.BlockSpec((rows, LANES), lambda i: (i, 0))],
        out_specs=[
            pl.BlockSpec((rows, LANES), lambda i: (i, 0)),
            pl.BlockSpec((per_step, SUBLANES, LANES), lambda i: (i, 0, 0)),
            pl.BlockSpec((1, LANES), lambda i: (0, 0)),
        ],
        out_shape=[
            jax.ShapeDtypeStruct((t, LANES), jnp.int32),
            jax.ShapeDtypeStruct((t // tt, SUBLANES, LANES), jnp.int32),
            jax.ShapeDtypeStruct((1, LANES), jnp.int32),
        ],
        scratch_shapes=[pltpu.VMEM((1, LANES), F32)],
        compiler_params=_cparams(("arbitrary",)),
        name="moe_rank_count",
    )(idx)


def _num_expert_blocks(n_assign):
    bound = n_assign + N_EXPERTS * (EXPERT_TM - 1)
    return -(-bound // EXPERT_TM)


def _routing_tables(n_tokens, counts):
    nb = _num_expert_blocks(n_tokens * TOP_K)
    counts = counts[0, :N_EXPERTS]
    padded = (counts + EXPERT_TM - 1) // EXPERT_TM * EXPERT_TM
    pends = jnp.cumsum(padded)
    pstarts = (pends - padded).astype(jnp.int32)
    block_row0 = jnp.arange(nb, dtype=jnp.int32) * EXPERT_TM
    segments_done = jnp.sum((pends[None, :] <= block_row0[:, None]).astype(jnp.int32), axis=1)
    block_e = jnp.minimum(segments_done, N_EXPERTS - 1)
    r0 = block_row0[:, None]
    in_segment = (pstarts[None, :] <= r0) & (r0 < pends[None, :])
    live_if = jnp.clip(pstarts[None, :] + counts[None, :] - r0, 0, EXPERT_TM)
    live = jnp.sum(jnp.where(in_segment, live_if, 0), axis=1).astype(jnp.int32)
    return pstarts, block_e, live


def _expert_run_tables(block_e):
    nb = block_e.shape[0]
    pos = jnp.arange(nb, dtype=jnp.int32)
    first = jnp.concatenate([jnp.ones((1,), jnp.int32), (block_e[1:] != block_e[:-1]).astype(jnp.int32)])
    slot = (jnp.cumsum(first) - 1) % 2
    start_at_or_after = lax.cummin(jnp.where(first == 1, pos, nb), reverse=True)
    next_start = jnp.concatenate([start_at_or_after[1:], jnp.full((1,), nb, jnp.int32)])
    picks = next_start[:, None] == pos[None, :]
    next_e = jnp.where(next_start < nb, jnp.sum(jnp.where(picks, block_e[None, :], 0), axis=1), -1)
    return first, slot.astype(jnp.int32), next_e.astype(jnp.int32)


def _pad_fill_copies(live_ref, i, zeros, xb_hbm, sem):
    lv = live_ref[i]
    pad = EXPERT_TM - lv
    row0 = i * EXPERT_TM + lv
    copies = []
    piece = EXPERT_TM
    while piece >= 1:
        offset = pad & ~(2 * piece - 1)
        start = pl.multiple_of((row0 + offset) * ROW_TILE, ROW_TILE)
        copy = pltpu.make_async_copy(zeros.at[pl.ds(0, piece * ROW_TILE), :],
                                     xb_hbm.at[pl.ds(start, piece * ROW_TILE), :], sem)
        copies.append(((pad & piece) != 0, copy))
        piece //= 2
    return copies


def _run_piece_copies(src_ref, len_ref, dst_ref, run, sorted_hbm, stage, sem, to_stage):
    length = len_ref[run]
    src0 = src_ref[run]
    dst0 = dst_ref[run]
    copies = []
    piece = ROUTE_TT
    while piece >= 1:
        offset = length & ~(2 * piece - 1)
        hbm_rows = pl.ds(pl.multiple_of((src0 + offset) * ROW_TILE, ROW_TILE), piece * ROW_TILE)
        stage_rows = pl.ds(pl.multiple_of((dst0 + offset) * ROW_TILE, ROW_TILE), piece * ROW_TILE)
        ends = (sorted_hbm.at[hbm_rows, :], stage.at[stage_rows, :])
        copy = pltpu.make_async_copy(*(ends if to_stage else ends[::-1]), sem)
        copies.append(((length & piece) != 0, copy))
        piece //= 2
    return copies


def _dispatch_kernel(src_ref, len_ref, dst_ref, live_ref, h_ref, loc_ref, xb_hbm, stage, zeros, sem, fill_sem):
    tt = loc_ref.shape[0]
    step = pl.program_id(0)
    n_steps = pl.num_programs(0)
    slot = step % 2
    n_blocks = xb_hbm.shape[0] // (EXPERT_TM * ROW_TILE)
    whole_tile = pl.ds(0, TOP_K * tt * ROW_TILE)

    @pl.when(step == 0)
    def _():
        zeros[...] = jnp.zeros(zeros.shape, F32)

        def start_fill(i, carry):
            for live_bit, copy in _pad_fill_copies(live_ref, i, zeros, xb_hbm, fill_sem):
                pl.when(live_bit)(copy.start)
            return carry

        lax.fori_loop(0, n_blocks, start_fill, 0)

    def wait_tile_writes(s):
        pltpu.make_async_copy(stage.at[s], xb_hbm.at[whole_tile, :], sem.at[s]).wait()

    @pl.when(step >= 2)
    def _():
        wait_tile_writes(slot)

    h = _tiles_to_rows(h_ref).astype(BF16)
    loc = loc_ref[...]
    pos = lax.broadcasted_iota(jnp.int32, (tt, TOP_K * tt), 1)
    one_hot = pos == loc[:, 0:1]
    for k in range(1, TOP_K):
        one_hot = one_hot | (pos == loc[:, k:k + 1])
    packed = lax.dot_general(one_hot.astype(BF16), h, (((0,), (0,)), ((), ())),
                             preferred_element_type=F32)
    _rows_to_tiles(stage.at[slot], packed)

    def per_expert(e, carry):
        copies = _run_piece_copies(src_ref, len_ref, dst_ref, step * N_EXPERTS + e, xb_hbm,
                                   stage.at[slot], sem.at[slot], to_stage=False)
        for wanted, copy in copies:
            pl.when(wanted)(copy.start)
        return carry

    lax.fori_loop(0, N_EXPERTS, per_expert, 0)

    @pl.when(step == n_steps - 1)
    def _():
        wait_tile_writes(slot)

        @pl.when(n_steps >= 2)
        def _():
            wait_tile_writes(1 - slot)

        def wait_fill(i, carry):
            for live_bit, copy in _pad_fill_copies(live_ref, i, zeros, xb_hbm, fill_sem):
                pl.when(live_bit)(copy.wait)
            return carry

        lax.fori_loop(0, n_blocks, wait_fill, 0)


def _run_tables(pstarts, tile_runs):
    run_src = (pstarts[None, :] + tile_runs[:, 0, :N_EXPERTS]).reshape(-1)
    run_len = tile_runs[:, 1, :N_EXPERTS].reshape(-1)
    run_dst = tile_runs[:, 2, :N_EXPERTS].reshape(-1)
    return run_src, run_len, run_dst


def _dispatch(run_tables, live, h3, loc, n_rows):
    t = h3.shape[0] // ROW_TILE
    tt = ROUTE_TT
    tile_map = lambda i, *_: (i, 0)
    return pl.pallas_call(
        _dispatch_kernel,
        grid_spec=pltpu.PrefetchScalarGridSpec(
            num_scalar_prefetch=4,
            grid=(t // tt,),
            in_specs=[pl.BlockSpec((tt * ROW_TILE, LANES), tile_map),
                      pl.BlockSpec((tt, LANES), tile_map)],
            out_specs=pl.BlockSpec(memory_space=pl.ANY),
            scratch_shapes=[pltpu.VMEM((2, TOP_K * tt * ROW_TILE, LANES), F32),
                            pltpu.VMEM((EXPERT_TM * ROW_TILE, LANES), F32),
                            pltpu.SemaphoreType.DMA((2,)), pltpu.SemaphoreType.DMA],
        ),
        out_shape=jax.ShapeDtypeStruct((n_rows * ROW_TILE, LANES), F32),
        compiler_params=_cparams(("arbitrary",)),
        name="moe_dispatch",
    )(*run_tables, live, h3, loc)


def _expert_kernel(be_ref, live_ref, first_ref, slot_ref, next_ref, x_ref, w1_hbm, b1_ref, w2_hbm, b2_ref,
                   y_ref, w1f, w2f, w1s, w2s, sem, *, layer):
    i = pl.program_id(0)
    live = live_ref[i]

    def weight_copies(e, slot):
        return (pltpu.make_async_copy(w1_hbm.at[layer, e], w1f.at[slot], sem.at[0, slot]),
                pltpu.make_async_copy(w2_hbm.at[layer, e], w2f.at[slot], sem.at[1, slot]))

    @pl.when(i == 0)
    def _():
        for copy in weight_copies(be_ref[0], 0):
            copy.start()

    @pl.when(first_ref[i] == 1)
    def _():
        slot = slot_ref[i]
        for copy in weight_copies(be_ref[i], slot):
            copy.wait()
        nxt = next_ref[i]

        @pl.when(nxt >= 0)
        def _():
            for copy in weight_copies(nxt, 1 - slot):
                copy.start()

        w1s[...] = w1f[slot].astype(BF16)
        w2s[...] = w2f[slot].astype(BF16)

    @pl.when(live > 0)
    def _():
        dff = w2s.shape[0]
        x = _tiles_to_rows(x_ref)
        row = lax.broadcasted_iota(jnp.int32, (x.shape[0], 1), 0)
        x = jnp.where(row < live, x, 0.0).astype(BF16)
        hh = jnp.dot(x, w1s[...], preferred_element_type=F32) + b1_ref[0, 0]
        x_glu = jnp.minimum(hh[:, :dff], SWIGLU_LIMIT)
        x_lin = jnp.clip(hh[:, dff:], -SWIGLU_LIMIT, SWIGLU_LIMIT)
        act = x_glu * _sigmoid(SWIGLU_ALPHA * x_glu) * (x_lin + 1.0)
        y = jnp.dot(act.astype(BF16), w2s[...], preferred_element_type=F32) + b2_ref[0, 0]
        _rows_to_tiles(y_ref, y)

    @pl.when(live == 0)
    def _():
        y_ref[...] = jnp.zeros(y_ref.shape, F32)


def _experts(block_e, live, xb, layer, w1, b1, w2, b2):
    depth, n_e, d, two_f = w1.shape
    dff = two_f // 2
    block_rows = EXPERT_TM * ROW_TILE
    nb = xb.shape[0] // block_rows
    first, slot, next_e = _expert_run_tables(block_e)
    bias_map = lambda i, be, *_: (layer, be[i], 0, 0)
    return pl.pallas_call(
        functools.partial(_expert_kernel, layer=layer),
        grid_spec=pltpu.PrefetchScalarGridSpec(
            num_scalar_prefetch=5,
            grid=(nb,),
            in_specs=[
                pl.BlockSpec((block_rows, LANES), lambda i, *_: (i, 0)),
                pl.BlockSpec(memory_space=pl.ANY),
                pl.BlockSpec((1, 1, 1, two_f), bias_map),
                pl.BlockSpec(memory_space=pl.ANY),
                pl.BlockSpec((1, 1, 1, d), bias_map),
            ],
            out_specs=pl.BlockSpec((block_rows, LANES), lambda i, *_: (i, 0)),
            scratch_shapes=[pltpu.VMEM((2, d, two_f), F32), pltpu.VMEM((2, dff, d), F32),
                            pltpu.VMEM((d, two_f), BF16), pltpu.VMEM((dff, d), BF16),
                            pltpu.SemaphoreType.DMA((2, 2))],
        ),
        out_shape=jax.ShapeDtypeStruct(xb.shape, F32),
        compiler_params=_cparams(("arbitrary",)),
        name="moe_experts",
    )(block_e, live, first, slot, next_e, xb, w1, b1.reshape(depth, n_e, 1, two_f), w2,
      b2.reshape(depth, n_e, 1, d))


def _combine_kernel(src_ref, len_ref, dst_ref, yb_hbm, loc_ref, gate_ref, x_ref, mod_ref, lng_ref, lnb_ref,
                    *rest, with_pw1):
    if with_pw1:
        modn_ref, wa_ref, wb_ref, ba_ref, bb_ref, o_ref, u_ref, stage, sem = rest
    else:
        o_ref, stage, sem = rest
    tt = x_ref.shape[0]
    step = pl.program_id(0)
    n_steps = pl.num_programs(0)

    def start_gather(tile, slot):
        def per_expert(e, carry):
            copies = _run_piece_copies(src_ref, len_ref, dst_ref, tile * N_EXPERTS + e, yb_hbm,
                                       stage.at[slot], sem.at[slot], to_stage=True)
            for wanted, copy in copies:
                pl.when(wanted)(copy.start)
            return carry

        lax.fori_loop(0, N_EXPERTS, per_expert, 0)

    slot = step % 2

    @pl.when(step == 0)
    def _():
        start_gather(step, slot)

    @pl.when(step + 1 < n_steps)
    def _():
        start_gather(step + 1, 1 - slot)

    pltpu.make_async_copy(yb_hbm.at[pl.ds(0, TOP_K * tt * ROW_TILE), :], stage.at[slot], sem.at[slot]).wait()

    staged = _tiles_to_rows(stage.at[slot]).astype(BF16)
    loc = loc_ref[...]
    gates = gate_ref[...]
    pos = lax.broadcasted_iota(jnp.int32, (tt, TOP_K * tt), 1)
    pick = jnp.zeros((tt, TOP_K * tt), F32)
    for k in range(TOP_K):
        pick = jnp.where(pos == loc[:, k:k + 1], gates[:, k:k + 1], pick)
    y = jnp.dot(pick.astype(BF16), staged, preferred_element_type=F32)
    x_new = _residual_ln(x_ref[...], y, mod_ref[0, 5:6, :], lng_ref[...], lnb_ref[...])
    o_ref[...] = x_new
    if with_pw1:
        _pw1_glu_store(x_new, modn_ref, wa_ref, wb_ref, ba_ref, bb_ref, u_ref)


def _combine(run_tables, yb, loc, gates, x1, mod, ln_g, ln_b, seq, next_pw1=None):
    t, d = x1.shape
    tt = ROUTE_TT
    per_b = seq // tt
    const = lambda i, *_: (0, 0)
    tile_map = lambda i, *_: (i, 0)
    in_specs = [
        pl.BlockSpec(memory_space=pl.ANY),
        pl.BlockSpec((tt, LANES), tile_map),
        pl.BlockSpec((tt, LANES), tile_map),
        pl.BlockSpec((tt, d), tile_map),
        pl.BlockSpec((1, 6, d), lambda i, *_: (i // per_b, 0, 0)),
        pl.BlockSpec((1, d), const),
        pl.BlockSpec((1, d), const),
    ]
    operands = [*run_tables, yb, loc, gates, x1, mod, ln_g.reshape(1, d), ln_b.reshape(1, d)]
    out_specs = [pl.BlockSpec((tt, d), tile_map)]
    out_shape = [jax.ShapeDtypeStruct((t, d), F32)]
    if next_pw1 is not None:
        mod_next, pw1_w, pw1_b = next_pw1
        inner = pw1_w.shape[1] // 2
        assert inner == ROW_TILE * LANES
        w = pw1_w.astype(BF16)
        b = pw1_b.reshape(1, 2 * inner)
        in_specs += [
            pl.BlockSpec((1, 6, d), lambda i, *_: (i // per_b, 0, 0)),
            pl.BlockSpec((d, inner), const),
            pl.BlockSpec((d, inner), lambda i, *_: (0, 1)),
            pl.BlockSpec((1, inner), const),
            pl.BlockSpec((1, inner), lambda i, *_: (0, 1)),
        ]
        operands += [mod_next, w, w, b, b]
        out_specs.append(pl.BlockSpec((tt * ROW_TILE, LANES), tile_map))
        out_shape.append(jax.ShapeDtypeStruct((t * ROW_TILE, LANES), F32))
    return pl.pallas_call(
        functools.partial(_combine_kernel, with_pw1=next_pw1 is not None),
        grid_spec=pltpu.PrefetchScalarGridSpec(
            num_scalar_prefetch=3,
            grid=(t // tt,),
            in_specs=in_specs,
            out_specs=out_specs,
            scratch_shapes=[pltpu.VMEM((2, TOP_K * tt * ROW_TILE, LANES), F32), pltpu.SemaphoreType.DMA((2,))],
        ),
        out_shape=out_shape,
        compiler_params=_cparams(("arbitrary",)),
        name="moe_combine_ln",
    )(*operands)


def _moe_block(x1, h3, idx, gates, mod, layer, w1, b1, w2, b2, ln_g, ln_b, seq, next_pw1=None):
    t = x1.shape[0]
    loc, tile_runs, counts = _rank_and_count(idx)
    pstarts, block_e, live = _routing_tables(t, counts)
    run_tables = _run_tables(pstarts, tile_runs)
    n_rows = _num_expert_blocks(t * TOP_K) * EXPERT_TM
    xb = _dispatch(run_tables, live, h3, loc, n_rows)
    yb = _experts(block_e, live, xb, layer, w1, b1, w2, b2)
    return _combine(run_tables, yb, loc, gates, x1, mod, ln_g, ln_b, seq, next_pw1)


def kernel(x, c, ada_w, ada_b, dn_in_w, dn_conv_w, dn_A_log, dn_dt_bias, dn_onorm_w, dn_out_w,
           cf_pw1_w, cf_pw1_b, cf_dw_w, cf_dw_b, cf_ln_g, cf_ln_b, cf_pw2_w, cf_pw2_b,
           ln1_g, ln1_b, router_w, router_b, e_w1, e_b1, e_w2, e_b2, ln2_g, ln2_b):
    bsz, seq, d = x.shape
    mods = _ada_ln(c, ada_w, ada_b)

    proj, ba = _dn_in_proj(x, mods[0], dn_in_w[0])
    qn, kn, vv, gates_dn = _dn_prep(proj, ba, dn_conv_w[0], dn_A_log[0], dn_dt_bias[0])
    o = _dn_chunk(qn, kn, vv, proj, gates_dn, dn_onorm_w[0])
    x1, h3, idx, gates = _dn_out(o, dn_out_w[0], x, mods[0], ln1_g[0], ln1_b[0], router_w[0], router_b[0])
    x2, u = _moe_block(x1, h3, idx, gates, mods[0], 0, e_w1, e_b1, e_w2, e_b2, ln2_g[0], ln2_b[0], seq,
                       next_pw1=(mods[1], cf_pw1_w[0], cf_pw1_b[0]))

    x2 = x2.reshape(bsz, seq, d)
    x3, h3, idx, gates = _cf_tail(u, cf_dw_w[0], cf_dw_b[0], cf_ln_g[0], cf_ln_b[0], cf_pw2_w[0], cf_pw2_b[0],
                                  x2, mods[1], ln1_g[1], ln1_b[1], router_w[1], router_b[1])
    (x4,) = _moe_block(x3, h3, idx, gates, mods[1], 1, e_w1, e_b1, e_w2, e_b2, ln2_g[1], ln2_b[1], seq)
    return x4.reshape(bsz, seq, d)
```

```python
import functools

import jax
import jax.numpy as jnp
from jax import lax
from jax.experimental import pallas as pl
from jax.experimental.pallas import tpu as pltpu

F32 = jnp.float32
BF16 = jnp.bfloat16
HIGHEST = lax.Precision.HIGHEST

DEPTH = 2
DN_QK_HEADS = 8
DN_V_HEADS = 16
DN_HEAD_DIM = 128
DN_CONV = 4
DN_CHUNK = 64
CF_KERNEL = 31
N_EXPERTS = 32
TOP_K = 4
SWIGLU_LIMIT = 7.0
SWIGLU_ALPHA = 1.702
LN_EPS = 1e-5
RMS_EPS = 1e-6
L2_EPS = 1e-6
DEEPNORM_ALPHA = (2 * DEPTH) ** 0.25

LANES = 128
SUBLANES = 8
BF16_SUBLANES = 16
VMEM_LIMIT = 56 * 1024 * 1024

ADA_TN = 1536
PROJ_TM = 1024
PROJ_TN = 2048
PREP_TS = 512
CHUNKS_PER_STEP = 8
DN_CHUNK_UNROLL = 4
POST_TM = 512
ROUTE_TT = 256
RANK_TILES_PER_STEP = 4
EXPERT_TM = 512
CF_TM = 512
CF_HALO = 32
CF_CONV_ROWS = 32

NEG_BIG = -1e30


def _cparams(sem):
    return pltpu.CompilerParams(dimension_semantics=sem, vmem_limit_bytes=VMEM_LIMIT)


def _sigmoid(x):
    return jax.nn.sigmoid(x)


def _layer_norm(v, g, b):
    mu = jnp.mean(v, -1, keepdims=True)
    d = v - mu
    var = jnp.mean(d * d, -1, keepdims=True)
    return d * lax.rsqrt(var + LN_EPS) * g + b


ROW_TILE = SUBLANES


def _rows_to_tiles(o_ref, val):
    rows = val.shape[0]
    for s in range(ROW_TILE):
        o_ref[pl.ds(s, rows, stride=ROW_TILE), :] = val[:, s * LANES:(s + 1) * LANES]


def _tiles_to_rows(x_ref):
    rows = x_ref.shape[0] // ROW_TILE
    return jnp.concatenate([x_ref[pl.ds(s, rows, stride=ROW_TILE), :] for s in range(ROW_TILE)], axis=1)


def _split_weight(w):
    k, n = w.shape
    hi = w.astype(BF16)
    lo = (w - hi.astype(F32)).astype(BF16)
    out = jnp.zeros((k, 2 * LANES), BF16)
    return out.at[:, :n].set(hi).at[:, LANES:LANES + n].set(lo)


def _narrow_dot(h, w_split_ref):
    m = h.shape[0]
    hi = h.astype(BF16)
    lo = (h - hi.astype(F32)).astype(BF16)
    out = jnp.dot(jnp.concatenate([hi, lo], axis=0), w_split_ref[...], preferred_element_type=F32)
    return out[:m, :LANES] + out[:m, LANES:] + out[m:, :LANES]


def _ada_kernel(c_ref, w_ref, b_ref, o_ref):
    c = c_ref[...]
    cond = c * _sigmoid(c)
    o_ref[0] = jnp.dot(cond, w_ref[0], precision=HIGHEST, preferred_element_type=F32) + b_ref[0]


def _ada_ln(c, ada_w, ada_b):
    depth, d, n = ada_w.shape
    bsz = c.shape[0]
    c_pad = jnp.zeros((SUBLANES, d), F32).at[:bsz].set(c)
    out = pl.pallas_call(
        _ada_kernel,
        grid=(depth, n // ADA_TN),
        in_specs=[
            pl.BlockSpec((SUBLANES, d), lambda i, j: (0, 0)),
            pl.BlockSpec((1, d, ADA_TN), lambda i, j: (i, 0, j)),
            pl.BlockSpec((1, 1, ADA_TN), lambda i, j: (i, 0, j)),
        ],
        out_specs=pl.BlockSpec((1, SUBLANES, ADA_TN), lambda i, j: (i, 0, j)),
        out_shape=jax.ShapeDtypeStruct((depth, SUBLANES, n), F32),
        compiler_params=_cparams(("parallel", "parallel")),
        name="ada_ln",
    )(c_pad, ada_w, ada_b.reshape(depth, 1, n))
    return out[:, :bsz].reshape(depth, bsz, 6, d)


def _inproj_kernel(x_ref, mod_ref, w_ref, wba_ref, proj_ref, ba_ref, h_scr):
    @pl.when(pl.program_id(2) == 0)
    def _():
        h = x_ref[0] * (1.0 + mod_ref[0, 1:2, :]) + mod_ref[0, 0:1, :]
        h_scr[...] = h.astype(BF16)
        ba_ref[0] = _narrow_dot(h, wba_ref)

    proj_ref[0] = jnp.dot(h_scr[...], w_ref[...], preferred_element_type=F32).astype(BF16)


def _dn_in_proj(x, mod, in_w):
    bsz, seq, d = x.shape
    n_main = in_w.shape[1] - 2 * DN_V_HEADS
    w_main = in_w[:, :n_main].astype(BF16)
    w_ba = _split_weight(in_w[:, n_main:])
    tm = min(PROJ_TM, seq)
    return pl.pallas_call(
        _inproj_kernel,
        grid=(bsz, seq // tm, n_main // PROJ_TN),
        in_specs=[
            pl.BlockSpec((1, tm, d), lambda b, i, j: (b, i, 0)),
            pl.BlockSpec((1, 6, d), lambda b, i, j: (b, 0, 0)),
            pl.BlockSpec((d, PROJ_TN), lambda b, i, j: (0, j)),
            pl.BlockSpec((d, 2 * LANES), lambda b, i, j: (0, 0)),
        ],
        out_specs=[
            pl.BlockSpec((1, tm, PROJ_TN), lambda b, i, j: (b, i, j)),
            pl.BlockSpec((1, tm, LANES), lambda b, i, j: (b, i, 0)),
        ],
        out_shape=[
            jax.ShapeDtypeStruct((bsz, seq, n_main), BF16),
            jax.ShapeDtypeStruct((bsz, seq, LANES), F32),
        ],
        scratch_shapes=[pltpu.VMEM((tm, d), BF16)],
        compiler_params=_cparams(("parallel", "parallel", "arbitrary")),
        name="dn_in_proj",
    )(x, mod, w_main, w_ba)


def _dn_prep_kernel(q_ref, k_ref, v_ref, qh_ref, kh_ref, vh_ref, cw_ref, ba_ref, alog_ref, dt_ref,
                    qo_ref, ko_ref, vo_ref, g_ref, scr):
    ts = q_ref.shape[1]
    halo = qh_ref.shape[1]
    keep = (pl.program_id(1) > 0).astype(F32)

    def conv_silu(x_ref, h_ref, c0):
        width = x_ref.shape[2]
        scr[0:halo, 0:width] = h_ref[0].astype(F32) * keep
        scr[halo:halo + ts, 0:width] = x_ref[0].astype(F32)
        acc = None
        for j in range(DN_CONV):
            off = halo - (DN_CONV - 1) + j
            term = scr[off:off + ts, 0:width] * cw_ref[j:j + 1, c0:c0 + width]
            acc = term if acc is None else acc + term
        return acc * _sigmoid(acc)

    def l2norm_store(o_ref, x, scale):
        for h in range(x.shape[1] // DN_HEAD_DIM):
            xh = x[:, h * DN_HEAD_DIM:(h + 1) * DN_HEAD_DIM]
            ss = jnp.sum(xh * xh, -1, keepdims=True)
            o_ref[0, :, h * DN_HEAD_DIM:(h + 1) * DN_HEAD_DIM] = (
                xh * lax.rsqrt(ss + L2_EPS) * scale).astype(o_ref.dtype)

    kd = q_ref.shape[2]
    l2norm_store(qo_ref, conv_silu(q_ref, qh_ref, 0), DN_HEAD_DIM ** -0.5)
    l2norm_store(ko_ref, conv_silu(k_ref, kh_ref, kd), 1.0)
    vo_ref[0] = conv_silu(v_ref, vh_ref, 2 * kd).astype(vo_ref.dtype)

    ba = ba_ref[0]
    beta = _sigmoid(ba)
    zz = ba + dt_ref[...]
    softplus = jnp.maximum(zz, 0.0) + jnp.log1p(jnp.exp(-jnp.abs(zz)))
    g = -jnp.exp(alog_ref[...]) * softplus
    r = lax.broadcasted_iota(jnp.int32, (ts, ts), 0)
    c = lax.broadcasted_iota(jnp.int32, (ts, ts), 1)
    in_chunk_tril = ((r // DN_CHUNK == c // DN_CHUNK) & (c <= r)).astype(BF16)
    g1 = g.astype(BF16)
    g2 = (g - g1.astype(F32)).astype(BF16)
    g3 = (g - g1.astype(F32) - g2.astype(F32)).astype(BF16)
    parts = jnp.dot(in_chunk_tril, jnp.concatenate([g1, g2, g3], axis=1), preferred_element_type=F32)
    gcum = parts[:, :LANES] + parts[:, LANES:2 * LANES] + parts[:, 2 * LANES:]
    lane = lax.broadcasted_iota(jnp.int32, ba.shape, 1)
    g_ref[0] = jnp.where(lane < DN_V_HEADS, beta, gcum)


def _dn_prep(proj, ba, conv_w, a_log, dt_bias):
    bsz, seq, _ = proj.shape
    kd = DN_QK_HEADS * DN_HEAD_DIM
    vd = DN_V_HEADS * DN_HEAD_DIM
    ts = min(PREP_TS, seq)
    halo = BF16_SUBLANES
    hb = ts // halo
    alog_row = jnp.zeros((1, LANES), F32).at[0, DN_V_HEADS:2 * DN_V_HEADS].set(a_log)
    dt_row = jnp.zeros((1, LANES), F32).at[0, DN_V_HEADS:2 * DN_V_HEADS].set(dt_bias)

    def halo_map(col):
        return lambda b, i: (b, jnp.maximum(i * hb - 1, 0), col)

    return pl.pallas_call(
        _dn_prep_kernel,
        grid=(bsz, seq // ts),
        in_specs=[
            pl.BlockSpec((1, ts, kd), lambda b, i: (b, i, 0)),
            pl.BlockSpec((1, ts, kd), lambda b, i: (b, i, 1)),
            pl.BlockSpec((1, ts, vd), lambda b, i: (b, i, 1)),
            pl.BlockSpec((1, halo, kd), halo_map(0)),
            pl.BlockSpec((1, halo, kd), halo_map(1)),
            pl.BlockSpec((1, halo, vd), halo_map(1)),
            pl.BlockSpec((DN_CONV, 2 * kd + vd), lambda b, i: (0, 0)),
            pl.BlockSpec((1, ts, LANES), lambda b, i: (b, i, 0)),
            pl.BlockSpec((1, LANES), lambda b, i: (0, 0)),
            pl.BlockSpec((1, LANES), lambda b, i: (0, 0)),
        ],
        out_specs=[
            pl.BlockSpec((1, ts, kd), lambda b, i: (b, i, 0)),
            pl.BlockSpec((1, ts, kd), lambda b, i: (b, i, 0)),
            pl.BlockSpec((1, ts, vd), lambda b, i: (b, i, 0)),
            pl.BlockSpec((1, ts, LANES), lambda b, i: (b, i, 0)),
        ],
        out_shape=[
            jax.ShapeDtypeStruct((bsz, seq, kd), BF16),
            jax.ShapeDtypeStruct((bsz, seq, kd), BF16),
            jax.ShapeDtypeStruct((bsz, seq, vd), BF16),
            jax.ShapeDtypeStruct((bsz, seq, LANES), F32),
        ],
        scratch_shapes=[pltpu.VMEM((ts + halo, vd), F32)],
        compiler_params=_cparams(("parallel", "arbitrary")),
        name="dn_prep",
    )(proj, proj, proj, proj, proj, proj, conv_w, ba, alog_row, dt_row)


def _bmm(a, b):
    return lax.dot_general(a.astype(BF16), b.astype(BF16), (((2,), (1,)), ((0,), (0,))),
                           preferred_element_type=F32)


def _bmm_nt(a, b):
    return lax.dot_general(a.astype(BF16), b.astype(BF16), (((2,), (2,)), ((0,), (0,))),
                           preferred_element_type=F32)


def _bmm_tn(a, b):
    return lax.dot_general(a.astype(BF16), b.astype(BF16), (((1,), (1,)), ((0,), (0,))),
                           preferred_element_type=F32)


def _unit_lower_inverse_wide(a_twice, upper, eye_upper):
    c = a_twice.shape[1]
    x = jnp.where(upper, eye_upper, -a_twice)
    span = 1
    while span < c:
        x = _bmm(x[:, :, :c], x) + jnp.where(upper, x, 0.0)
        span *= 2
    return x


def _dn_chunk_kernel(q_ref, k_ref, v_ref, z_ref, g_ref, gt_ref, ow_ref, o_ref, s_ref):
    c_len = DN_CHUNK
    dh = DN_HEAD_DIM
    rep = DN_V_HEADS // DN_QK_HEADS

    @pl.when(pl.program_id(1) == 0)
    def _():
        s_ref[...] = jnp.zeros(s_ref.shape, F32)

    heads = range(DN_V_HEADS)
    n_heads = DN_V_HEADS
    ri = lax.broadcasted_iota(jnp.int32, (1, c_len, 2 * c_len), 1)
    lane = lax.broadcasted_iota(jnp.int32, (1, c_len, 2 * c_len), 2)
    upper = lane >= c_len
    ci = jnp.where(upper, lane - c_len, lane)
    causal = ci <= ri
    strict = ci < ri
    eye_upper = ((ci == ri) & upper).astype(F32)
    onorm = ow_ref[...]

    def head_cols(ref, rows, h):
        return ref[0, rows, h * dh:(h + 1) * dh]

    def per_v_head(t):
        return jnp.stack([t[h // rep] for h in heads])

    def state_free_part(c):
        r0 = pl.multiple_of(c * c_len, c_len)
        rows = pl.ds(r0, c_len)
        gcols = g_ref[0, rows, :]
        grows = gt_ref[0, c]
        wide = (n_heads, c_len, dh)
        beta_b = jnp.broadcast_to(jnp.stack([gcols[:, h:h + 1] for h in heads]), wide)
        g_b = jnp.broadcast_to(
            jnp.stack([gcols[:, DN_V_HEADS + h:DN_V_HEADS + h + 1] for h in heads]), wide)
        g_r = jnp.stack([grows[DN_V_HEADS + h:DN_V_HEADS + h + 1, :] for h in heads])
        g_r = jnp.concatenate([g_r, g_r], axis=2)
        g_last = g_b[:, c_len - 1:c_len, :]

        qn = jnp.stack([head_cols(q_ref, rows, hq) for hq in range(DN_QK_HEADS)])
        kn = jnp.stack([head_cols(k_ref, rows, hq) for hq in range(DN_QK_HEADS)])
        qk_kk = _bmm_nt(jnp.concatenate([qn, kn], axis=1), jnp.concatenate([kn, kn], axis=1))
        qk = per_v_head(qk_kk[:, :c_len])
        kk = per_v_head(qk_kk[:, c_len:])
        qf = per_v_head(qn).astype(F32)
        kf = per_v_head(kn).astype(F32)
        vf = jnp.stack([head_cols(v_ref, rows, h) for h in heads]).astype(F32)

        decay = jnp.where(causal, jnp.exp(g_b - g_r), 0.0)
        a_twice = jnp.where(strict, beta_b * kk * decay, 0.0)
        x_inv = _unit_lower_inverse_wide(a_twice, upper, eye_upper)
        eg = jnp.exp(g_b)
        rhs = jnp.concatenate([vf * beta_b, kf * (beta_b * eg)], axis=2)
        sol = _bmm(x_inv, jnp.concatenate([jnp.zeros_like(rhs), rhs], axis=1))
        u = sol[:, :, :dh]
        w = sol[:, :, dh:]
        qkm = jnp.where(causal, qk * decay, 0.0)[:, :, :c_len]
        qg = qf * eg
        kdec = kf * jnp.exp(g_last - g_b)
        return rows, jnp.concatenate([w, qg], axis=1), u, qkm, kdec, jnp.exp(g_last)

    def state_part(rows, w_qg, u, qkm, kdec, chunk_decay):
        state = s_ref[...]
        ws = _bmm(w_qg, state)
        v_new = u - ws[:, :c_len]
        o = ws[:, c_len:] + _bmm(qkm, v_new)
        s_ref[...] = state * chunk_decay + _bmm_tn(kdec, v_new)

        o = o * lax.rsqrt(jnp.mean(o * o, -1, keepdims=True) + RMS_EPS) * onorm
        for h in heads:
            zf = head_cols(z_ref, rows, h).astype(F32)
            o_ref[0, rows, h * dh:(h + 1) * dh] = (o[h] * (zf * _sigmoid(zf))).astype(o_ref.dtype)

    def chunk_group(i, carry):
        prepared = [state_free_part(i * DN_CHUNK_UNROLL + j) for j in range(DN_CHUNK_UNROLL)]
        for args in prepared:
            state_part(*args)
        return carry

    lax.fori_loop(0, q_ref.shape[1] // (c_len * DN_CHUNK_UNROLL), chunk_group, 0)


def _dn_chunk(qn, kn, vv, proj, gates, onorm_w):
    bsz, seq, kd = qn.shape
    vd = vv.shape[2]
    n_chunks = seq // DN_CHUNK
    cb = min(CHUNKS_PER_STEP, n_chunks)
    rows = cb * DN_CHUNK
    gates_t = jnp.swapaxes(gates[:, :, :2 * DN_V_HEADS].reshape(bsz, n_chunks, DN_CHUNK, 2 * DN_V_HEADS), 2, 3)
    z_col = (2 * kd + vd) // vd
    return pl.pallas_call(
        _dn_chunk_kernel,
        grid=(bsz, n_chunks // cb),
        in_specs=[
            pl.BlockSpec((1, rows, kd), lambda b, n: (b, n, 0)),
            pl.BlockSpec((1, rows, kd), lambda b, n: (b, n, 0)),
            pl.BlockSpec((1, rows, vd), lambda b, n: (b, n, 0)),
            pl.BlockSpec((1, rows, vd), lambda b, n: (b, n, z_col)),
            pl.BlockSpec((1, rows, LANES), lambda b, n: (b, n, 0)),
            pl.BlockSpec((1, cb, 2 * DN_V_HEADS, DN_CHUNK), lambda b, n: (b, n, 0, 0)),
            pl.BlockSpec((1, DN_HEAD_DIM), lambda b, n: (0, 0)),
        ],
        out_specs=pl.BlockSpec((1, rows, vd), lambda b, n: (b, n, 0)),
        out_shape=jax.ShapeDtypeStruct((bsz, seq, vd), BF16),
        scratch_shapes=[pltpu.VMEM((DN_V_HEADS, DN_HEAD_DIM, DN_HEAD_DIM), F32)],
        compiler_params=_cparams(("parallel", "arbitrary")),
        name="dn_chunk",
    )(qn, kn, vv, proj, gates, gates_t, onorm_w.reshape(1, DN_HEAD_DIM))


def _residual_ln(x, y, gate_row, ln_g, ln_b):
    return _layer_norm(DEEPNORM_ALPHA * x + (1.0 + gate_row) * y, ln_g, ln_b)


def _route_store(h, rw_ref, rb_ref, idx_ref, gate_ref):
    logits = _narrow_dot(h, rw_ref) + rb_ref[...]
    lane = lax.broadcasted_iota(jnp.int32, logits.shape, 1).astype(F32)
    work = logits
    idx_out = jnp.zeros(logits.shape, F32)
    val_out = jnp.full(logits.shape, NEG_BIG, F32)
    for k in range(TOP_K):
        m = jnp.max(work, -1, keepdims=True)
        am = jnp.min(jnp.where(work == m, lane, float(LANES)), -1, keepdims=True)
        idx_out = jnp.where(lane == k, am, idx_out)
        val_out = jnp.where(lane == k, m, val_out)
        work = jnp.where(lane == am, NEG_BIG * 2.0, work)
    top = jnp.max(val_out, -1, keepdims=True)
    e = jnp.where(lane < TOP_K, jnp.exp(val_out - top), 0.0)
    idx_ref[...] = idx_out.astype(jnp.int32)
    gate_ref[...] = e / jnp.sum(e, -1, keepdims=True)


def _post_mixer_tail(x, y, mod_ref, lng_ref, lnb_ref, rw_ref, rb_ref, x1_ref, h3_ref, idx_ref, gate_ref):
    x1 = _residual_ln(x, y, mod_ref[0, 2:3, :], lng_ref[...], lnb_ref[...])
    x1_ref[...] = x1
    h2 = x1 * (1.0 + mod_ref[0, 4:5, :]) + mod_ref[0, 3:4, :]
    _rows_to_tiles(h3_ref, h2)
    _route_store(h2, rw_ref, rb_ref, idx_ref, gate_ref)


def _router_operands(router_w, router_b):
    rb = jnp.full((1, LANES), NEG_BIG, F32).at[0, :N_EXPERTS].set(router_b)
    return _split_weight(router_w), rb


def _post_out_specs(tm, d):
    return [
        pl.BlockSpec((tm, d), lambda i: (i, 0)),
        pl.BlockSpec((tm * ROW_TILE, LANES), lambda i: (i, 0)),
        pl.BlockSpec((tm, LANES), lambda i: (i, 0)),
        pl.BlockSpec((tm, LANES), lambda i: (i, 0)),
    ]


def _post_out_shapes(t, d):
    assert d == ROW_TILE * LANES
    return [
        jax.ShapeDtypeStruct((t, d), F32),
        jax.ShapeDtypeStruct((t * ROW_TILE, LANES), F32),
        jax.ShapeDtypeStruct((t, LANES), jnp.int32),
        jax.ShapeDtypeStruct((t, LANES), F32),
    ]


def _dn_out_kernel(o_ref, w_ref, x_ref, mod_ref, lng_ref, lnb_ref, rw_ref, rb_ref,
                   x1_ref, h3_ref, idx_ref, gate_ref):
    y = jnp.dot(o_ref[...], w_ref[...], preferred_element_type=F32)
    _post_mixer_tail(x_ref[...], y, mod_ref, lng_ref, lnb_ref, rw_ref, rb_ref,
                     x1_ref, h3_ref, idx_ref, gate_ref)


def _dn_out(o, out_w, x, mod, ln_g, ln_b, router_w, router_b):
    bsz, seq, d = x.shape
    t = bsz * seq
    vd = o.shape[2]
    tm = min(POST_TM, seq)
    rw, rb = _router_operands(router_w, router_b)
    per_b = seq // tm
    return pl.pallas_call(
        _dn_out_kernel,
        grid=(t // tm,),
        in_specs=[
            pl.BlockSpec((tm, vd), lambda i: (i, 0)),
            pl.BlockSpec((vd, d), lambda i: (0, 0)),
            pl.BlockSpec((tm, d), lambda i: (i, 0)),
            pl.BlockSpec((1, 6, d), lambda i: (i // per_b, 0, 0)),
            pl.BlockSpec((1, d), lambda i: (0, 0)),
            pl.BlockSpec((1, d), lambda i: (0, 0)),
            pl.BlockSpec((d, 2 * LANES), lambda i: (0, 0)),
            pl.BlockSpec((1, LANES), lambda i: (0, 0)),
        ],
        out_specs=_post_out_specs(tm, d),
        out_shape=_post_out_shapes(t, d),
        compiler_params=_cparams(("parallel",)),
        name="dn_out_ln_route",
    )(o.reshape(t, vd), out_w.astype(BF16), x.reshape(t, d), mod, ln_g.reshape(1, d), ln_b.reshape(1, d), rw, rb)


def _pw1_glu_store(x, mod_ref, wa_ref, wb_ref, ba_ref, bb_ref, u_ref):
    h = (x * (1.0 + mod_ref[0, 1:2, :]) + mod_ref[0, 0:1, :]).astype(BF16)
    pa = jnp.dot(h, wa_ref[...], preferred_element_type=F32) + ba_ref[...]
    pb = jnp.dot(h, wb_ref[...], preferred_element_type=F32) + bb_ref[...]
    _rows_to_tiles(u_ref, pa * _sigmoid(pb))


def _cf_tail_kernel(u_ref, uh_ref, dw_ref, dwb_ref, cg_ref, cb_ref, w2_ref, b2_ref,
                    x_ref, mod_ref, lng_ref, lnb_ref, rw_ref, rb_ref,
                    x1_ref, h3_ref, idx_ref, gate_ref, scr, conv_scr):
    tm = u_ref.shape[0] // ROW_TILE
    halo = uh_ref.shape[0] // ROW_TILE
    keep = (pl.program_id(1) > 0).astype(F32)
    scr[0:halo * ROW_TILE, :] = uh_ref[...] * keep
    scr[halo * ROW_TILE:(halo + tm) * ROW_TILE, :] = u_ref[...]
    first = halo - (CF_KERNEL - 1)
    blk = CF_CONV_ROWS * ROW_TILE

    def conv_tokens(tb, carry):
        r0 = pl.multiple_of(tb * blk, blk)
        acc = jnp.concatenate([dwb_ref[...]] * CF_CONV_ROWS, axis=0)
        for j in range(CF_KERNEL):
            w_tile = dw_ref[j * ROW_TILE:(j + 1) * ROW_TILE, :]
            w_blk = jnp.concatenate([w_tile] * CF_CONV_ROWS, axis=0)
            acc = acc + scr[pl.ds(r0 + (first + j) * ROW_TILE, blk), :] * w_blk
        conv_scr[pl.ds(r0, blk), :] = acc
        return carry

    lax.fori_loop(0, tm // CF_CONV_ROWS, conv_tokens, 0)
    conv = _tiles_to_rows(conv_scr)
    normed = _layer_norm(conv, cg_ref[...], cb_ref[...])
    act = normed * _sigmoid(normed)
    y = jnp.dot(act.astype(BF16), w2_ref[...], preferred_element_type=F32) + b2_ref[...]
    _post_mixer_tail(x_ref[0], y, mod_ref, lng_ref, lnb_ref, rw_ref, rb_ref,
                     x1_ref, h3_ref, idx_ref, gate_ref)


def _cf_tail(u, dw_w, dw_b, cf_ln_g, cf_ln_b, pw2_w, pw2_b, x, mod, ln_g, ln_b, router_w, router_b):
    bsz, seq, d = x.shape
    inner = dw_w.shape[1]
    t = bsz * seq
    tm = min(CF_TM, seq)
    per_b = seq // tm
    hb = tm // CF_HALO
    halos_per_b = seq // CF_HALO
    rw, rb = _router_operands(router_w, router_b)
    row = lambda v: v.reshape(1, -1)
    tiles = lambda v: v.reshape(-1, LANES)
    const = lambda b, i: (0, 0)
    flat = lambda b, i: (b * per_b + i, 0)
    return pl.pallas_call(
        _cf_tail_kernel,
        grid=(bsz, per_b),
        in_specs=[
            pl.BlockSpec((tm * ROW_TILE, LANES), flat),
            pl.BlockSpec((CF_HALO * ROW_TILE, LANES),
                         lambda b, i: (b * halos_per_b + jnp.maximum(i * hb - 1, 0), 0)),
            pl.BlockSpec((CF_KERNEL * ROW_TILE, LANES), const),
            pl.BlockSpec((ROW_TILE, LANES), const),
            pl.BlockSpec((1, inner), const),
            pl.BlockSpec((1, inner), const),
            pl.BlockSpec((inner, d), const),
            pl.BlockSpec((1, d), const),
            pl.BlockSpec((1, tm, d), lambda b, i: (b, i, 0)),
            pl.BlockSpec((1, 6, d), lambda b, i: (b, 0, 0)),
            pl.BlockSpec((1, d), const),
            pl.BlockSpec((1, d), const),
            pl.BlockSpec((d, 2 * LANES), const),
            pl.BlockSpec((1, LANES), const),
        ],
        out_specs=[
            pl.BlockSpec((tm, d), flat),
            pl.BlockSpec((tm * ROW_TILE, LANES), flat),
            pl.BlockSpec((tm, LANES), flat),
            pl.BlockSpec((tm, LANES), flat),
        ],
        out_shape=_post_out_shapes(t, d),
        scratch_shapes=[pltpu.VMEM(((tm + CF_HALO) * ROW_TILE, LANES), F32),
                        pltpu.VMEM((tm * ROW_TILE, LANES), F32)],
        compiler_params=_cparams(("parallel", "arbitrary")),
        name="cf_conv_ln_pw2_route",
    )(u, u, tiles(dw_w), tiles(dw_b), row(cf_ln_g), row(cf_ln_b), pw2_w.astype(BF16), row(pw2_b),
      x, mod, row(ln_g), row(ln_b), rw, rb)


def _rank_kernel(idx_ref, loc_ref, tile_ref, cnt_ref, carry):
    tt = ROUTE_TT

    @pl.when(pl.program_id(0) == 0)
    def _():
        carry[...] = jnp.zeros(carry.shape, F32)

    lane = lax.broadcasted_iota(jnp.int32, (tt, LANES), 1)
    r = lax.broadcasted_iota(jnp.int32, (tt, tt), 0)
    c = lax.broadcasted_iota(jnp.int32, (tt, tt), 1)
    earlier = (c < r).astype(BF16)
    e_from = lax.broadcasted_iota(jnp.int32, (LANES, LANES), 0)
    e_to = lax.broadcasted_iota(jnp.int32, (LANES, LANES), 1)
    lower_experts = (e_from < e_to).astype(BF16)
    row = lax.broadcasted_iota(jnp.int32, (SUBLANES, LANES), 0)

    before_tile = carry[...]
    for sub in range(idx_ref.shape[0] // tt):
        rows = slice(sub * tt, (sub + 1) * tt)
        idx = idx_ref[rows, :]
        sel = [lane == idx[:, k:k + 1] for k in range(TOP_K)]
        multi_hot = sel[0]
        for k in range(1, TOP_K):
            multi_hot = multi_hot | sel[k]
        mh = multi_hot.astype(BF16)
        before_in_tile = jnp.dot(earlier, mh, preferred_element_type=F32)
        run_len = jnp.sum(mh.astype(F32), 0, keepdims=True)
        run_start = jnp.dot(jnp.broadcast_to(run_len, (SUBLANES, LANES)).astype(BF16), lower_experts,
                            preferred_element_type=F32)[:1]
        loc = jnp.zeros(idx.shape, F32)
        for k in range(TOP_K):
            lk = jnp.sum(jnp.where(sel[k], before_in_tile + run_start, 0.0), -1, keepdims=True)
            loc = jnp.where(lane == k, lk, loc)
        loc_ref[rows, :] = loc.astype(jnp.int32)
        tile_ref[sub] = jnp.where(row == 0, before_tile,
                                  jnp.where(row == 1, run_len, run_start)).astype(jnp.int32)
        before_tile = before_tile + run_len
    carry[...] = before_tile
    cnt_ref[...] = before_tile.astype(jnp.int32)


def _rank_and_count(idx):
    t = idx.shape[0]
    tt = ROUTE_TT
    assert tt <= 256
    per_step = RANK_TILES_PER_STEP
    rows = per_step * tt
    assert t % rows == 0
    return pl.pallas_call(
        _rank_kernel,
        grid=(t // rows,),
        in_specs=[pl.BlockSpec((rows, LANES), lambda i: (i, 0))],
        out_specs=[
            pl.BlockSpec((rows, LANES), lambda i: (i, 0)),
            pl.BlockSpec((per_step, SUBLANES, LANES), lambda i: (i, 0, 0)),
            pl.BlockSpec((1, LANES), lambda i: (0, 0)),
        ],
        out_shape=[
            jax.ShapeDtypeStruct((t, LANES), jnp.int32),
            jax.ShapeDtypeStruct((t // tt, SUBLANES, LANES), jnp.int32),
            jax.ShapeDtypeStruct((1, LANES), jnp.int32),
        ],
        scratch_shapes=[pltpu.VMEM((1, LANES), F32)],
        compiler_params=_cparams(("arbitrary",)),
        name="moe_rank_count",
    )(idx)


def _num_expert_blocks(n_assign):
    bound = n_assign + N_EXPERTS * (EXPERT_TM - 1)
    return -(-bound // EXPERT_TM)


def _routing_tables(n_tokens, counts):
    nb = _num_expert_blocks(n_tokens * TOP_K)
    counts = counts[0, :N_EXPERTS]
    padded = (counts + EXPERT_TM - 1) // EXPERT_TM * EXPERT_TM
    pends = jnp.cumsum(padded)
    pstarts = (pends - padded).astype(jnp.int32)
    block_row0 = jnp.arange(nb, dtype=jnp.int32) * EXPERT_TM
    segments_done = jnp.sum((pends[None, :] <= block_row0[:, None]).astype(jnp.int32), axis=1)
    block_e = jnp.minimum(segments_done, N_EXPERTS - 1)
    r0 = block_row0[:, None]
    in_segment = (pstarts[None, :] <= r0) & (r0 < pends[None, :])
    live_if = jnp.clip(pstarts[None, :] + counts[None, :] - r0, 0, EXPERT_TM)
    live = jnp.sum(jnp.where(in_segment, live_if, 0), axis=1).astype(jnp.int32)
    return pstarts, block_e, live


def _expert_run_tables(block_e):
    nb = block_e.shape[0]
    pos = jnp.arange(nb, dtype=jnp.int32)
    first = jnp.concatenate([jnp.ones((1,), jnp.int32), (block_e[1:] != block_e[:-1]).astype(jnp.int32)])
    slot = (jnp.cumsum(first) - 1) % 2
    start_at_or_after = lax.cummin(jnp.where(first == 1, pos, nb), reverse=True)
    next_start = jnp.concatenate([start_at_or_after[1:], jnp.full((1,), nb, jnp.int32)])
    picks = next_start[:, None] == pos[None, :]
    next_e = jnp.where(next_start < nb, jnp.sum(jnp.where(picks, block_e[None, :], 0), axis=1), -1)
    return first, slot.astype(jnp.int32), next_e.astype(jnp.int32)


def _pad_fill_copies(live_ref, i, zeros, xb_hbm, sem):
    lv = live_ref[i]
    pad = EXPERT_TM - lv
    row0 = i * EXPERT_TM + lv
    copies = []
    piece = EXPERT_TM
    while piece >= 1:
        offset = pad & ~(2 * piece - 1)
        start = pl.multiple_of((row0 + offset) * ROW_TILE, ROW_TILE)
        copy = pltpu.make_async_copy(zeros.at[pl.ds(0, piece * ROW_TILE), :],
                                     xb_hbm.at[pl.ds(start, piece * ROW_TILE), :], sem)
        copies.append(((pad & piece) != 0, copy))
        piece //= 2
    return copies


def _run_piece_copies(src_ref, len_ref, dst_ref, run, sorted_hbm, stage, sem, to_stage):
    length = len_ref[run]
    src0 = src_ref[run]
    dst0 = dst_ref[run]
    copies = []
    piece = ROUTE_TT
    while piece >= 1:
        offset = length & ~(2 * piece - 1)
        hbm_rows = pl.ds(pl.multiple_of((src0 + offset) * ROW_TILE, ROW_TILE), piece * ROW_TILE)
        stage_rows = pl.ds(pl.multiple_of((dst0 + offset) * ROW_TILE, ROW_TILE), piece * ROW_TILE)
        ends = (sorted_hbm.at[hbm_rows, :], stage.at[stage_rows, :])
        copy = pltpu.make_async_copy(*(ends if to_stage else ends[::-1]), sem)
        copies.append(((length & piece) != 0, copy))
        piece //= 2
    return copies


def _start_pieces(copies):
    for n, (wanted, copy) in enumerate(copies):
        pl.when(wanted)(functools.partial(copy.start, priority=n % 2))


def _dispatch_kernel(src_ref, len_ref, dst_ref, live_ref, h_ref, loc_ref, xb_hbm, stage, zeros, sem, fill_sem):
    tt = loc_ref.shape[0]
    step = pl.program_id(0)
    n_steps = pl.num_programs(0)
    slot = step % 2
    n_blocks = xb_hbm.shape[0] // (EXPERT_TM * ROW_TILE)
    whole_tile = pl.ds(0, TOP_K * tt * ROW_TILE)

    @pl.when(step == 0)
    def _():
        zeros[...] = jnp.zeros(zeros.shape, F32)

        def start_fill(i, carry):
            for live_bit, copy in _pad_fill_copies(live_ref, i, zeros, xb_hbm, fill_sem):
                pl.when(live_bit)(copy.start)
            return carry

        lax.fori_loop(0, n_blocks, start_fill, 0)

    def wait_tile_writes(s):
        pltpu.make_async_copy(stage.at[s], xb_hbm.at[whole_tile, :], sem.at[s]).wait()

    @pl.when(step >= 2)
    def _():
        wait_tile_writes(slot)

    h = _tiles_to_rows(h_ref).astype(BF16)
    loc = loc_ref[...]
    pos = lax.broadcasted_iota(jnp.int32, (tt, TOP_K * tt), 1)
    one_hot = pos == loc[:, 0:1]
    for k in range(1, TOP_K):
        one_hot = one_hot | (pos == loc[:, k:k + 1])
    packed = lax.dot_general(one_hot.astype(BF16), h, (((0,), (0,)), ((), ())),
                             preferred_element_type=F32)
    _rows_to_tiles(stage.at[slot], packed)

    def per_expert(e, carry):
        copies = _run_piece_copies(src_ref, len_ref, dst_ref, step * N_EXPERTS + e, xb_hbm,
                                   stage.at[slot], sem.at[slot], to_stage=False)
        _start_pieces(copies)
        return carry

    lax.fori_loop(0, N_EXPERTS, per_expert, 0)

    @pl.when(step == n_steps - 1)
    def _():
        wait_tile_writes(slot)

        @pl.when(n_steps >= 2)
        def _():
            wait_tile_writes(1 - slot)

        def wait_fill(i, carry):
            for live_bit, copy in _pad_fill_copies(live_ref, i, zeros, xb_hbm, fill_sem):
                pl.when(live_bit)(copy.wait)
            return carry

        lax.fori_loop(0, n_blocks, wait_fill, 0)


def _run_tables(pstarts, tile_runs):
    run_src = (pstarts[None, :] + tile_runs[:, 0, :N_EXPERTS]).reshape(-1)
    run_len = tile_runs[:, 1, :N_EXPERTS].reshape(-1)
    run_dst = tile_runs[:, 2, :N_EXPERTS].reshape(-1)
    return run_src, run_len, run_dst


def _dispatch(run_tables, live, h3, loc, n_rows):
    t = h3.shape[0] // ROW_TILE
    tt = ROUTE_TT
    tile_map = lambda i, *_: (i, 0)
    return pl.pallas_call(
        _dispatch_kernel,
        grid_spec=pltpu.PrefetchScalarGridSpec(
            num_scalar_prefetch=4,
            grid=(t // tt,),
            in_specs=[pl.BlockSpec((tt * ROW_TILE, LANES), tile_map),
                      pl.BlockSpec((tt, LANES), tile_map)],
            out_specs=pl.BlockSpec(memory_space=pl.ANY),
            scratch_shapes=[pltpu.VMEM((2, TOP_K * tt * ROW_TILE, LANES), F32),
                            pltpu.VMEM((EXPERT_TM * ROW_TILE, LANES), F32),
                            pltpu.SemaphoreType.DMA((2,)), pltpu.SemaphoreType.DMA],
        ),
        out_shape=jax.ShapeDtypeStruct((n_rows * ROW_TILE, LANES), F32),
        compiler_params=_cparams(("arbitrary",)),
        name="moe_dispatch",
    )(*run_tables, live, h3, loc)


def _expert_kernel(be_ref, live_ref, first_ref, slot_ref, next_ref, x_ref, w1_hbm, b1_ref, w2_hbm, b2_ref,
                   y_ref, w1f, w2f, w1s, w2s, sem, *, layer):
    i = pl.program_id(0)
    live = live_ref[i]

    def weight_copies(e, slot):
        return (pltpu.make_async_copy(w1_hbm.at[layer, e], w1f.at[slot], sem.at[0, slot]),
                pltpu.make_async_copy(w2_hbm.at[layer, e], w2f.at[slot], sem.at[1, slot]))

    @pl.when(i == 0)
    def _():
        for copy in weight_copies(be_ref[0], 0):
            copy.start()

    @pl.when(first_ref[i] == 1)
    def _():
        slot = slot_ref[i]
        for copy in weight_copies(be_ref[i], slot):
            copy.wait()
        nxt = next_ref[i]

        @pl.when(nxt >= 0)
        def _():
            for copy in weight_copies(nxt, 1 - slot):
                copy.start()

        w1s[...] = w1f[slot].astype(BF16)
        w2s[...] = w2f[slot].astype(BF16)

    @pl.when(live > 0)
    def _():
        dff = w2s.shape[0]
        x = _tiles_to_rows(x_ref)
        row = lax.broadcasted_iota(jnp.int32, (x.shape[0], 1), 0)
        x = jnp.where(row < live, x, 0.0).astype(BF16)
        hh = jnp.dot(x, w1s[...], preferred_element_type=F32) + b1_ref[0, 0]
        x_glu = jnp.minimum(hh[:, :dff], SWIGLU_LIMIT)
        x_lin = jnp.clip(hh[:, dff:], -SWIGLU_LIMIT, SWIGLU_LIMIT)
        act = x_glu * _sigmoid(SWIGLU_ALPHA * x_glu) * (x_lin + 1.0)
        y = jnp.dot(act.astype(BF16), w2s[...], preferred_element_type=F32) + b2_ref[0, 0]
        _rows_to_tiles(y_ref, y)

    @pl.when(live == 0)
    def _():
        y_ref[...] = jnp.zeros(y_ref.shape, F32)


def _experts(block_e, live, xb, layer, w1, b1, w2, b2):
    depth, n_e, d, two_f = w1.shape
    dff = two_f // 2
    block_rows = EXPERT_TM * ROW_TILE
    nb = xb.shape[0] // block_rows
    first, slot, next_e = _expert_run_tables(block_e)
    bias_map = lambda i, be, *_: (layer, be[i], 0, 0)
    return pl.pallas_call(
        functools.partial(_expert_kernel, layer=layer),
        grid_spec=pltpu.PrefetchScalarGridSpec(
            num_scalar_prefetch=5,
            grid=(nb,),
            in_specs=[
                pl.BlockSpec((block_rows, LANES), lambda i, *_: (i, 0)),
                pl.BlockSpec(memory_space=pl.ANY),
                pl.BlockSpec((1, 1, 1, two_f), bias_map),
                pl.BlockSpec(memory_space=pl.ANY),
                pl.BlockSpec((1, 1, 1, d), bias_map),
            ],
            out_specs=pl.BlockSpec((block_rows, LANES), lambda i, *_: (i, 0)),
            scratch_shapes=[pltpu.VMEM((2, d, two_f), F32), pltpu.VMEM((2, dff, d), F32),
                            pltpu.VMEM((d, two_f), BF16), pltpu.VMEM((dff, d), BF16),
                            pltpu.SemaphoreType.DMA((2, 2))],
        ),
        out_shape=jax.ShapeDtypeStruct(xb.shape, F32),
        compiler_params=_cparams(("arbitrary",)),
        name="moe_experts",
    )(block_e, live, first, slot, next_e, xb, w1, b1.reshape(depth, n_e, 1, two_f), w2,
      b2.reshape(depth, n_e, 1, d))


def _combine_kernel(src_ref, len_ref, dst_ref, yb_hbm, loc_ref, gate_ref, x_ref, mod_ref, lng_ref, lnb_ref,
                    *rest, with_pw1):
    if with_pw1:
        modn_ref, wa_ref, wb_ref, ba_ref, bb_ref, o_ref, u_ref, stage, sem = rest
    else:
        o_ref, stage, sem = rest
    tt = x_ref.shape[0]
    step = pl.program_id(0)
    n_steps = pl.num_programs(0)

    def start_gather(tile, slot):
        def per_expert(e, carry):
            copies = _run_piece_copies(src_ref, len_ref, dst_ref, tile * N_EXPERTS + e, yb_hbm,
                                       stage.at[slot], sem.at[slot], to_stage=True)
            _start_pieces(copies)
            return carry

        lax.fori_loop(0, N_EXPERTS, per_expert, 0)

    slot = step % 2

    @pl.when(step == 0)
    def _():
        start_gather(step, slot)

    @pl.when(step + 1 < n_steps)
    def _():
        start_gather(step + 1, 1 - slot)

    pltpu.make_async_copy(yb_hbm.at[pl.ds(0, TOP_K * tt * ROW_TILE), :], stage.at[slot], sem.at[slot]).wait()

    staged = _tiles_to_rows(stage.at[slot]).astype(BF16)
    loc = loc_ref[...]
    gates = gate_ref[...]
    pos = lax.broadcasted_iota(jnp.int32, (tt, TOP_K * tt), 1)
    pick = jnp.zeros((tt, TOP_K * tt), F32)
    for k in range(TOP_K):
        pick = jnp.where(pos == loc[:, k:k + 1], gates[:, k:k + 1], pick)
    y = jnp.dot(pick.astype(BF16), staged, preferred_element_type=F32)
    x_new = _residual_ln(x_ref[...], y, mod_ref[0, 5:6, :], lng_ref[...], lnb_ref[...])
    o_ref[...] = x_new
    if with_pw1:
        _pw1_glu_store(x_new, modn_ref, wa_ref, wb_ref, ba_ref, bb_ref, u_ref)


def _combine(run_tables, yb, loc, gates, x1, mod, ln_g, ln_b, seq, next_pw1=None):
    t, d = x1.shape
    tt = ROUTE_TT
    per_b = seq // tt
    const = lambda i, *_: (0, 0)
    tile_map = lambda i, *_: (i, 0)
    in_specs = [
        pl.BlockSpec(memory_space=pl.ANY),
        pl.BlockSpec((tt, LANES), tile_map),
        pl.BlockSpec((tt, LANES), tile_map),
        pl.BlockSpec((tt, d), tile_map),
        pl.BlockSpec((1, 6, d), lambda i, *_: (i // per_b, 0, 0)),
        pl.BlockSpec((1, d), const),
        pl.BlockSpec((1, d), const),
    ]
    operands = [*run_tables, yb, loc, gates, x1, mod, ln_g.reshape(1, d), ln_b.reshape(1, d)]
    out_specs = [pl.BlockSpec((tt, d), tile_map)]
    out_shape = [jax.ShapeDtypeStruct((t, d), F32)]
    if next_pw1 is not None:
        mod_next, pw1_w, pw1_b = next_pw1
        inner = pw1_w.shape[1] // 2
        assert inner == ROW_TILE * LANES
        w = pw1_w.astype(BF16)
        b = pw1_b.reshape(1, 2 * inner)
        in_specs += [
            pl.BlockSpec((1, 6, d), lambda i, *_: (i // per_b, 0, 0)),
            pl.BlockSpec((d, inner), const),
            pl.BlockSpec((d, inner), lambda i, *_: (0, 1)),
            pl.BlockSpec((1, inner), const),
            pl.BlockSpec((1, inner), lambda i, *_: (0, 1)),
        ]
        operands += [mod_next, w, w, b, b]
        out_specs.append(pl.BlockSpec((tt * ROW_TILE, LANES), tile_map))
        out_shape.append(jax.ShapeDtypeStruct((t * ROW_TILE, LANES), F32))
    return pl.pallas_call(
        functools.partial(_combine_kernel, with_pw1=next_pw1 is not None),
        grid_spec=pltpu.PrefetchScalarGridSpec(
            num_scalar_prefetch=3,
            grid=(t // tt,),
            in_specs=in_specs,
            out_specs=out_specs,
            scratch_shapes=[pltpu.VMEM((2, TOP_K * tt * ROW_TILE, LANES), F32), pltpu.SemaphoreType.DMA((2,))],
        ),
        out_shape=out_shape,
        compiler_params=_cparams(("arbitrary",)),
        name="moe_combine_ln",
    )(*operands)


def _moe_block(x1, h3, idx, gates, mod, layer, w1, b1, w2, b2, ln_g, ln_b, seq, next_pw1=None):
    t = x1.shape[0]
    loc, tile_runs, counts = _rank_and_count(idx)
    pstarts, block_e, live = _routing_tables(t, counts)
    run_tables = _run_tables(pstarts, tile_runs)
    n_rows = _num_expert_blocks(t * TOP_K) * EXPERT_TM
    xb = _dispatch(run_tables, live, h3, loc, n_rows)
    yb = _experts(block_e, live, xb, layer, w1, b1, w2, b2)
    return _combine(run_tables, yb, loc, gates, x1, mod, ln_g, ln_b, seq, next_pw1)


def kernel(x, c, ada_w, ada_b, dn_in_w, dn_conv_w, dn_A_log, dn_dt_bias, dn_onorm_w, dn_out_w,
           cf_pw1_w, cf_pw1_b, cf_dw_w, cf_dw_b, cf_ln_g, cf_ln_b, cf_pw2_w, cf_pw2_b,
           ln1_g, ln1_b, router_w, router_b, e_w1, e_b1, e_w2, e_b2, ln2_g, ln2_b):
    bsz, seq, d = x.shape
    mods = _ada_ln(c, ada_w, ada_b)

    proj, ba = _dn_in_proj(x, mods[0], dn_in_w[0])
    qn, kn, vv, gates_dn = _dn_prep(proj, ba, dn_conv_w[0], dn_A_log[0], dn_dt_bias[0])
    o = _dn_chunk(qn, kn, vv, proj, gates_dn, dn_onorm_w[0])
    x1, h3, idx, gates = _dn_out(o, dn_out_w[0], x, mods[0], ln1_g[0], ln1_b[0], router_w[0], router_b[0])
    x2, u = _moe_block(x1, h3, idx, gates, mods[0], 0, e_w1, e_b1, e_w2, e_b2, ln2_g[0], ln2_b[0], seq,
                       next_pw1=(mods[1], cf_pw1_w[0], cf_pw1_b[0]))

    x2 = x2.reshape(bsz, seq, d)
    x3, h3, idx, gates = _cf_tail(u, cf_dw_w[0], cf_dw_b[0], cf_ln_g[0], cf_ln_b[0], cf_pw2_w[0], cf_pw2_b[0],
                                  x2, mods[1], ln1_g[1], ln1_b[1], router_w[1], router_b[1])
    (x4,) = _moe_block(x3, h3, idx, gates, mods[1], 1, e_w1, e_b1, e_w2, e_b2, ln2_g[1], ln2_b[1], seq)
    return x4.reshape(bsz, seq, d)
```
